```python
import math
import jax, jax.numpy as jnp
from jax import lax
import numpy as np

D_MODEL = 2048
BATCH = 16
SEQ = 256
DEPTH = 2
DEC_BATCH = 2
DEC_SEQ = 1024
PAST_LEN = 256

GRID_W = 64
HEAD_DIM = 128
N_Q_HEADS = 8
N_KV_HEADS = 2
Q_GROUP = N_Q_HEADS // N_KV_HEADS
Q_WIDTH = N_Q_HEADS * HEAD_DIM
KV_WIDTH = N_KV_HEADS * HEAD_DIM
Q_BLOCK = 128
ROPE_THETA = 10000.0
LRU_WIDTH = D_MODEL // 2
LRU_BLOCKS = 8
LRU_BLOCK = LRU_WIDTH // LRU_BLOCKS
LRU_CONV_W = 4
LRU_C = 8.0
MIX0_IN = Q_WIDTH + 2 * KV_WIDTH + 2 * LRU_WIDTH
MIX0_OUT = Q_WIDTH + LRU_WIDTH
HYENA_ORDER = 2
HYENA_WIDTH = D_MODEL
HYENA_SHORT_W = 3
HYENA_EMB = 33
HYENA_BANDS = (HYENA_EMB - 1) // 2
HYENA_FILTER_HIDDEN = 64
D_FF = 5632
N_EXPERTS = 8
TOP_K = 2
D_FF_EXPERT = 7168
N_MOD = 6
LN_EPS = 1e-5
QK_EPS = 1e-6
DN_ALPHA = (2 * DEPTH) ** 0.25
DN_BETA = (8 * DEPTH) ** -0.25

kernel_name = 'hybrid_diffusion_prefix_step'


def layer_norm(x, g, b):
    xf = x.astype(jnp.float32)
    mu = jnp.mean(xf, -1, keepdims=True)
    var = jnp.mean(jnp.square(xf - mu), -1, keepdims=True)
    return ((xf - mu) * lax.rsqrt(var + LN_EPS) * g + b).astype(x.dtype)


def rms_norm(x, g):
    xf = x.astype(jnp.float32)
    return (xf * lax.rsqrt(jnp.mean(xf * xf, -1, keepdims=True) + QK_EPS) * g).astype(x.dtype)


def ada_modulation(cond, w, b):
    m = jnp.einsum('nd,de->ne', jax.nn.silu(cond), w) + b
    return [t[:, None, :] for t in jnp.split(m, N_MOD, axis=-1)]


def modulate(x, shift, scale):
    return x * (1 + scale) + shift


def post_norm(x, delta, gate, g, b):
    return layer_norm(DN_ALPHA * x + gate * delta, g, b)


def depthwise_conv(x, w, b):
    width = w.shape[0]
    left = (width - 1) // 2
    y = lax.conv_general_dilated(x, w[:, None, :].astype(x.dtype), window_strides=(1,),
                                 padding=[(left, width - 1 - left)],
                                 dimension_numbers=('NWC', 'WIO', 'NWC'),
                                 feature_group_count=x.shape[-1])
    return y + b


def axial_rope(x):
    L = x.shape[1]
    t = jnp.arange(L)
    row = (t // GRID_W).astype(jnp.float32)
    col = (t % GRID_W).astype(jnp.float32)
    n_freq = HEAD_DIM // 4
    inv = 1.0 / (ROPE_THETA ** (jnp.arange(n_freq, dtype=jnp.float32) / n_freq))
    ang = jnp.concatenate([row[:, None] * inv, col[:, None] * inv], -1)
    cos = jnp.cos(ang)[None, :, None, :]
    sin = jnp.sin(ang)[None, :, None, :]
    xf = x.astype(jnp.float32).reshape(x.shape[:-1] + (HEAD_DIM // 2, 2))
    x0, x1 = xf[..., 0], xf[..., 1]
    out = jnp.stack([x0 * cos - x1 * sin, x0 * sin + x1 * cos], -1)
    return out.reshape(x.shape).astype(x.dtype)


def block_attention(q, k, v):
    B, Lq, _, hd = q.shape
    qb = q.reshape(B, Lq // Q_BLOCK, Q_BLOCK, N_KV_HEADS, Q_GROUP, hd).transpose(1, 0, 2, 3, 4, 5)
    scale = hd ** -0.5

    def one_block(q_blk):
        s = jnp.einsum('bqhgd,bkhd->bhgqk', q_blk, k, preferred_element_type=jnp.float32) * scale
        p = jax.nn.softmax(s, axis=-1).astype(v.dtype)
        return jnp.einsum('bhgqk,bkhd->bqhgd', p, v)

    o = lax.map(one_block, qb)
    return o.transpose(1, 0, 2, 3, 4, 5).reshape(B, Lq, N_Q_HEADS * hd)


def block_diag_linear(x, w, b):
    xb = x.reshape(x.shape[:-1] + (LRU_BLOCKS, LRU_BLOCK))
    return jnp.einsum('blhi,hij->blhj', xb, w).reshape(x.shape) + b


def _linear_combine(e1, e2):
    a1, b1 = e1
    a2, b2 = e2
    return a1 * a2, a2 * b1 + b2


def rglru(x, lam, w_r, b_r, w_i, b_i, h0, reverse):
    if reverse:
        x = jnp.flip(x, 1)
    r = jax.nn.sigmoid(block_diag_linear(x, w_r, b_r).astype(jnp.float32))
    i = jax.nn.sigmoid(block_diag_linear(x, w_i, b_i).astype(jnp.float32))
    log_a = -LRU_C * r * jax.nn.softplus(-lam.astype(jnp.float32))
    a = jnp.exp(log_a)
    b = jnp.sqrt(-jnp.expm1(2.0 * log_a)) * (i * x.astype(jnp.float32))
    b = b.at[:, 0].add(a[:, 0] * h0.astype(jnp.float32))
    _, h = lax.associative_scan(_linear_combine, (a, b), axis=1)
    return jnp.flip(h, 1) if reverse else h


def _ab_project(u, p):
    B, L, _ = u.shape
    proj = jnp.einsum('bld,de->ble', u, p['w_in'])
    q, k, v, xb, gb = jnp.split(proj, [Q_WIDTH, Q_WIDTH + KV_WIDTH, Q_WIDTH + 2 * KV_WIDTH,
                                      Q_WIDTH + 2 * KV_WIDTH + LRU_WIDTH], axis=-1)
    q = rms_norm(q.reshape(B, L, N_Q_HEADS, HEAD_DIM), p['q_norm'])
    k = rms_norm(k.reshape(B, L, N_KV_HEADS, HEAD_DIM), p['k_norm'])
    v = v.reshape(B, L, N_KV_HEADS, HEAD_DIM)
    xc = depthwise_conv(xb, p['conv_w'], p['conv_b'])
    return q, k, v, xc, gb


def _lru_bidir(xc, gb, p, h0):
    hf = rglru(xc, p['lam'][0], p['w_r'][0], p['b_r'][0], p['w_i'][0], p['b_i'][0], h0[:, 0], False)
    hb = rglru(xc, p['lam'][1], p['w_r'][1], p['b_r'][1], p['w_i'][1], p['b_i'][1], h0[:, 1], True)
    y = (hf + hb).astype(xc.dtype) * jax.nn.gelu(gb)
    return y, jnp.stack([hf[:, -1], hb[:, 0]], axis=1)


def mixer_ab_context(u, p):
    q, k, v, xc, gb = _ab_project(u, p)
    attn = block_attention(q, k, v)
    h0 = jnp.zeros((u.shape[0], 2, LRU_WIDTH), jnp.float32)
    lru, h_fin = _lru_bidir(xc, gb, p, h0)
    out = jnp.einsum('ble,ed->bld', jnp.concatenate([attn, lru], -1), p['w_out'])
    return out, k, v, h_fin


def mixer_ab_latent(u, p, ctx_k, ctx_v, ctx_h):
    q, k, v, xc, gb = _ab_project(u, p)
    q, k = axial_rope(q), axial_rope(k)
    attn = block_attention(q, jnp.concatenate([ctx_k, k], 1), jnp.concatenate([ctx_v, v], 1))
    lru, _ = _lru_bidir(xc, gb, p, ctx_h)
    return jnp.einsum('ble,ed->bld', jnp.concatenate([attn, lru], -1), p['w_out'])


def hyena_kernel_fft(L, p):
    t = jnp.arange(L, dtype=jnp.float32)
    t01 = t / (L - 1)
    w = 2.0 * math.pi * t / L
    f = jnp.linspace(1e-4, HYENA_BANDS - 1, HYENA_BANDS, dtype=jnp.float32)
    fw = w[:, None] * f[None, :]
    feat = jnp.concatenate([t01[:, None], jnp.cos(fw), -jnp.sin(fw)], -1)
    h = jnp.sin(p['filt_f1'] * (feat @ p['filt_w1'] + p['filt_b1']))
    h = jnp.sin(p['filt_f2'] * (h @ p['filt_w2'] + p['filt_b2']))
    h = (h @ p['filt_w3']).reshape(L, 2, HYENA_ORDER, HYENA_WIDTH)
    h = h * jnp.exp(-t01[:, None, None, None] * jnp.abs(p['filt_decay']))
    zero = jnp.zeros((1, HYENA_ORDER, HYENA_WIDTH), h.dtype)
    k2 = jnp.concatenate([h[:, 0], zero, jnp.flip(h[1:, 1], 0)], 0)
    return jnp.fft.rfft(k2.astype(jnp.float32), axis=0)


def fft_long_conv(u, k_f, bias):
    L = u.shape[1]
    uf = u.astype(jnp.float32)
    y = jnp.fft.irfft(jnp.fft.rfft(uf, n=2 * L, axis=1) * k_f, n=2 * L, axis=1)[:, :L]
    return (y + uf * bias).astype(u.dtype)


def mixer_hyena(u, p):
    L = u.shape[1]
    proj = depthwise_conv(jnp.einsum('bld,de->ble', u, p['w_in']), p['short_w'], p['short_b'])
    x1, x2, z = jnp.split(proj, 3, axis=-1)
    k_f = hyena_kernel_fft(L, p)
    for n, gate in enumerate((x1, x2)):
        z = gate * fft_long_conv(z, k_f[:, n], p['filt_bias'][n])
    return jnp.einsum('ble,ed->bld', z, p['w_out'])


def swiglu(u, p):
    h = jax.nn.silu(u @ p['ffn_w1']) * (u @ p['ffn_w3'])
    return h @ p['ffn_w2']


def moe_swiglu(u, p):
    logits = jnp.einsum('bld,de->ble', u, p['router']).astype(jnp.float32)
    top_v, top_i = lax.top_k(logits, TOP_K)
    probs = jax.nn.softmax(top_v, axis=-1)
    combine = jnp.sum(jax.nn.one_hot(top_i, N_EXPERTS, dtype=jnp.float32) * probs[..., None], axis=-2)
    out = jnp.zeros(u.shape, jnp.float32)
    for e in range(N_EXPERTS):
        h = jax.nn.silu(u @ p['exp_w1'][e]) * (u @ p['exp_w3'][e])
        out = out + combine[..., e:e + 1] * (h @ p['exp_w2'][e])
    return out.astype(u.dtype)


def setup_inputs(seed: int = 0) -> dict:
    key = jax.random.key(seed)
    ks = iter(jax.random.split(key, 64))
    f32 = jnp.float32
    D = D_MODEL

    def nrm(shape, scale):
        return jax.random.normal(next(ks), shape, f32) * scale

    def gain(n):
        return 1.0 + nrm((n,), 0.02)

    a8 = jax.random.uniform(next(ks), (2, LRU_WIDTH), f32, 0.9, 0.999)
    a = a8 ** (1.0 / LRU_C)
    lru_lambda = jnp.log(a) - jnp.log1p(-a)
    decay = jax.random.uniform(next(ks), (2, HYENA_ORDER, HYENA_WIDTH), f32,
                               math.log(100.0) / 1.5, math.log(100.0) / 0.3)
    return {
        'x_prompt': nrm((BATCH, SEQ, D), 1.0),
        'x_sample': nrm((DEC_BATCH, DEC_SEQ, D), 1.0),
        'c': nrm((DEC_BATCH, D), 1.0),
        'c_ctx': nrm((D,), 1.0),
        'cache_l0_k': nrm((DEC_BATCH, PAST_LEN, N_KV_HEADS, HEAD_DIM), 1.0),
        'cache_l0_v': nrm((DEC_BATCH, PAST_LEN, N_KV_HEADS, HEAD_DIM), 1.0),
        'state_l0_lru': nrm((DEC_BATCH, 2, LRU_WIDTH), 0.5),
        'l0_ada_w': nrm((D, N_MOD * D), D ** -0.5),
        'l0_ada_b': nrm((N_MOD * D,), 0.02),
        'l0_w_in': nrm((D, MIX0_IN), D ** -0.5),
        'l0_q_norm': gain(HEAD_DIM),
        'l0_k_norm': gain(HEAD_DIM),
        'l0_lru_conv_w': nrm((LRU_CONV_W, LRU_WIDTH), LRU_CONV_W ** -0.5),
        'l0_lru_conv_b': nrm((LRU_WIDTH,), 0.02),
        'l0_lru_lambda': lru_lambda,
        'l0_lru_w_r': nrm((2, LRU_BLOCKS, LRU_BLOCK, LRU_BLOCK), LRU_BLOCK ** -0.5),
        'l0_lru_b_r': nrm((2, LRU_WIDTH), 0.02),
        'l0_lru_w_i': nrm((2, LRU_BLOCKS, LRU_BLOCK, LRU_BLOCK), LRU_BLOCK ** -0.5),
        'l0_lru_b_i': nrm((2, LRU_WIDTH), 0.02),
        'l0_w_out': nrm((MIX0_OUT, D), MIX0_OUT ** -0.5 * DN_BETA),
        'l0_ln1_g': gain(D),
        'l0_ln1_b': nrm((D,), 0.02),
        'l0_ffn_w1': nrm((D, D_FF), D ** -0.5),
        'l0_ffn_w3': nrm((D, D_FF), D ** -0.5),
        'l0_ffn_w2': nrm((D_FF, D), D_FF ** -0.5 * DN_BETA),
        'l0_ln2_g': gain(D),
        'l0_ln2_b': nrm((D,), 0.02),
        'l1_ada_w': nrm((D, N_MOD * D), D ** -0.5),
        'l1_ada_b': nrm((N_MOD * D,), 0.02),
        'l1_w_in': nrm((D, 3 * HYENA_WIDTH), D ** -0.5),
        'l1_short_w': nrm((HYENA_SHORT_W, 3 * HYENA_WIDTH), HYENA_SHORT_W ** -0.5),
        'l1_short_b': nrm((3 * HYENA_WIDTH,), 0.02),
        'l1_filt_w1': nrm((HYENA_EMB, HYENA_FILTER_HIDDEN), HYENA_EMB ** -0.5),
        'l1_filt_b1': nrm((HYENA_FILTER_HIDDEN,), 0.02),
        'l1_filt_f1': gain(HYENA_FILTER_HIDDEN),
        'l1_filt_w2': nrm((HYENA_FILTER_HIDDEN, HYENA_FILTER_HIDDEN), HYENA_FILTER_HIDDEN ** -0.5),
        'l1_filt_b2': nrm((HYENA_FILTER_HIDDEN,), 0.02),
        'l1_filt_f2': gain(HYENA_FILTER_HIDDEN),
        'l1_filt_w3': nrm((HYENA_FILTER_HIDDEN, 2 * HYENA_ORDER * HYENA_WIDTH), 0.1 * HYENA_FILTER_HIDDEN ** -0.5),
        'l1_filt_decay': decay,
        'l1_filt_bias': nrm((HYENA_ORDER, HYENA_WIDTH), 0.5),
        'l1_w_out': nrm((HYENA_WIDTH, D), HYENA_WIDTH ** -0.5 * DN_BETA),
        'l1_ln1_g': gain(D),
        'l1_ln1_b': nrm((D,), 0.02),
        'l1_router': nrm((D, N_EXPERTS), D ** -0.5),
        'l1_exp_w1': nrm((N_EXPERTS, D, D_FF_EXPERT), D ** -0.5),
        'l1_exp_w3': nrm((N_EXPERTS, D, D_FF_EXPERT), D ** -0.5),
        'l1_exp_w2': nrm((N_EXPERTS, D_FF_EXPERT, D), D_FF_EXPERT ** -0.5 * DN_BETA),
        'l1_ln2_g': gain(D),
        'l1_ln2_b': nrm((D,), 0.02),
    }


def reference(x_prompt, x_sample, c, c_ctx, cache_l0_k, cache_l0_v, state_l0_lru,
              l0_ada_w, l0_ada_b, l0_w_in, l0_q_norm, l0_k_norm, l0_lru_conv_w, l0_lru_conv_b,
              l0_lru_lambda, l0_lru_w_r, l0_lru_b_r, l0_lru_w_i, l0_lru_b_i, l0_w_out,
              l0_ln1_g, l0_ln1_b, l0_ffn_w1, l0_ffn_w3, l0_ffn_w2, l0_ln2_g, l0_ln2_b,
              l1_ada_w, l1_ada_b, l1_w_in, l1_short_w, l1_short_b,
              l1_filt_w1, l1_filt_b1, l1_filt_f1, l1_filt_w2, l1_filt_b2, l1_filt_f2, l1_filt_w3,
              l1_filt_decay, l1_filt_bias, l1_w_out, l1_ln1_g, l1_ln1_b,
              l1_router, l1_exp_w1, l1_exp_w3, l1_exp_w2, l1_ln2_g, l1_ln2_b):
    params = [
        dict(ada_w=l0_ada_w, ada_b=l0_ada_b, w_in=l0_w_in, q_norm=l0_q_norm, k_norm=l0_k_norm,
             conv_w=l0_lru_conv_w, conv_b=l0_lru_conv_b, lam=l0_lru_lambda,
             w_r=l0_lru_w_r, b_r=l0_lru_b_r, w_i=l0_lru_w_i, b_i=l0_lru_b_i, w_out=l0_w_out,
             ln1_g=l0_ln1_g, ln1_b=l0_ln1_b, ffn_w1=l0_ffn_w1, ffn_w3=l0_ffn_w3, ffn_w2=l0_ffn_w2,
             ln2_g=l0_ln2_g, ln2_b=l0_ln2_b),
        dict(ada_w=l1_ada_w, ada_b=l1_ada_b, w_in=l1_w_in, short_w=l1_short_w, short_b=l1_short_b,
             filt_w1=l1_filt_w1, filt_b1=l1_filt_b1, filt_f1=l1_filt_f1,
             filt_w2=l1_filt_w2, filt_b2=l1_filt_b2, filt_f2=l1_filt_f2, filt_w3=l1_filt_w3,
             filt_decay=l1_filt_decay, filt_bias=l1_filt_bias, w_out=l1_w_out,
             ln1_g=l1_ln1_g, ln1_b=l1_ln1_b, router=l1_router,
             exp_w1=l1_exp_w1, exp_w3=l1_exp_w3, exp_w2=l1_exp_w2,
             ln2_g=l1_ln2_g, ln2_b=l1_ln2_b),
    ]
    context_cache = [(cache_l0_k, cache_l0_v, state_l0_lru), None]
    cond_ctx = c_ctx[None, :]
    xp, xs = x_prompt, x_sample
    new_k = new_v = new_h = None
    for layer in range(DEPTH):
        p = params[layer]
        mp = ada_modulation(cond_ctx, p['ada_w'], p['ada_b'])
        ms = ada_modulation(c, p['ada_w'], p['ada_b'])
        up, us = modulate(xp, mp[0], mp[1]), modulate(xs, ms[0], ms[1])
        if layer % 2 == 0:
            ck, cv, ch = context_cache[layer]
            dp, new_k, new_v, new_h = mixer_ab_context(up, p)
            ds = mixer_ab_latent(us, p, ck, cv, ch)
        else:
            dp, ds = mixer_hyena(up, p), mixer_hyena(us, p)
        xp = post_norm(xp, dp, mp[2], p['ln1_g'], p['ln1_b'])
        xs = post_norm(xs, ds, ms[2], p['ln1_g'], p['ln1_b'])
        up, us = modulate(xp, mp[3], mp[4]), modulate(xs, ms[3], ms[4])
        if layer % 2 == 0:
            dp, ds = swiglu(up, p), swiglu(us, p)
        else:
            dp, ds = moe_swiglu(up, p), moe_swiglu(us, p)
        xp = post_norm(xp, dp, mp[5], p['ln2_g'], p['ln2_b'])
        xs = post_norm(xs, ds, ms[5], p['ln2_g'], p['ln2_b'])
    return (xp, xs, new_k, new_v, new_h)
```

```python
import functools
import math

import jax
import jax.numpy as jnp
from jax import lax
from jax.experimental import pallas as pl
from jax.experimental.pallas import tpu as pltpu

F32 = jnp.float32
BF16 = jnp.bfloat16

D_MODEL = 2048
BATCH = 16
SEQ = 256
DEC_BATCH = 2
DEC_SEQ = 1024
PAST_LEN = 256
GRID_W = 64
HEAD_DIM = 128
N_Q_HEADS = 8
N_KV_HEADS = 2
Q_GROUP = N_Q_HEADS // N_KV_HEADS
Q_WIDTH = N_Q_HEADS * HEAD_DIM
KV_WIDTH = N_KV_HEADS * HEAD_DIM
ROPE_THETA = 10000.0
LRU_WIDTH = D_MODEL // 2
LRU_BLOCK = 128
LRU_C = 8.0
MIX0_IN = Q_WIDTH + 2 * KV_WIDTH + 2 * LRU_WIDTH
HYENA_ORDER = 2
HYENA_EMB = 33
HYENA_BANDS = (HYENA_EMB - 1) // 2
HYENA_FILTER_HIDDEN = 64
D_FF = 5632
N_EXPERTS = 8
TOP_K = 2
D_FF_EXPERT = 7168
N_MOD = 6
LN_EPS = 1e-5
QK_EPS = 1e-6
DEPTH = 2
DN_ALPHA = (2 * DEPTH) ** 0.25

TOK_P = BATCH * SEQ
TOK_S = DEC_BATCH * DEC_SEQ
N_TOK = TOK_P + TOK_S
N_COND = 8
LANE = 128
SUBLANE = 8
VMEM_LIMIT = 56 * 1024 * 1024

ROW_TILE = 1024
MOE_CHUNK = 2048
MOE_ROW_TILE = 256
MOE_FF_TILE = 256
MOE_ITEMS = N_EXPERTS + (N_TOK * TOP_K) // MOE_CHUNK
MOE_ROWS = MOE_ITEMS * MOE_CHUNK


def _cparams(sem):
    return pltpu.CompilerParams(dimension_semantics=sem, vmem_limit_bytes=VMEM_LIMIT)


def _cond_of_tile(i, tm):
    return jnp.maximum(i * tm // DEC_SEQ - (TOK_P // DEC_SEQ - 1), 0)


def _mod_spec(tm, chunk, width=D_MODEL, col_of=None):
    per = D_MODEL // width
    if col_of is None:
        return pl.BlockSpec((1, 1, width), lambda i, *_: (_cond_of_tile(i, tm), 0, chunk * per))
    return pl.BlockSpec((1, 1, width),
                        lambda i, j, *_: (_cond_of_tile(i, tm), 0, chunk * per + col_of(j)))


def _silu(x):
    return x * jax.nn.sigmoid(x)


def _split_bf16(x):
    hi = x.astype(BF16)
    lo = (x - hi.astype(F32)).astype(BF16)
    return hi, lo


def _dot3(a, b):
    ah, al = _split_bf16(a)
    bh, bl = _split_bf16(b)
    d = lambda x, y: jnp.dot(x, y, preferred_element_type=F32)
    return d(ah, bh) + (d(ah, bl) + d(al, bh))


def _layer_norm_rows(y, g, b):
    mu = jnp.mean(y, -1, keepdims=True)
    yc = y - mu
    var = jnp.mean(yc * yc, -1, keepdims=True)
    return yc * lax.rsqrt(var + LN_EPS) * g + b


def _ada_kernel(c_ref, w_ref, b_ref, o_ref):
    s = _silu(c_ref[...]).astype(BF16)
    o_ref[...] = jnp.dot(s, w_ref[...].astype(BF16), preferred_element_type=F32) + b_ref[...]


def _ada(cond, w, b):
    tn = 1024
    n = w.shape[1]
    out = pl.pallas_call(
        _ada_kernel,
        grid=(n // tn,),
        in_specs=[pl.BlockSpec((N_COND, D_MODEL), lambda j: (0, 0)),
                  pl.BlockSpec((D_MODEL, tn), lambda j: (0, j)),
                  pl.BlockSpec((1, tn), lambda j: (0, j))],
        out_specs=pl.BlockSpec((N_COND, tn), lambda j: (0, j)),
        out_shape=jax.ShapeDtypeStruct((N_COND, n), F32),
        compiler_params=_cparams(("arbitrary",)),
        name="ada_modulation",
    )(cond, w, b.reshape(1, n))
    return out.reshape(N_COND, 1, n)


def _proj_kernel(x_ref, sh_ref, sc_ref, w_ref, o_ref, u_ref):
    @pl.when(pl.program_id(1) == 0)
    def _():
        u_ref[...] = (x_ref[...] * (1.0 + sc_ref[0]) + sh_ref[0]).astype(BF16)

    o_ref[...] = jnp.dot(u_ref[...], w_ref[...].astype(BF16), preferred_element_type=F32)


def _proj(x, mods, w, *, tn, name):
    tm = ROW_TILE
    n = w.shape[1]
    return pl.pallas_call(
        _proj_kernel,
        grid=(N_TOK // tm, n // tn),
        in_specs=[pl.BlockSpec((tm, D_MODEL), lambda i, j: (i, 0)),
                  _mod_spec(tm, 0), _mod_spec(tm, 1),
                  pl.BlockSpec((D_MODEL, tn), lambda i, j: (0, j))],
        out_specs=pl.BlockSpec((tm, tn), lambda i, j: (i, j)),
        out_shape=jax.ShapeDtypeStruct((N_TOK, n), F32),
        scratch_shapes=[pltpu.VMEM((tm, D_MODEL), BF16)],
        compiler_params=_cparams(("arbitrary", "arbitrary")),
        name=name,
    )(x, mods, mods, w)


def _rms(x, g):
    return x * lax.rsqrt(jnp.mean(x * x, -1, keepdims=True) + QK_EPS) * g


def _dot_nt(a, b):
    return lax.dot_general(a, b, (((1,), (1,)), ((), ())), preferred_element_type=F32)


def _rope(x, cos, sin_signed):
    lane = lax.broadcasted_iota(jnp.int32, x.shape, 1)
    partner = jnp.where(lane % 2 == 0, pltpu.roll(x, HEAD_DIM - 1, 1), pltpu.roll(x, 1, 1))
    return x * cos + partner * sin_signed


def _attn_ctx_kernel(q_ref, k_ref, v_ref, qn_ref, kn_ref, o_ref, ko_ref, vo_ref):
    scale = HEAD_DIM ** -0.5
    kn = _rms(k_ref[...], kn_ref[...])
    v = v_ref[...]
    ko_ref[...] = kn
    vo_ref[...] = v
    kb = kn.astype(BF16)
    vb = v.astype(BF16)
    for g in range(Q_GROUP):
        cols = slice(g * HEAD_DIM, (g + 1) * HEAD_DIM)
        q = _rms(q_ref[:, cols], qn_ref[...]).astype(BF16)
        s = _dot_nt(q, kb) * scale
        p = jnp.exp(s - jnp.max(s, -1, keepdims=True))
        p = p / jnp.sum(p, -1, keepdims=True)
        o = jnp.dot(p.astype(BF16), vb, preferred_element_type=F32)
        o_ref[:, cols] = o.astype(o_ref.dtype)


def _attn_context(proj, q_norm, k_norm):
    qw = Q_GROUP * HEAD_DIM
    k_blk0 = Q_WIDTH // HEAD_DIM
    v_blk0 = (Q_WIDTH + KV_WIDTH) // HEAD_DIM
    vec = pl.BlockSpec((1, HEAD_DIM), lambda b, h: (0, 0))
    return pl.pallas_call(
        _attn_ctx_kernel,
        grid=(BATCH, N_KV_HEADS),
        in_specs=[pl.BlockSpec((SEQ, qw), lambda b, h: (b, h)),
                  pl.BlockSpec((SEQ, HEAD_DIM), lambda b, h: (b, k_blk0 + h)),
                  pl.BlockSpec((SEQ, HEAD_DIM), lambda b, h: (b, v_blk0 + h)),
                  vec, vec],
        out_specs=[pl.BlockSpec((SEQ, qw), lambda b, h: (b, h)),
                   pl.BlockSpec((SEQ, HEAD_DIM), lambda b, h: (b, h)),
                   pl.BlockSpec((SEQ, HEAD_DIM), lambda b, h: (b, h))],
        out_shape=[jax.ShapeDtypeStruct((TOK_P, Q_WIDTH), BF16),
                   jax.ShapeDtypeStruct((TOK_P, KV_WIDTH), F32),
                   jax.ShapeDtypeStruct((TOK_P, KV_WIDTH), F32)],
        compiler_params=_cparams(("arbitrary", "arbitrary")),
        name="attn_context",
    )(proj, proj, proj, q_norm.reshape(1, HEAD_DIM), k_norm.reshape(1, HEAD_DIM))


ATTN_Q_ROWS = 256


def _attn_lat_kernel(q_ref, k_ref, v_ref, ck_ref, cv_ref, qn_ref, kn_ref,
                     cq_ref, sq_ref, ck_tab_ref, sk_tab_ref, o_ref):
    scale = HEAD_DIM ** -0.5
    kb = _rope(_rms(k_ref[...], kn_ref[...]), ck_tab_ref[...], sk_tab_ref[...]).astype(BF16)
    vb = v_ref[...].astype(BF16)
    ckb = ck_ref[...].astype(BF16)
    cvb = cv_ref[...].astype(BF16)
    for g in range(Q_GROUP):
        cols = slice(g * HEAD_DIM, (g + 1) * HEAD_DIM)
        q = _rope(_rms(q_ref[:, cols], qn_ref[...]), cq_ref[...], sq_ref[...]).astype(BF16)
        s1 = _dot_nt(q, ckb) * scale
        s2 = _dot_nt(q, kb) * scale
        m = jnp.maximum(jnp.max(s1, -1, keepdims=True), jnp.max(s2, -1, keepdims=True))
        p1 = jnp.exp(s1 - m)
        p2 = jnp.exp(s2 - m)
        den = jnp.sum(p1, -1, keepdims=True) + jnp.sum(p2, -1, keepdims=True)
        o = (jnp.dot((p1 / den).astype(BF16), cvb, preferred_element_type=F32)
             + jnp.dot((p2 / den).astype(BF16), vb, preferred_element_type=F32))
        o_ref[:, cols] = o.astype(o_ref.dtype)


def _rope_tables():
    t = jnp.arange(DEC_SEQ)
    row = (t // GRID_W).astype(F32)
    col = (t % GRID_W).astype(F32)
    n_freq = HEAD_DIM // 4
    inv = 1.0 / (ROPE_THETA ** (jnp.arange(n_freq, dtype=F32) / n_freq))
    ang = jnp.concatenate([row[:, None] * inv, col[:, None] * inv], -1)
    cos = jnp.repeat(jnp.cos(ang), 2, axis=-1)
    sign = jnp.where(jnp.arange(HEAD_DIM) % 2 == 0, -1.0, 1.0).astype(F32)
    sin_signed = jnp.repeat(jnp.sin(ang), 2, axis=-1) * sign
    return cos, sin_signed


def _attn_latent(proj, cache_k, cache_v, q_norm, k_norm):
    qw = Q_GROUP * HEAD_DIM
    nq = DEC_SEQ // ATTN_Q_ROWS
    q_row0 = TOK_P // ATTN_Q_ROWS
    kv_row0 = TOK_P // DEC_SEQ
    k_blk0 = Q_WIDTH // HEAD_DIM
    v_blk0 = (Q_WIDTH + KV_WIDTH) // HEAD_DIM
    cos, sin_signed = _rope_tables()
    vec = pl.BlockSpec((1, HEAD_DIM), lambda b, h, c: (0, 0))
    tab_q = pl.BlockSpec((ATTN_Q_ROWS, HEAD_DIM), lambda b, h, c: (c, 0))
    tab_k = pl.BlockSpec((DEC_SEQ, HEAD_DIM), lambda b, h, c: (0, 0))
    ctx = pl.BlockSpec((PAST_LEN, HEAD_DIM), lambda b, h, c: (b, h))
    return pl.pallas_call(
        _attn_lat_kernel,
        grid=(DEC_BATCH, N_KV_HEADS, nq),
        in_specs=[pl.BlockSpec((ATTN_Q_ROWS, qw), lambda b, h, c: (q_row0 + b * nq + c, h)),
                  pl.BlockSpec((DEC_SEQ, HEAD_DIM), lambda b, h, c: (kv_row0 + b, k_blk0 + h)),
                  pl.BlockSpec((DEC_SEQ, HEAD_DIM), lambda b, h, c: (kv_row0 + b, v_blk0 + h)),
                  ctx, ctx, vec, vec, tab_q, tab_q, tab_k, tab_k],
        out_specs=pl.BlockSpec((ATTN_Q_ROWS, qw), lambda b, h, c: (b * nq + c, h)),
        out_shape=jax.ShapeDtypeStruct((TOK_S, Q_WIDTH), BF16),
        compiler_params=_cparams(("arbitrary", "arbitrary", "arbitrary")),
        name="attn_latent",
    )(proj, proj, proj,
      cache_k.reshape(DEC_BATCH * PAST_LEN, KV_WIDTH), cache_v.reshape(DEC_BATCH * PAST_LEN, KV_WIDTH),
      q_norm.reshape(1, HEAD_DIM), k_norm.reshape(1, HEAD_DIM), cos, sin_signed, cos, sin_signed)


LRU_CT = 256
LRU_RC = 128


def _softplus(x):
    return jnp.maximum(x, 0.0) + jnp.log1p(jnp.exp(-jnp.abs(x)))


def _gelu_tanh(x):
    return 0.5 * x * (1.0 + jnp.tanh(math.sqrt(2.0 / math.pi) * (x + 0.044715 * (x * x * x))))


def _lru_kernel(xb_ref, gb_ref, cw_ref, cb_ref, lam_ref, wr_ref, br_ref, wi_ref, bi_ref, h0_ref,
                y_ref, hfin_ref, xpad_ref, a_ref, b_ref, hf_ref, hb_ref, *, seq):
    ct = LRU_CT
    zeros = jnp.zeros((SUBLANE, ct), F32)
    xpad_ref[0:SUBLANE, :] = zeros
    xpad_ref[seq + SUBLANE:seq + 2 * SUBLANE, :] = zeros
    xpad_ref[SUBLANE:seq + SUBLANE, :] = xb_ref[...]
    cw = cw_ref[...]
    sp = _softplus(-lam_ref[...])
    for r0 in range(0, seq, LRU_RC):
        xc = cb_ref[...] + cw[0:1, :] * xpad_ref[r0 + 7:r0 + 7 + LRU_RC, :]
        for w in range(1, 4):
            xc = xc + cw[w:w + 1, :] * xpad_ref[r0 + 7 + w:r0 + 7 + w + LRU_RC, :]
        for kb in range(ct // LRU_BLOCK):
            cols = slice(kb * LRU_BLOCK, (kb + 1) * LRU_BLOCK)
            xk = xc[:, cols]
            xkb = xk.astype(BF16)
            for d in range(2):
                r = jax.nn.sigmoid(jnp.dot(xkb, wr_ref[d, kb].astype(BF16), preferred_element_type=F32)
                                   + br_ref[d:d + 1, cols])
                i = jax.nn.sigmoid(jnp.dot(xkb, wi_ref[d, kb].astype(BF16), preferred_element_type=F32)
                                   + bi_ref[d:d + 1, cols])
                log_a = -LRU_C * r * sp[d:d + 1, cols]
                a = jnp.exp(log_a)
                a_ref[d, r0:r0 + LRU_RC, cols] = a
                b_ref[d, r0:r0 + LRU_RC, cols] = jnp.sqrt(-jnp.tanh(log_a) * (a * a + 1.0)) * (i * xk)

    nblk = seq // SUBLANE
    row = lax.broadcasted_iota(jnp.int32, (SUBLANE, ct), 0)

    def body(i, carry):
        hf, hb = carry
        rf = pl.multiple_of(i * SUBLANE, SUBLANE)
        a = a_ref[0, pl.ds(rf, SUBLANE), :]
        b = b_ref[0, pl.ds(rf, SUBLANE), :]
        for s in (1, 2, 4):
            m = row >= s
            a_s = jnp.where(m, pltpu.roll(a, s, 0), 1.0)
            b_s = jnp.where(m, pltpu.roll(b, s, 0), 0.0)
            b = a * b_s + b
            a = a * a_s
        hblk = a * hf + b
        hf_ref[pl.ds(rf, SUBLANE), :] = hblk
        hf = hblk[SUBLANE - 1:SUBLANE, :]

        rb = pl.multiple_of((nblk - 1 - i) * SUBLANE, SUBLANE)
        a = a_ref[1, pl.ds(rb, SUBLANE), :]
        b = b_ref[1, pl.ds(rb, SUBLANE), :]
        for s in (1, 2, 4):
            m = row < SUBLANE - s
            a_s = jnp.where(m, pltpu.roll(a, SUBLANE - s, 0), 1.0)
            b_s = jnp.where(m, pltpu.roll(b, SUBLANE - s, 0), 0.0)
            b = a * b_s + b
            a = a * a_s
        hblk = a * hb + b
        hb_ref[pl.ds(rb, SUBLANE), :] = hblk
        hb = hblk[0:1, :]
        return hf, hb

    hf, hb = lax.fori_loop(0, nblk, body, (h0_ref[0:1, :], h0_ref[1:2, :]))
    hfin_ref[0:1, :] = hf
    hfin_ref[1:2, :] = hb
    for r0 in range(0, seq, LRU_RC):
        rows = slice(r0, r0 + LRU_RC)
        y = (hf_ref[rows, :] + hb_ref[rows, :]) * _gelu_tanh(gb_ref[rows, :])
        y_ref[rows, :] = y.astype(y_ref.dtype)


def _lru(proj, h0, p, *, seq, batch, row0, name):
    ct = LRU_CT
    nkb = ct // LRU_BLOCK
    xb_blk0 = (Q_WIDTH + 2 * KV_WIDTH) // ct
    gb_blk0 = (Q_WIDTH + 2 * KV_WIDTH + LRU_WIDTH) // ct
    r0 = row0 // seq
    vec2 = pl.BlockSpec((2, ct), lambda b, c: (0, c))
    wblk = pl.BlockSpec((2, nkb, LRU_BLOCK, LRU_BLOCK), lambda b, c: (0, c, 0, 0))
    return pl.pallas_call(
        functools.partial(_lru_kernel, seq=seq),
        grid=(batch, LRU_WIDTH // ct),
        in_specs=[pl.BlockSpec((seq, ct), lambda b, c: (r0 + b, xb_blk0 + c)),
                  pl.BlockSpec((seq, ct), lambda b, c: (r0 + b, gb_blk0 + c)),
                  pl.BlockSpec((4, ct), lambda b, c: (0, c)),
                  pl.BlockSpec((1, ct), lambda b, c: (0, c)),
                  vec2, wblk, vec2, wblk, vec2,
                  pl.BlockSpec((None, 2, ct), lambda b, c: (b, 0, c))],
        out_specs=[pl.BlockSpec((seq, ct), lambda b, c: (b, c)),
                   pl.BlockSpec((None, 2, ct), lambda b, c: (b, 0, c))],
        out_shape=[jax.ShapeDtypeStruct((batch * seq, LRU_WIDTH), BF16),
                   jax.ShapeDtypeStruct((batch, 2, LRU_WIDTH), F32)],
        scratch_shapes=[pltpu.VMEM((seq + 2 * SUBLANE, ct), F32),
                        pltpu.VMEM((2, seq, ct), F32),
                        pltpu.VMEM((2, seq, ct), F32),
                        pltpu.VMEM((seq, ct), F32),
                        pltpu.VMEM((seq, ct), F32)],
        compiler_params=_cparams(("arbitrary", "arbitrary")),
        name=name,
    )(proj, proj, p['conv_w'], p['conv_b'].reshape(1, LRU_WIDTH), p['lam'],
      p['w_r'], p['b_r'], p['w_i'], p['b_i'], h0)


OUT_TN = 512
OUT_TM = 512


def _outproj_kernel(*refs, n_a, with_router):
    a_refs = refs[:n_a]
    w_refs = refs[n_a:2 * n_a]
    res_ref, gate_ref, g_ref, b_ref = refs[2 * n_a:2 * n_a + 4]
    pos = 2 * n_a + 4
    if with_router:
        sh_ref, sc_ref, rt_ref = refs[pos:pos + 3]
        pos += 3
        o_ref, u_ref, lg_ref, acc_ref = refs[pos:pos + 4]
    else:
        o_ref, acc_ref = refs[pos:pos + 2]
    j = pl.program_id(1)
    nj = pl.num_programs(1)
    acc = jnp.dot(a_refs[0][...], w_refs[0][...].astype(BF16), preferred_element_type=F32)
    for k in range(1, n_a):
        acc = acc + jnp.dot(a_refs[k][...], w_refs[k][...].astype(BF16), preferred_element_type=F32)
    acc_ref[j] = DN_ALPHA * res_ref[...] + gate_ref[0] * acc

    @pl.when(j == nj - 1)
    def _():
        tm = acc_ref.shape[1]
        n_chunks = acc_ref.shape[0]
        rc = 128

        def chunk(c, _):
            r0 = pl.multiple_of(c * rc, rc)
            ys = [acc_ref[k, pl.ds(r0, rc), :] for k in range(n_chunks)]
            tot = ys[0].sum(-1, keepdims=True)
            for y in ys[1:]:
                tot = tot + y.sum(-1, keepdims=True)
            mu = tot / D_MODEL
            sq = None
            for y in ys:
                t = ((y - mu) * (y - mu)).sum(-1, keepdims=True)
                sq = t if sq is None else sq + t
            rstd = lax.rsqrt(sq / D_MODEL + LN_EPS)
            lg = None
            for k, y in enumerate(ys):
                cols = slice(k * OUT_TN, (k + 1) * OUT_TN)
                xn = (y - mu) * rstd * g_ref[:, cols] + b_ref[:, cols]
                o_ref[pl.ds(r0, rc), cols] = xn
                if with_router:
                    u = xn * (1.0 + sc_ref[0][:, cols]) + sh_ref[0][:, cols]
                    u_ref[pl.ds(r0, rc), cols] = u.astype(BF16)
                    t = _dot3(u, rt_ref[cols, :])
                    lg = t if lg is None else lg + t
            if with_router:
                lg_ref[pl.ds(r0, rc), :] = lg
            return 0

        lax.fori_loop(0, tm // rc, chunk, 0)


def _outproj(a_list, w, res, mods, gate_chunk, g, b, *, name, router=None, router_chunks=None):
    tm = OUT_TM
    tn = OUT_TN
    nj = D_MODEL // tn
    n_a = len(a_list)
    in_specs = []
    k0 = 0
    w_specs = []
    for a in a_list:
        ka = a.shape[1]
        in_specs.append(pl.BlockSpec((tm, ka), lambda i, j: (i, 0)))
        w_specs.append(pl.BlockSpec((ka, tn), lambda i, j, blk=k0 // ka: (blk, j)))
        k0 += ka
    in_specs += w_specs
    in_specs += [pl.BlockSpec((tm, tn), lambda i, j: (i, j)),
                 _mod_spec(tm, gate_chunk, tn, col_of=lambda j: j),
                 pl.BlockSpec((1, D_MODEL), lambda i, j: (0, 0)),
                 pl.BlockSpec((1, D_MODEL), lambda i, j: (0, 0))]
    args = list(a_list) + [w] * n_a + [res, mods, g.reshape(1, D_MODEL), b.reshape(1, D_MODEL)]
    out_specs = [pl.BlockSpec((tm, D_MODEL), lambda i, j: (i, 0))]
    out_shape = [jax.ShapeDtypeStruct((N_TOK, D_MODEL), F32)]
    if router is not None:
        in_specs += [_mod_spec(tm, router_chunks[0]), _mod_spec(tm, router_chunks[1]),
                     pl.BlockSpec((D_MODEL, LANE), lambda i, j: (0, 0))]
        args += [mods, mods, router]
        out_specs += [pl.BlockSpec((tm, D_MODEL), lambda i, j: (i, 0)),
                      pl.BlockSpec((tm, LANE), lambda i, j: (i, 0))]
        out_shape += [jax.ShapeDtypeStruct((N_TOK, D_MODEL), BF16),
                      jax.ShapeDtypeStruct((N_TOK, LANE), F32)]
    return pl.pallas_call(
        functools.partial(_outproj_kernel, n_a=n_a, with_router=router is not None),
        grid=(N_TOK // tm, nj),
        in_specs=in_specs,
        out_specs=out_specs,
        out_shape=out_shape,
        scratch_shapes=[pltpu.VMEM((nj, tm, tn), F32)],
        compiler_params=_cparams(("arbitrary", "arbitrary")),
        name=name,
    )(*args)


FFN_TF = 256


def _ffn_kernel(x_ref, sh_ref, sc_ref, gate_ref, g_ref, b_ref, w1_ref, w3_ref, w2_ref, o_ref, u_ref):
    j = pl.program_id(1)
    nj = pl.num_programs(1)

    @pl.when(j == 0)
    def _():
        u_ref[...] = (x_ref[...] * (1.0 + sc_ref[0]) + sh_ref[0]).astype(BF16)

    u = u_ref[...]
    h1 = jnp.dot(u, w1_ref[...].astype(BF16), preferred_element_type=F32)
    h3 = jnp.dot(u, w3_ref[...].astype(BF16), preferred_element_type=F32)
    h = (_silu(h1) * h3).astype(BF16)
    part = jnp.dot(h, w2_ref[...].astype(BF16), preferred_element_type=F32)

    @pl.when(j == 0)
    def _():
        o_ref[...] = part

    @pl.when(j > 0)
    def _():
        o_ref[...] += part

    @pl.when(j == nj - 1)
    def _():
        rc = 128

        def chunk(c, _):
            r0 = pl.multiple_of(c * rc, rc)
            y = DN_ALPHA * x_ref[pl.ds(r0, rc), :] + gate_ref[0] * o_ref[pl.ds(r0, rc), :]
            o_ref[pl.ds(r0, rc), :] = _layer_norm_rows(y, g_ref[...], b_ref[...])
            return 0

        lax.fori_loop(0, o_ref.shape[0] // rc, chunk, 0)


def _ffn(x, mods, p):
    tm = ROW_TILE
    tf = FFN_TF
    vec = pl.BlockSpec((1, D_MODEL), lambda i, j: (0, 0))
    return pl.pallas_call(
        _ffn_kernel,
        grid=(N_TOK // tm, D_FF // tf),
        in_specs=[pl.BlockSpec((tm, D_MODEL), lambda i, j: (i, 0), pipeline_mode=pl.Buffered(1)),
                  _mod_spec(tm, 3), _mod_spec(tm, 4), _mod_spec(tm, 5), vec, vec,
                  pl.BlockSpec((D_MODEL, tf), lambda i, j: (0, j)),
                  pl.BlockSpec((D_MODEL, tf), lambda i, j: (0, j)),
                  pl.BlockSpec((tf, D_MODEL), lambda i, j: (j, 0))],
        out_specs=pl.BlockSpec((tm, D_MODEL), lambda i, j: (i, 0)),
        out_shape=jax.ShapeDtypeStruct((N_TOK, D_MODEL), F32),
        scratch_shapes=[pltpu.VMEM((tm, D_MODEL), BF16)],
        compiler_params=_cparams(("arbitrary", "arbitrary")),
        name="ffn_swiglu",
    )(x, mods, mods, mods, p['ln2_g'].reshape(1, D_MODEL), p['ln2_b'].reshape(1, D_MODEL),
      p['ffn_w1'], p['ffn_w3'], p['ffn_w2'])


def _filter_mlp_kernel(feat_ref, w1_ref, b1_ref, f1_ref, w2_ref, b2_ref, f2_ref, o_ref):
    h = jnp.sin(f1_ref[...] * (_dot3(feat_ref[...], w1_ref[...]) + b1_ref[...]))
    o_ref[...] = jnp.sin(f2_ref[...] * (_dot3(h, w2_ref[...]) + b2_ref[...]))


def _filter_features(seq):
    t = jnp.arange(seq, dtype=F32)
    t01 = t / (seq - 1)
    w = 2.0 * math.pi * t / seq
    f = jnp.linspace(1e-4, HYENA_BANDS - 1, HYENA_BANDS, dtype=F32)
    fw = w[:, None] * f[None, :]
    feat = jnp.concatenate([t01[:, None], jnp.cos(fw), -jnp.sin(fw)], -1)
    return jnp.pad(feat, ((0, 0), (0, LANE - HYENA_EMB))), t01[:, None]


def _filter_mlp(seq, p):
    feat, t01 = _filter_features(seq)
    hid = HYENA_FILTER_HIDDEN
    w1 = jnp.pad(p['filt_w1'], ((0, LANE - HYENA_EMB), (0, 0)))
    row = lambda v: v.reshape(1, hid)
    h2 = pl.pallas_call(
        _filter_mlp_kernel,
        out_shape=jax.ShapeDtypeStruct((seq, hid), F32),
        compiler_params=pltpu.CompilerParams(vmem_limit_bytes=VMEM_LIMIT),
        name=f"hyena_filter_mlp_{seq}",
    )(feat, w1, row(p['filt_b1']), row(p['filt_f1']), p['filt_w2'], row(p['filt_b2']), row(p['filt_f2']))
    return h2, t01


def _dft_matrices(seq):
    n = 2 * seq
    f = jnp.arange(seq, dtype=jnp.int32)
    k = (f[:, None] * f[None, :]) % n
    ang = k.astype(F32) * (math.pi / seq)
    c = jnp.cos(ang)
    s = -jnp.sin(ang)
    alt = jnp.where(f % 2 == 0, 1.0, -1.0).astype(F32)
    s_fwd = s.at[0, :].set(alt)
    s_inv = s.at[:, 0].set(alt)
    fwd = jnp.concatenate([c, s_fwd], axis=0).astype(BF16)
    inv = jnp.concatenate([c, s_inv], axis=1).astype(BF16)
    return fwd, inv, alt[:, None]


HY_RC = 128


def _hyena_kernel(x1_ref, x2_ref, z_ref, sw1_ref, sw2_ref, swz_ref, sb1_ref, sb2_ref, sbz_ref,
                  h2_ref, t01_ref, alt_ref, w3_ref, dec_ref, bias_ref, fwd_ref, inv_ref,
                  o_ref, kr_ref, ki_ref, zb_ref, zf_ref, yb_ref, pad_ref, g1_ref, g2_ref, zc_ref,
                  *, seq, nb, ct):
    nrc = seq // HY_RC

    @pl.when(pl.program_id(1) == 0)
    def _():
        t01 = t01_ref[...]
        h2 = h2_ref[...]
        row0 = lax.broadcasted_iota(jnp.int32, (seq, ct), 0) == 0
        wf = jnp.where(row0, 1.0, 2.0) / (2.0 * seq)
        for n in range(HYENA_ORDER):
            kpos = _dot3(h2, w3_ref[n]) * jnp.exp(-t01 * jnp.abs(dec_ref[n:n + 1, :]))
            kneg = _dot3(h2, w3_ref[HYENA_ORDER + n]) * jnp.exp(
                -t01 * jnp.abs(dec_ref[HYENA_ORDER + n:HYENA_ORDER + n + 1, :]))
            kneg = jnp.where(row0, 0.0, kneg)
            ksum = kpos + kneg
            kdif = kpos - kneg
            kr = jnp.dot(fwd_ref[0:seq, :], ksum.astype(BF16), preferred_element_type=F32)
            ki = jnp.dot(fwd_ref[seq:2 * seq, :], kdif.astype(BF16), preferred_element_type=F32)
            nyq = jnp.sum(alt_ref[...] * ksum, axis=0, keepdims=True) / (2.0 * seq)
            kr_ref[n] = kr * wf
            ki_ref[n] = jnp.where(row0, nyq, ki * wf)

    zeros = jnp.zeros((SUBLANE, ct), F32)
    pad_ref[0:SUBLANE, :] = zeros
    pad_ref[seq + SUBLANE:seq + 2 * SUBLANE, :] = zeros

    def short_conv(src_ref, rows0, w_ref, b_ref, dst_ref):
        pad_ref[SUBLANE:seq + SUBLANE, :] = src_ref[rows0:rows0 + seq, :]
        w = w_ref[...]
        for c in range(nrc):
            r0 = c * HY_RC
            acc = b_ref[...] + w[0:1, :] * pad_ref[r0 + 7:r0 + 7 + HY_RC, :]
            acc = acc + w[1:2, :] * pad_ref[r0 + 8:r0 + 8 + HY_RC, :]
            acc = acc + w[2:3, :] * pad_ref[r0 + 9:r0 + 9 + HY_RC, :]
            dst_ref[r0:r0 + HY_RC, :] = acc

    for bi in range(nb):
        rows0 = bi * seq
        short_conv(x1_ref, rows0, sw1_ref, sb1_ref, g1_ref)
        short_conv(x2_ref, rows0, sw2_ref, sb2_ref, g2_ref)
        short_conv(z_ref, rows0, swz_ref, sbz_ref, zc_ref)
        for n, gate_ref in enumerate((g1_ref, g2_ref)):
            for c in range(nrc):
                rows = slice(c * HY_RC, (c + 1) * HY_RC)
                zb_ref[rows, :] = zc_ref[rows, :].astype(BF16)
            zf_ref[...] = jnp.dot(fwd_ref[...], zb_ref[...], preferred_element_type=F32)
            for c in range(nrc):
                rows = slice(c * HY_RC, (c + 1) * HY_RC)
                rows_i = slice(seq + c * HY_RC, seq + (c + 1) * HY_RC)
                zr = zf_ref[rows, :]
                zi = zf_ref[rows_i, :]
                kr = kr_ref[n, rows, :]
                ki = ki_ref[n, rows, :]
                yr = zr * kr - zi * ki
                yi = zr * ki + zi * kr
                if c == 0:
                    first = lax.broadcasted_iota(jnp.int32, (HY_RC, ct), 0) == 0
                    yr = jnp.where(first, zr * kr, yr)
                    yi = jnp.where(first, zi * ki, yi)
                yb_ref[rows, :] = yr.astype(BF16)
                yb_ref[rows_i, :] = yi.astype(BF16)
            zf_ref[0:seq, :] = jnp.dot(inv_ref[...], yb_ref[...], preferred_element_type=F32)
            for c in range(nrc):
                rows = slice(c * HY_RC, (c + 1) * HY_RC)
                zc = zc_ref[rows, :]
                znew = gate_ref[rows, :] * (zf_ref[rows, :] + zc * bias_ref[n:n + 1, :])
                if n == HYENA_ORDER - 1:
                    o_ref[rows0 + c * HY_RC:rows0 + (c + 1) * HY_RC, :] = znew.astype(o_ref.dtype)
                else:
                    zc_ref[rows, :] = znew


def _hyena(proj, p, *, seq, batch, row0, nb, ct, name):
    nct = D_MODEL // ct
    h2, t01 = _filter_mlp(seq, p)
    fwd, inv, alt = _dft_matrices(seq)
    rb0 = row0 // (nb * seq)
    w3 = p['filt_w3'].reshape(HYENA_FILTER_HIDDEN, 2 * HYENA_ORDER, D_MODEL).transpose(1, 0, 2)
    dec = p['filt_decay'].reshape(2 * HYENA_ORDER, D_MODEL)
    sw = p['short_w']
    sb = p['short_b'].reshape(1, 3 * D_MODEL)
    slab = lambda k: pl.BlockSpec((nb * seq, ct), lambda c, b: (rb0 + b, k * nct + c))
    swk = lambda k: pl.BlockSpec((3, ct), lambda c, b: (0, k * nct + c))
    sbk = lambda k: pl.BlockSpec((1, ct), lambda c, b: (0, k * nct + c))
    const = lambda shape: pl.BlockSpec(shape, lambda c, b: tuple(0 for _ in shape))
    return pl.pallas_call(
        functools.partial(_hyena_kernel, seq=seq, nb=nb, ct=ct),
        grid=(nct, batch // nb),
        in_specs=[slab(0), slab(1), slab(2), swk(0), swk(1), swk(2), sbk(0), sbk(1), sbk(2),
                  const((seq, HYENA_FILTER_HIDDEN)), const((seq, 1)), const((seq, 1)),
                  pl.BlockSpec((2 * HYENA_ORDER, HYENA_FILTER_HIDDEN, ct), lambda c, b: (0, 0, c)),
                  pl.BlockSpec((2 * HYENA_ORDER, ct), lambda c, b: (0, c)),
                  pl.BlockSpec((HYENA_ORDER, ct), lambda c, b: (0, c)),
                  pl.BlockSpec((2 * seq, seq), lambda c, b: (0, 0), pipeline_mode=pl.Buffered(1)),
                  pl.BlockSpec((seq, 2 * seq), lambda c, b: (0, 0), pipeline_mode=pl.Buffered(1))],
        out_specs=pl.BlockSpec((nb * seq, ct), lambda c, b: (b, c)),
        out_shape=jax.ShapeDtypeStruct((batch * seq, D_MODEL), BF16),
        scratch_shapes=[pltpu.VMEM((HYENA_ORDER, seq, ct), F32),
                        pltpu.VMEM((HYENA_ORDER, seq, ct), F32),
                        pltpu.VMEM((seq, ct), BF16),
                        pltpu.VMEM((2 * seq, ct), F32),
                        pltpu.VMEM((2 * seq, ct), BF16),
                        pltpu.VMEM((seq + 2 * SUBLANE, ct), F32),
                        pltpu.VMEM((seq, ct), F32),
                        pltpu.VMEM((seq, ct), F32),
                        pltpu.VMEM((seq, ct), F32)],
        compiler_params=_cparams(("arbitrary", "arbitrary")),
        name=name,
    )(proj, proj, proj, sw, sw, sw, sb, sb, sb, h2, t01, alt, w3, dec, p['filt_bias'], fwd, inv)


def _moe_kernel(ie_ref, ib_ref, ir_ref, x_ref, w1_ref, w3_ref, w2_ref, o_hbm,
                acc_ref, w1b_ref, w3b_ref, w2b_ref, sem):
    w = pl.program_id(0)
    j = pl.program_id(1)
    nj = pl.num_programs(1)
    rows = ir_ref[w]
    ntiles = (rows + (MOE_ROW_TILE - 1)) // MOE_ROW_TILE
    rt = MOE_ROW_TILE

    def partial_out(r):
        off = pl.multiple_of(r * rt, rt)
        xt = x_ref[pl.ds(off, rt), :]
        h1 = jnp.dot(xt, w1b_ref[...], preferred_element_type=F32)
        h3 = jnp.dot(xt, w3b_ref[...], preferred_element_type=F32)
        h = (_silu(h1) * h3).astype(BF16)
        return off, jnp.dot(h, w2b_ref[...], preferred_element_type=F32)

    @pl.when(rows > 0)
    def _():
        w1b_ref[...] = w1_ref[...].astype(BF16)
        w3b_ref[...] = w3_ref[...].astype(BF16)
        w2b_ref[...] = w2_ref[...].astype(BF16)

        @pl.when(j == 0)
        def _():
            def body(r, _):
                off, part = partial_out(r)
                acc_ref[pl.ds(off, rt), :] = part
                return 0
            lax.fori_loop(0, ntiles, body, 0)

        @pl.when(j > 0)
        def _():
            def body(r, _):
                off, part = partial_out(r)
                acc_ref[pl.ds(off, rt), :] += part
                return 0
            lax.fori_loop(0, ntiles, body, 0)

        @pl.when(j == nj - 1)
        def _():
            base = ib_ref[w] * MOE_CHUNK

            def tile_copy(off):
                return pltpu.make_async_copy(acc_ref.at[pl.ds(off, rt)],
                                             o_hbm.at[pl.ds(base + off, rt)], sem)

            def start(r, _):
                tile_copy(pl.multiple_of(r * rt, rt)).start()
                return 0

            def wait(r, _):
                tile_copy(pl.multiple_of(r * rt, rt)).wait()
                return 0

            lax.fori_loop(0, ntiles, start, 0)
            lax.fori_loop(0, ntiles, wait, 0)


def _moe_experts(xs, item_expert, item_block, item_rows, p):
    tj = MOE_FF_TILE
    nj = D_FF_EXPERT // tj

    def jeff(w, j, ir):
        return jnp.where(ir[w] > 0, j, nj - 1)

    grid_spec = pltpu.PrefetchScalarGridSpec(
        num_scalar_prefetch=3,
        grid=(MOE_ITEMS, nj),
        in_specs=[pl.BlockSpec((MOE_CHUNK, D_MODEL), lambda w, j, ie, ib, ir: (ib[w], 0),
                               pipeline_mode=pl.Buffered(1)),
                  pl.BlockSpec((None, D_MODEL, tj), lambda w, j, ie, ib, ir: (ie[w], 0, jeff(w, j, ir))),
                  pl.BlockSpec((None, D_MODEL, tj), lambda w, j, ie, ib, ir: (ie[w], 0, jeff(w, j, ir))),
                  pl.BlockSpec((None, tj, D_MODEL), lambda w, j, ie, ib, ir: (ie[w], jeff(w, j, ir), 0))],
        out_specs=pl.BlockSpec(memory_space=pl.ANY),
        scratch_shapes=[pltpu.VMEM((MOE_CHUNK, D_MODEL), F32),
                        pltpu.VMEM((D_MODEL, tj), BF16),
                        pltpu.VMEM((D_MODEL, tj), BF16),
                        pltpu.VMEM((tj, D_MODEL), BF16),
                        pltpu.SemaphoreType.DMA(())],
    )
    return pl.pallas_call(
        _moe_kernel,
        grid_spec=grid_spec,
        out_shape=jax.ShapeDtypeStruct((MOE_ROWS, D_MODEL), F32),
        compiler_params=_cparams(("arbitrary", "arbitrary")),
        name="moe_experts",
    )(item_expert, item_block, item_rows, xs, p['exp_w1'], p['exp_w3'], p['exp_w2'])


def _route(logits):
    top_v, top_i = lax.top_k(logits[:, :N_EXPERTS], TOP_K)
    probs = jax.nn.softmax(top_v, axis=-1)
    e_flat = top_i.reshape(-1).astype(jnp.int32)
    onehot = (e_flat[:, None] == jnp.arange(N_EXPERTS, dtype=jnp.int32)[None, :]).astype(jnp.int32)
    csum = jnp.cumsum(onehot, axis=0)
    rank = jnp.take_along_axis(csum, e_flat[:, None], axis=1)[:, 0] - 1
    counts = csum[-1]
    blocks = (counts + MOE_CHUNK - 1) // MOE_CHUNK
    bend = jnp.cumsum(blocks)
    bstart = bend - blocks
    total = bend[-1]
    pos = bstart[e_flat] * MOE_CHUNK + rank
    w = jnp.arange(MOE_ITEMS, dtype=jnp.int32)
    w_eff = jnp.minimum(w, total - 1)
    item_expert = jnp.minimum(jnp.sum((w_eff[:, None] >= bend[None, :]).astype(jnp.int32), axis=1),
                              N_EXPERTS - 1)
    item_rows = jnp.where(w < total,
                          jnp.clip(counts[item_expert] - (w - bstart[item_expert]) * MOE_CHUNK, 0, MOE_CHUNK),
                          0)
    src = jnp.zeros((MOE_ROWS,), jnp.int32).at[pos].set(
        jnp.arange(N_TOK * TOP_K, dtype=jnp.int32) // TOP_K)
    return probs, pos.reshape(N_TOK, TOP_K), src, item_expert.astype(jnp.int32), \
        w_eff.astype(jnp.int32), item_rows.astype(jnp.int32)


def _postnorm_kernel(x_ref, d_ref, gate_ref, g_ref, b_ref, o_ref):
    y = DN_ALPHA * x_ref[...] + gate_ref[0] * d_ref[...]
    o_ref[...] = _layer_norm_rows(y, g_ref[...], b_ref[...])


def _postnorm(x, delta, mods, gate_chunk, g, b):
    tm = 256
    vec = pl.BlockSpec((1, D_MODEL), lambda i: (0, 0))
    return pl.pallas_call(
        _postnorm_kernel,
        grid=(N_TOK // tm,),
        in_specs=[pl.BlockSpec((tm, D_MODEL), lambda i: (i, 0)),
                  pl.BlockSpec((tm, D_MODEL), lambda i: (i, 0)),
                  _mod_spec(tm, gate_chunk), vec, vec],
        out_specs=pl.BlockSpec((tm, D_MODEL), lambda i: (i, 0)),
        out_shape=jax.ShapeDtypeStruct((N_TOK, D_MODEL), F32),
        compiler_params=_cparams(("arbitrary",)),
        name="final_postnorm",
    )(x, delta, mods, g.reshape(1, D_MODEL), b.reshape(1, D_MODEL))


def kernel(x_prompt, x_sample, c, c_ctx, cache_l0_k, cache_l0_v, state_l0_lru, l0_ada_w, l0_ada_b, l0_w_in, l0_q_norm, l0_k_norm, l0_lru_conv_w, l0_lru_conv_b, l0_lru_lambda, l0_lru_w_r, l0_lru_b_r, l0_lru_w_i, l0_lru_b_i, l0_w_out, l0_ln1_g, l0_ln1_b, l0_ffn_w1, l0_ffn_w3, l0_ffn_w2, l0_ln2_g, l0_ln2_b, l1_ada_w, l1_ada_b, l1_w_in, l1_short_w, l1_short_b, l1_filt_w1, l1_filt_b1, l1_filt_f1, l1_filt_w2, l1_filt_b2, l1_filt_f2, l1_filt_w3, l1_filt_decay, l1_filt_bias, l1_w_out, l1_ln1_g, l1_ln1_b, l1_router, l1_exp_w1, l1_exp_w3, l1_exp_w2, l1_ln2_g, l1_ln2_b):
    x = jnp.concatenate([x_prompt.reshape(TOK_P, D_MODEL), x_sample.reshape(TOK_S, D_MODEL)], axis=0)
    cond = jnp.concatenate([c_ctx[None, :], c, jnp.zeros((N_COND - 1 - DEC_BATCH, D_MODEL), F32)], axis=0)
    mods0 = _ada(cond, l0_ada_w, l0_ada_b)
    mods1 = _ada(cond, l1_ada_w, l1_ada_b)

    lru_p = dict(conv_w=l0_lru_conv_w, conv_b=l0_lru_conv_b, lam=l0_lru_lambda,
                 w_r=l0_lru_w_r, b_r=l0_lru_b_r, w_i=l0_lru_w_i, b_i=l0_lru_b_i)
    proj0 = _proj(x, mods0, l0_w_in, tn=512, name="l0_in_proj")
    attn_p, new_k, new_v = _attn_context(proj0, l0_q_norm, l0_k_norm)
    attn_s = _attn_latent(proj0, cache_l0_k, cache_l0_v, l0_q_norm, l0_k_norm)
    lru_p_out, new_h = _lru(proj0, jnp.zeros((BATCH, 2, LRU_WIDTH), F32), lru_p,
                            seq=SEQ, batch=BATCH, row0=0, name="lru_context")
    lru_s_out, _ = _lru(proj0, state_l0_lru, lru_p,
                        seq=DEC_SEQ, batch=DEC_BATCH, row0=TOK_P, name="lru_latent")
    attn = jnp.concatenate([attn_p, attn_s], axis=0)
    lru = jnp.concatenate([lru_p_out, lru_s_out], axis=0)
    x = _outproj([attn, lru], l0_w_out, x, mods0, 2, l0_ln1_g, l0_ln1_b, name="l0_out_proj")[0]
    x = _ffn(x, mods0, dict(ffn_w1=l0_ffn_w1, ffn_w3=l0_ffn_w3, ffn_w2=l0_ffn_w2,
                            ln2_g=l0_ln2_g, ln2_b=l0_ln2_b))

    hy_p = dict(short_w=l1_short_w, short_b=l1_short_b, filt_w1=l1_filt_w1, filt_b1=l1_filt_b1,
                filt_f1=l1_filt_f1, filt_w2=l1_filt_w2, filt_b2=l1_filt_b2, filt_f2=l1_filt_f2,
                filt_w3=l1_filt_w3, filt_decay=l1_filt_decay, filt_bias=l1_filt_bias)
    proj1 = _proj(x, mods1, l1_w_in, tn=512, name="l1_in_proj")
    z_p = _hyena(proj1, hy_p, seq=SEQ, batch=BATCH, row0=0, nb=4, ct=512, name="hyena_context")
    z_s = _hyena(proj1, hy_p, seq=DEC_SEQ, batch=DEC_BATCH, row0=TOK_P, nb=1, ct=256, name="hyena_latent")
    z = jnp.concatenate([z_p, z_s], axis=0)
    router = jnp.pad(l1_router, ((0, 0), (0, LANE - N_EXPERTS)))
    x, u, logits = _outproj([z], l1_w_out, x, mods1, 2, l1_ln1_g, l1_ln1_b, name="l1_out_proj",
                            router=router, router_chunks=(3, 4))
    probs, pos, src, item_expert, item_block, item_rows = _route(logits)
    xs = jnp.take(u, src, axis=0)
    ys = _moe_experts(xs, item_expert, item_block, item_rows,
                      dict(exp_w1=l1_exp_w1, exp_w3=l1_exp_w3, exp_w2=l1_exp_w2))
    delta = (probs[:, 0:1] * jnp.take(ys, pos[:, 0], axis=0)
             + probs[:, 1:2] * jnp.take(ys, pos[:, 1], axis=0))
    x = _postnorm(x, delta, mods1, 5, l1_ln2_g, l1_ln2_b)

    y_prompt = x[:TOK_P].reshape(BATCH, SEQ, D_MODEL)
    y_sample = x[TOK_P:].reshape(DEC_BATCH, DEC_SEQ, D_MODEL)
    return (y_prompt, y_sample,
            new_k.reshape(BATCH, SEQ, N_KV_HEADS, HEAD_DIM),
            new_v.reshape(BATCH, SEQ, N_KV_HEADS, HEAD_DIM),
            new_h)
```

```python
import functools
import math

import jax
import jax.numpy as jnp
from jax import lax
from jax.experimental import pallas as pl
from jax.experimental.pallas import tpu as pltpu

F32 = jnp.float32
BF16 = jnp.bfloat16

D_MODEL = 2048
BATCH = 16
SEQ = 256
DEC_BATCH = 2
DEC_SEQ = 1024
PAST_LEN = 256
GRID_W = 64
HEAD_DIM = 128
N_Q_HEADS = 8
N_KV_HEADS = 2
Q_GROUP = N_Q_HEADS // N_KV_HEADS
Q_WIDTH = N_Q_HEADS * HEAD_DIM
KV_WIDTH = N_KV_HEADS * HEAD_DIM
ROPE_THETA = 10000.0
LRU_WIDTH = D_MODEL // 2
LRU_BLOCK = 128
LRU_C = 8.0
MIX0_IN = Q_WIDTH + 2 * KV_WIDTH + 2 * LRU_WIDTH
HYENA_ORDER = 2
HYENA_EMB = 33
HYENA_BANDS = (HYENA_EMB - 1) // 2
HYENA_FILTER_HIDDEN = 64
D_FF = 5632
N_EXPERTS = 8
TOP_K = 2
D_FF_EXPERT = 7168
N_MOD = 6
LN_EPS = 1e-5
QK_EPS = 1e-6
DEPTH = 2
DN_ALPHA = (2 * DEPTH) ** 0.25

TOK_P = BATCH * SEQ
TOK_S = DEC_BATCH * DEC_SEQ
N_TOK = TOK_P + TOK_S
N_COND = 8
LANE = 128
SUBLANE = 8
VMEM_LIMIT = 56 * 1024 * 1024

ROW_TILE = 1024
MOE_CHUNK = 2048
MOE_ROW_TILE = 256
MOE_FF_TILE = 256
MOE_ITEMS = N_EXPERTS + (N_TOK * TOP_K) // MOE_CHUNK
MOE_ROWS = MOE_ITEMS * MOE_CHUNK


def _cparams(sem):
    return pltpu.CompilerParams(dimension_semantics=sem, vmem_limit_bytes=VMEM_LIMIT)


def _cond_of_tile(i, tm):
    return jnp.maximum(i * tm // DEC_SEQ - (TOK_P // DEC_SEQ - 1), 0)


def _mod_spec(tm, chunk, width=D_MODEL, col_of=None):
    per = D_MODEL // width
    if col_of is None:
        return pl.BlockSpec((1, 1, width), lambda i, *_: (_cond_of_tile(i, tm), 0, chunk * per))
    return pl.BlockSpec((1, 1, width),
                        lambda i, j, *_: (_cond_of_tile(i, tm), 0, chunk * per + col_of(j)))


def _silu(x):
    return x * jax.nn.sigmoid(x)


def _split_bf16(x):
    hi = x.astype(BF16)
    lo = (x - hi.astype(F32)).astype(BF16)
    return hi, lo


def _dot3(a, b):
    ah, al = _split_bf16(a)
    bh, bl = _split_bf16(b)
    d = lambda x, y: jnp.dot(x, y, preferred_element_type=F32)
    return d(ah, bh) + (d(ah, bl) + d(al, bh))


def _layer_norm_rows(y, g, b):
    mu = jnp.mean(y, -1, keepdims=True)
    yc = y - mu
    var = jnp.mean(yc * yc, -1, keepdims=True)
    return yc * lax.rsqrt(var + LN_EPS) * g + b


def _ada_kernel(c_ref, w_ref, b_ref, o_ref):
    s = _silu(c_ref[...]).astype(BF16)
    o_ref[...] = jnp.dot(s, w_ref[...].astype(BF16), preferred_element_type=F32) + b_ref[...]


def _ada(cond, w, b):
    tn = 1024
    n = w.shape[1]
    out = pl.pallas_call(
        _ada_kernel,
        grid=(n // tn,),
        in_specs=[pl.BlockSpec((N_COND, D_MODEL), lambda j: (0, 0)),
                  pl.BlockSpec((D_MODEL, tn), lambda j: (0, j)),
                  pl.BlockSpec((1, tn), lambda j: (0, j))],
        out_specs=pl.BlockSpec((N_COND, tn), lambda j: (0, j)),
        out_shape=jax.ShapeDtypeStruct((N_COND, n), F32),
        compiler_params=_cparams(("arbitrary",)),
        name="ada_modulation",
    )(cond, w, b.reshape(1, n))
    return out.reshape(N_COND, 1, n)


def _proj_kernel(x_ref, sh_ref, sc_ref, w_ref, o_ref, u_ref, wb_ref):
    i = pl.program_id(0)
    j = pl.program_id(1)

    @pl.when(j == 0)
    def _():
        u_ref[...] = (x_ref[...] * (1.0 + sc_ref[0]) + sh_ref[0]).astype(BF16)

    @pl.when(i == 0)
    def _():
        wb_ref[j] = w_ref[...].astype(BF16)

    o_ref[...] = jnp.dot(u_ref[...], wb_ref[j], preferred_element_type=F32)


def _first_pass_block(i, j, nj):
    return jnp.where(i == 0, j, nj - 1)


def _proj(x, mods, w, *, tn, name):
    tm = ROW_TILE
    n = w.shape[1]
    nj = n // tn
    return pl.pallas_call(
        _proj_kernel,
        grid=(N_TOK // tm, nj),
        in_specs=[pl.BlockSpec((tm, D_MODEL), lambda i, j: (i, 0), pipeline_mode=pl.Buffered(1)),
                  _mod_spec(tm, 0), _mod_spec(tm, 1),
                  pl.BlockSpec((D_MODEL, tn), lambda i, j: (0, _first_pass_block(i, j, nj)))],
        out_specs=pl.BlockSpec((tm, tn), lambda i, j: (i, j)),
        out_shape=jax.ShapeDtypeStruct((N_TOK, n), F32),
        scratch_shapes=[pltpu.VMEM((tm, D_MODEL), BF16),
                        pltpu.VMEM((nj, D_MODEL, tn), BF16)],
        compiler_params=_cparams(("arbitrary", "arbitrary")),
        name=name,
    )(x, mods, mods, w)


def _rms(x, g):
    return x * lax.rsqrt(jnp.mean(x * x, -1, keepdims=True) + QK_EPS) * g


def _dot_nt(a, b):
    return lax.dot_general(a, b, (((1,), (1,)), ((), ())), preferred_element_type=F32)


def _rope(x, cos, sin_signed):
    lane = lax.broadcasted_iota(jnp.int32, x.shape, 1)
    partner = jnp.where(lane % 2 == 0, pltpu.roll(x, HEAD_DIM - 1, 1), pltpu.roll(x, 1, 1))
    return x * cos + partner * sin_signed


def _attn_ctx_kernel(q_ref, k_ref, v_ref, qn_ref, kn_ref, o_ref, ko_ref, vo_ref):
    scale = HEAD_DIM ** -0.5
    kn = _rms(k_ref[...], kn_ref[...])
    v = v_ref[...]
    ko_ref[...] = kn
    vo_ref[...] = v
    kb = kn.astype(BF16)
    vb = v.astype(BF16)
    for g in range(Q_GROUP):
        cols = slice(g * HEAD_DIM, (g + 1) * HEAD_DIM)
        q = _rms(q_ref[:, cols], qn_ref[...]).astype(BF16)
        s = _dot_nt(q, kb) * scale
        p = jnp.exp(s - jnp.max(s, -1, keepdims=True))
        p = p / jnp.sum(p, -1, keepdims=True)
        o = jnp.dot(p.astype(BF16), vb, preferred_element_type=F32)
        o_ref[:, cols] = o.astype(o_ref.dtype)


def _attn_context(proj, q_norm, k_norm):
    qw = Q_GROUP * HEAD_DIM
    k_blk0 = Q_WIDTH // HEAD_DIM
    v_blk0 = (Q_WIDTH + KV_WIDTH) // HEAD_DIM
    vec = pl.BlockSpec((1, HEAD_DIM), lambda b, h: (0, 0))
    return pl.pallas_call(
        _attn_ctx_kernel,
        grid=(BATCH, N_KV_HEADS),
        in_specs=[pl.BlockSpec((SEQ, qw), lambda b, h: (b, h)),
                  pl.BlockSpec((SEQ, HEAD_DIM), lambda b, h: (b, k_blk0 + h)),
                  pl.BlockSpec((SEQ, HEAD_DIM), lambda b, h: (b, v_blk0 + h)),
                  vec, vec],
        out_specs=[pl.BlockSpec((SEQ, qw), lambda b, h: (b, h)),
                   pl.BlockSpec((SEQ, HEAD_DIM), lambda b, h: (b, h)),
                   pl.BlockSpec((SEQ, HEAD_DIM), lambda b, h: (b, h))],
        out_shape=[jax.ShapeDtypeStruct((TOK_P, Q_WIDTH), BF16),
                   jax.ShapeDtypeStruct((TOK_P, KV_WIDTH), F32),
                   jax.ShapeDtypeStruct((TOK_P, KV_WIDTH), F32)],
        compiler_params=_cparams(("arbitrary", "arbitrary")),
        name="attn_context",
    )(proj, proj, proj, q_norm.reshape(1, HEAD_DIM), k_norm.reshape(1, HEAD_DIM))


ATTN_Q_ROWS = 256


def _attn_lat_kernel(q_ref, k_ref, v_ref, ck_ref, cv_ref, qn_ref, kn_ref,
                     cq_ref, sq_ref, ck_tab_ref, sk_tab_ref, o_ref):
    scale = HEAD_DIM ** -0.5
    kb = _rope(_rms(k_ref[...], kn_ref[...]), ck_tab_ref[...], sk_tab_ref[...]).astype(BF16)
    vb = v_ref[...].astype(BF16)
    ckb = ck_ref[...].astype(BF16)
    cvb = cv_ref[...].astype(BF16)
    for g in range(Q_GROUP):
        cols = slice(g * HEAD_DIM, (g + 1) * HEAD_DIM)
        q = _rope(_rms(q_ref[:, cols], qn_ref[...]), cq_ref[...], sq_ref[...]).astype(BF16)
        s1 = _dot_nt(q, ckb) * scale
        s2 = _dot_nt(q, kb) * scale
        m = jnp.maximum(jnp.max(s1, -1, keepdims=True), jnp.max(s2, -1, keepdims=True))
        p1 = jnp.exp(s1 - m)
        p2 = jnp.exp(s2 - m)
        den = jnp.sum(p1, -1, keepdims=True) + jnp.sum(p2, -1, keepdims=True)
        o = (jnp.dot((p1 / den).astype(BF16), cvb, preferred_element_type=F32)
             + jnp.dot((p2 / den).astype(BF16), vb, preferred_element_type=F32))
        o_ref[:, cols] = o.astype(o_ref.dtype)


def _rope_tables():
    t = jnp.arange(DEC_SEQ)
    row = (t // GRID_W).astype(F32)
    col = (t % GRID_W).astype(F32)
    n_freq = HEAD_DIM // 4
    inv = 1.0 / (ROPE_THETA ** (jnp.arange(n_freq, dtype=F32) / n_freq))
    ang = jnp.concatenate([row[:, None] * inv, col[:, None] * inv], -1)
    cos = jnp.repeat(jnp.cos(ang), 2, axis=-1)
    sign = jnp.where(jnp.arange(HEAD_DIM) % 2 == 0, -1.0, 1.0).astype(F32)
    sin_signed = jnp.repeat(jnp.sin(ang), 2, axis=-1) * sign
    return cos, sin_signed


def _attn_latent(proj, cache_k, cache_v, q_norm, k_norm):
    qw = Q_GROUP * HEAD_DIM
    nq = DEC_SEQ // ATTN_Q_ROWS
    q_row0 = TOK_P // ATTN_Q_ROWS
    kv_row0 = TOK_P // DEC_SEQ
    k_blk0 = Q_WIDTH // HEAD_DIM
    v_blk0 = (Q_WIDTH + KV_WIDTH) // HEAD_DIM
    cos, sin_signed = _rope_tables()
    vec = pl.BlockSpec((1, HEAD_DIM), lambda b, h, c: (0, 0))
    tab_q = pl.BlockSpec((ATTN_Q_ROWS, HEAD_DIM), lambda b, h, c: (c, 0))
    tab_k = pl.BlockSpec((DEC_SEQ, HEAD_DIM), lambda b, h, c: (0, 0))
    ctx = pl.BlockSpec((PAST_LEN, HEAD_DIM), lambda b, h, c: (b, h))
    return pl.pallas_call(
        _attn_lat_kernel,
        grid=(DEC_BATCH, N_KV_HEADS, nq),
        in_specs=[pl.BlockSpec((ATTN_Q_ROWS, qw), lambda b, h, c: (q_row0 + b * nq + c, h)),
                  pl.BlockSpec((DEC_SEQ, HEAD_DIM), lambda b, h, c: (kv_row0 + b, k_blk0 + h)),
                  pl.BlockSpec((DEC_SEQ, HEAD_DIM), lambda b, h, c: (kv_row0 + b, v_blk0 + h)),
                  ctx, ctx, vec, vec, tab_q, tab_q, tab_k, tab_k],
        out_specs=pl.BlockSpec((ATTN_Q_ROWS, qw), lambda b, h, c: (b * nq + c, h)),
        out_shape=jax.ShapeDtypeStruct((TOK_S, Q_WIDTH), BF16),
        compiler_params=_cparams(("arbitrary", "arbitrary", "arbitrary")),
        name="attn_latent",
    )(proj, proj, proj,
      cache_k.reshape(DEC_BATCH * PAST_LEN, KV_WIDTH), cache_v.reshape(DEC_BATCH * PAST_LEN, KV_WIDTH),
      q_norm.reshape(1, HEAD_DIM), k_norm.reshape(1, HEAD_DIM), cos, sin_signed, cos, sin_signed)


LRU_CT = 256
LRU_RC = 128


def _softplus(x):
    return jnp.maximum(x, 0.0) + jnp.log1p(jnp.exp(-jnp.abs(x)))


def _gelu_tanh(x):
    return 0.5 * x * (1.0 + jnp.tanh(math.sqrt(2.0 / math.pi) * (x + 0.044715 * (x * x * x))))


def _lru_kernel(xb_ref, gb_ref, cw_ref, cb_ref, lam_ref, wr_ref, br_ref, wi_ref, bi_ref, h0_ref,
                y_ref, hfin_ref, xpad_ref, a_ref, b_ref, hf_ref, hb_ref, *, seq):
    ct = LRU_CT
    zeros = jnp.zeros((SUBLANE, ct), F32)
    xpad_ref[0:SUBLANE, :] = zeros
    xpad_ref[seq + SUBLANE:seq + 2 * SUBLANE, :] = zeros
    xpad_ref[SUBLANE:seq + SUBLANE, :] = xb_ref[...]
    cw = cw_ref[...]
    sp = _softplus(-lam_ref[...])
    for r0 in range(0, seq, LRU_RC):
        xc = cb_ref[...] + cw[0:1, :] * xpad_ref[r0 + 7:r0 + 7 + LRU_RC, :]
        for w in range(1, 4):
            xc = xc + cw[w:w + 1, :] * xpad_ref[r0 + 7 + w:r0 + 7 + w + LRU_RC, :]
        for kb in range(ct // LRU_BLOCK):
            cols = slice(kb * LRU_BLOCK, (kb + 1) * LRU_BLOCK)
            xk = xc[:, cols]
            xkb = xk.astype(BF16)
            for d in range(2):
                r = jax.nn.sigmoid(jnp.dot(xkb, wr_ref[d, kb].astype(BF16), preferred_element_type=F32)
                                   + br_ref[d:d + 1, cols])
                i = jax.nn.sigmoid(jnp.dot(xkb, wi_ref[d, kb].astype(BF16), preferred_element_type=F32)
                                   + bi_ref[d:d + 1, cols])
                log_a = -LRU_C * r * sp[d:d + 1, cols]
                a = jnp.exp(log_a)
                a_ref[d, r0:r0 + LRU_RC, cols] = a
                b_ref[d, r0:r0 + LRU_RC, cols] = jnp.sqrt(-jnp.tanh(log_a) * (a * a + 1.0)) * (i * xk)

    nblk = seq // SUBLANE
    row = lax.broadcasted_iota(jnp.int32, (SUBLANE, ct), 0)

    def body(i, carry):
        hf, hb = carry
        rf = pl.multiple_of(i * SUBLANE, SUBLANE)
        a = a_ref[0, pl.ds(rf, SUBLANE), :]
        b = b_ref[0, pl.ds(rf, SUBLANE), :]
        for s in (1, 2, 4):
            m = row >= s
            a_s = jnp.where(m, pltpu.roll(a, s, 0), 1.0)
            b_s = jnp.where(m, pltpu.roll(b, s, 0), 0.0)
            b = a * b_s + b
            a = a * a_s
        hblk = a * hf + b
        hf_ref[pl.ds(rf, SUBLANE), :] = hblk
        hf = hblk[SUBLANE - 1:SUBLANE, :]

        rb = pl.multiple_of((nblk - 1 - i) * SUBLANE, SUBLANE)
        a = a_ref[1, pl.ds(rb, SUBLANE), :]
        b = b_ref[1, pl.ds(rb, SUBLANE), :]
        for s in (1, 2, 4):
            m = row < SUBLANE - s
            a_s = jnp.where(m, pltpu.roll(a, SUBLANE - s, 0), 1.0)
            b_s = jnp.where(m, pltpu.roll(b, SUBLANE - s, 0), 0.0)
            b = a * b_s + b
            a = a * a_s
        hblk = a * hb + b
        hb_ref[pl.ds(rb, SUBLANE), :] = hblk
        hb = hblk[0:1, :]
        return hf, hb

    hf, hb = lax.fori_loop(0, nblk, body, (h0_ref[0:1, :], h0_ref[1:2, :]))
    hfin_ref[0:1, :] = hf
    hfin_ref[1:2, :] = hb
    for r0 in range(0, seq, LRU_RC):
        rows = slice(r0, r0 + LRU_RC)
        y = (hf_ref[rows, :] + hb_ref[rows, :]) * _gelu_tanh(gb_ref[rows, :])
        y_ref[rows, :] = y.astype(y_ref.dtype)


def _lru(proj, h0, p, *, seq, batch, row0, name):
    ct = LRU_CT
    nkb = ct // LRU_BLOCK
    xb_blk0 = (Q_WIDTH + 2 * KV_WIDTH) // ct
    gb_blk0 = (Q_WIDTH + 2 * KV_WIDTH + LRU_WIDTH) // ct
    r0 = row0 // seq
    vec2 = pl.BlockSpec((2, ct), lambda b, c: (0, c))
    wblk = pl.BlockSpec((2, nkb, LRU_BLOCK, LRU_BLOCK), lambda b, c: (0, c, 0, 0))
    return pl.pallas_call(
        functools.partial(_lru_kernel, seq=seq),
        grid=(batch, LRU_WIDTH // ct),
        in_specs=[pl.BlockSpec((seq, ct), lambda b, c: (r0 + b, xb_blk0 + c)),
                  pl.BlockSpec((seq, ct), lambda b, c: (r0 + b, gb_blk0 + c)),
                  pl.BlockSpec((4, ct), lambda b, c: (0, c)),
                  pl.BlockSpec((1, ct), lambda b, c: (0, c)),
                  vec2, wblk, vec2, wblk, vec2,
                  pl.BlockSpec((None, 2, ct), lambda b, c: (b, 0, c))],
        out_specs=[pl.BlockSpec((seq, ct), lambda b, c: (b, c)),
                   pl.BlockSpec((None, 2, ct), lambda b, c: (b, 0, c))],
        out_shape=[jax.ShapeDtypeStruct((batch * seq, LRU_WIDTH), BF16),
                   jax.ShapeDtypeStruct((batch, 2, LRU_WIDTH), F32)],
        scratch_shapes=[pltpu.VMEM((seq + 2 * SUBLANE, ct), F32),
                        pltpu.VMEM((2, seq, ct), F32),
                        pltpu.VMEM((2, seq, ct), F32),
                        pltpu.VMEM((seq, ct), F32),
                        pltpu.VMEM((seq, ct), F32)],
        compiler_params=_cparams(("arbitrary", "arbitrary")),
        name=name,
    )(proj, proj, p['conv_w'], p['conv_b'].reshape(1, LRU_WIDTH), p['lam'],
      p['w_r'], p['b_r'], p['w_i'], p['b_i'], h0)


OUT_TN = 512
OUT_TM = 512


def _outproj_kernel(*refs, n_a, with_router):
    a_refs = refs[:n_a]
    w_refs = refs[n_a:2 * n_a]
    res_ref, gate_ref, g_ref, b_ref = refs[2 * n_a:2 * n_a + 4]
    pos = 2 * n_a + 4
    if with_router:
        sh_ref, sc_ref, rt_ref = refs[pos:pos + 3]
        pos += 3
        o_ref, u_ref, lg_ref, acc_ref = refs[pos:pos + 4]
        pos += 4
    else:
        o_ref, acc_ref = refs[pos:pos + 2]
        pos += 2
    wb_refs = refs[pos:pos + n_a]
    i = pl.program_id(0)
    j = pl.program_id(1)
    nj = pl.num_programs(1)

    @pl.when(i == 0)
    def _():
        for k in range(n_a):
            wb_refs[k][j] = w_refs[k][...].astype(BF16)

    acc = jnp.dot(a_refs[0][...], wb_refs[0][j], preferred_element_type=F32)
    for k in range(1, n_a):
        acc = acc + jnp.dot(a_refs[k][...], wb_refs[k][j], preferred_element_type=F32)
    acc_ref[j] = DN_ALPHA * res_ref[...] + gate_ref[0] * acc

    @pl.when(j == nj - 1)
    def _():
        tm = acc_ref.shape[1]
        n_chunks = acc_ref.shape[0]
        rc = 128

        def chunk(c, _):
            r0 = pl.multiple_of(c * rc, rc)
            ys = [acc_ref[k, pl.ds(r0, rc), :] for k in range(n_chunks)]
            tot = ys[0].sum(-1, keepdims=True)
            for y in ys[1:]:
                tot = tot + y.sum(-1, keepdims=True)
            mu = tot / D_MODEL
            sq = None
            for y in ys:
                t = ((y - mu) * (y - mu)).sum(-1, keepdims=True)
                sq = t if sq is None else sq + t
            rstd = lax.rsqrt(sq / D_MODEL + LN_EPS)
            lg = None
            for k, y in enumerate(ys):
                cols = slice(k * OUT_TN, (k + 1) * OUT_TN)
                xn = (y - mu) * rstd * g_ref[:, cols] + b_ref[:, cols]
                o_ref[pl.ds(r0, rc), cols] = xn
                if with_router:
                    u = xn * (1.0 + sc_ref[0][:, cols]) + sh_ref[0][:, cols]
                    u_ref[pl.ds(r0, rc), cols] = u.astype(BF16)
                    t = _dot3(u, rt_ref[cols, :])
                    lg = t if lg is None else lg + t
            if with_router:
                lg_ref[pl.ds(r0, rc), :] = lg
            return 0

        lax.fori_loop(0, tm // rc, chunk, 0)


def _outproj(a_list, w, res, mods, gate_chunk, g, b, *, name, router=None, router_chunks=None):
    tm = OUT_TM
    tn = OUT_TN
    nj = D_MODEL // tn
    n_a = len(a_list)
    in_specs = []
    k0 = 0
    w_specs = []
    wb_shapes = []
    for a in a_list:
        ka = a.shape[1]
        in_specs.append(pl.BlockSpec((tm, ka), lambda i, j: (i, 0)))
        w_specs.append(pl.BlockSpec((ka, tn),
                                    lambda i, j, blk=k0 // ka: (blk, _first_pass_block(i, j, nj))))
        wb_shapes.append(pltpu.VMEM((nj, ka, tn), BF16))
        k0 += ka
    in_specs += w_specs
    in_specs += [pl.BlockSpec((tm, tn), lambda i, j: (i, j)),
                 _mod_spec(tm, gate_chunk, tn, col_of=lambda j: j),
                 pl.BlockSpec((1, D_MODEL), lambda i, j: (0, 0)),
                 pl.BlockSpec((1, D_MODEL), lambda i, j: (0, 0))]
    args = list(a_list) + [w] * n_a + [res, mods, g.reshape(1, D_MODEL), b.reshape(1, D_MODEL)]
    out_specs = [pl.BlockSpec((tm, D_MODEL), lambda i, j: (i, 0))]
    out_shape = [jax.ShapeDtypeStruct((N_TOK, D_MODEL), F32)]
    if router is not None:
        in_specs += [_mod_spec(tm, router_chunks[0]), _mod_spec(tm, router_chunks[1]),
                     pl.BlockSpec((D_MODEL, LANE), lambda i, j: (0, 0))]
        args += [mods, mods, router]
        out_specs += [pl.BlockSpec((tm, D_MODEL), lambda i, j: (i, 0)),
                      pl.BlockSpec((tm, LANE), lambda i, j: (i, 0))]
        out_shape += [jax.ShapeDtypeStruct((N_TOK, D_MODEL), BF16),
                      jax.ShapeDtypeStruct((N_TOK, LANE), F32)]
    return pl.pallas_call(
        functools.partial(_outproj_kernel, n_a=n_a, with_router=router is not None),
        grid=(N_TOK // tm, nj),
        in_specs=in_specs,
        out_specs=out_specs,
        out_shape=out_shape,
        scratch_shapes=[pltpu.VMEM((nj, tm, tn), F32)] + wb_shapes,
        compiler_params=_cparams(("arbitrary", "arbitrary")),
        name=name,
    )(*args)


FFN_TF = 512


def _cast_kernel(x_ref, o_ref):
    o_ref[...] = x_ref[...].astype(o_ref.dtype)


def _cast_bf16(w, rows, name):
    r, c = w.shape
    return pl.pallas_call(
        _cast_kernel,
        grid=(r // rows,),
        in_specs=[pl.BlockSpec((rows, c), lambda i: (i, 0))],
        out_specs=pl.BlockSpec((rows, c), lambda i: (i, 0)),
        out_shape=jax.ShapeDtypeStruct((r, c), BF16),
        compiler_params=_cparams(("arbitrary",)),
        name=name,
    )(w)


def _ffn_kernel(x_ref, sh_ref, sc_ref, gate_ref, g_ref, b_ref, w1_ref, w3_ref, w2_ref, o_ref, u_ref):
    j = pl.program_id(1)
    nj = pl.num_programs(1)

    @pl.when(j == 0)
    def _():
        u_ref[...] = (x_ref[...] * (1.0 + sc_ref[0]) + sh_ref[0]).astype(BF16)
        o_ref[...] = jnp.zeros(o_ref.shape, F32)

    u = u_ref[...]
    h1 = jnp.dot(u, w1_ref[...], preferred_element_type=F32)
    h3 = jnp.dot(u, w3_ref[...], preferred_element_type=F32)
    h = (_silu(h1) * h3).astype(BF16)
    o_ref[...] += jnp.dot(h, w2_ref[...], preferred_element_type=F32)

    @pl.when(j == nj - 1)
    def _():
        rc = 128

        def chunk(c, _):
            r0 = pl.multiple_of(c * rc, rc)
            y = DN_ALPHA * x_ref[pl.ds(r0, rc), :] + gate_ref[0] * o_ref[pl.ds(r0, rc), :]
            o_ref[pl.ds(r0, rc), :] = _layer_norm_rows(y, g_ref[...], b_ref[...])
            return 0

        lax.fori_loop(0, o_ref.shape[0] // rc, chunk, 0)


def _ffn(x, mods, p):
    tm = ROW_TILE
    tf = FFN_TF
    vec = pl.BlockSpec((1, D_MODEL), lambda i, j: (0, 0))
    return pl.pallas_call(
        _ffn_kernel,
        grid=(N_TOK // tm, D_FF // tf),
        in_specs=[pl.BlockSpec((tm, D_MODEL), lambda i, j: (i, 0), pipeline_mode=pl.Buffered(1)),
                  _mod_spec(tm, 3), _mod_spec(tm, 4), _mod_spec(tm, 5), vec, vec,
                  pl.BlockSpec((D_MODEL, tf), lambda i, j: (0, j)),
                  pl.BlockSpec((D_MODEL, tf), lambda i, j: (0, j)),
                  pl.BlockSpec((tf, D_MODEL), lambda i, j: (j, 0))],
        out_specs=pl.BlockSpec((tm, D_MODEL), lambda i, j: (i, 0)),
        out_shape=jax.ShapeDtypeStruct((N_TOK, D_MODEL), F32),
        scratch_shapes=[pltpu.VMEM((tm, D_MODEL), BF16)],
        compiler_params=_cparams(("arbitrary", "arbitrary")),
        name="ffn_swiglu",
    )(x, mods, mods, mods, p['ln2_g'].reshape(1, D_MODEL), p['ln2_b'].reshape(1, D_MODEL),
      _cast_bf16(p['ffn_w1'], 256, "ffn_w1_bf16"), _cast_bf16(p['ffn_w3'], 256, "ffn_w3_bf16"),
      _cast_bf16(p['ffn_w2'], 512, "ffn_w2_bf16"))


def _filter_mlp_kernel(feat_ref, w1_ref, b1_ref, f1_ref, w2_ref, b2_ref, f2_ref, o_ref):
    h = jnp.sin(f1_ref[...] * (_dot3(feat_ref[...], w1_ref[...]) + b1_ref[...]))
    o_ref[...] = jnp.sin(f2_ref[...] * (_dot3(h, w2_ref[...]) + b2_ref[...]))


def _filter_features(seq):
    t = jnp.arange(seq, dtype=F32)
    t01 = t / (seq - 1)
    w = 2.0 * math.pi * t / seq
    f = jnp.linspace(1e-4, HYENA_BANDS - 1, HYENA_BANDS, dtype=F32)
    fw = w[:, None] * f[None, :]
    feat = jnp.concatenate([t01[:, None], jnp.cos(fw), -jnp.sin(fw)], -1)
    return jnp.pad(feat, ((0, 0), (0, LANE - HYENA_EMB))), t01[:, None]


def _filter_mlp(seq, p):
    feat, t01 = _filter_features(seq)
    hid = HYENA_FILTER_HIDDEN
    w1 = jnp.pad(p['filt_w1'], ((0, LANE - HYENA_EMB), (0, 0)))
    row = lambda v: v.reshape(1, hid)
    h2 = pl.pallas_call(
        _filter_mlp_kernel,
        out_shape=jax.ShapeDtypeStruct((seq, hid), F32),
        compiler_params=pltpu.CompilerParams(vmem_limit_bytes=VMEM_LIMIT),
        name=f"hyena_filter_mlp_{seq}",
    )(feat, w1, row(p['filt_b1']), row(p['filt_f1']), p['filt_w2'], row(p['filt_b2']), row(p['filt_f2']))
    return h2, t01


def _dft_matrices(seq):
    n = 2 * seq
    f = jnp.arange(seq, dtype=jnp.int32)
    k = (f[:, None] * f[None, :]) % n
    ang = k.astype(F32) * (math.pi / seq)
    c = jnp.cos(ang)
    s = -jnp.sin(ang)
    alt = jnp.where(f % 2 == 0, 1.0, -1.0).astype(F32)
    s_fwd = s.at[0, :].set(alt)
    s_inv = s.at[:, 0].set(alt)
    fwd = jnp.concatenate([c, s_fwd], axis=0).astype(BF16)
    inv = jnp.concatenate([c, s_inv], axis=1).astype(BF16)
    return fwd, inv, alt[:, None]


HY_RC = 128


def _hyena_kernel(x1_ref, x2_ref, z_ref, sw1_ref, sw2_ref, swz_ref, sb1_ref, sb2_ref, sbz_ref,
                  h2_ref, t01_ref, alt_ref, w3_ref, dec_ref, bias_ref, fwd_ref, inv_ref,
                  o_ref, kr_ref, ki_ref, zb_ref, zf_ref, yb_ref, pad_ref, g1_ref, g2_ref, zc_ref,
                  *, seq, nb, ct):
    nrc = seq // HY_RC

    @pl.when(pl.program_id(1) == 0)
    def _():
        t01 = t01_ref[...]
        h2 = h2_ref[...]
        row0 = lax.broadcasted_iota(jnp.int32, (seq, ct), 0) == 0
        wf = jnp.where(row0, 1.0, 2.0) / (2.0 * seq)
        for n in range(HYENA_ORDER):
            kpos = _dot3(h2, w3_ref[n]) * jnp.exp(-t01 * jnp.abs(dec_ref[n:n + 1, :]))
            kneg = _dot3(h2, w3_ref[HYENA_ORDER + n]) * jnp.exp(
                -t01 * jnp.abs(dec_ref[HYENA_ORDER + n:HYENA_ORDER + n + 1, :]))
            kneg = jnp.where(row0, 0.0, kneg)
            ksum = kpos + kneg
            kdif = kpos - kneg
            kr = jnp.dot(fwd_ref[0:seq, :], ksum.astype(BF16), preferred_element_type=F32)
            ki = jnp.dot(fwd_ref[seq:2 * seq, :], kdif.astype(BF16), preferred_element_type=F32)
            nyq = jnp.sum(alt_ref[...] * ksum, axis=0, keepdims=True) / (2.0 * seq)
            kr_ref[n] = kr * wf
            ki_ref[n] = jnp.where(row0, nyq, ki * wf)

    zeros = jnp.zeros((SUBLANE, ct), F32)
    pad_ref[0:SUBLANE, :] = zeros
    pad_ref[seq + SUBLANE:seq + 2 * SUBLANE, :] = zeros

    def short_conv(src_ref, rows0, w_ref, b_ref, dst_ref):
        pad_ref[SUBLANE:seq + SUBLANE, :] = src_ref[rows0:rows0 + seq, :]
        w = w_ref[...]
        for c in range(nrc):
            r0 = c * HY_RC
            acc = b_ref[...] + w[0:1, :] * pad_ref[r0 + 7:r0 + 7 + HY_RC, :]
            acc = acc + w[1:2, :] * pad_ref[r0 + 8:r0 + 8 + HY_RC, :]
            acc = acc + w[2:3, :] * pad_ref[r0 + 9:r0 + 9 + HY_RC, :]
            dst_ref[r0:r0 + HY_RC, :] = acc

    for bi in range(nb):
        rows0 = bi * seq
        short_conv(x1_ref, rows0, sw1_ref, sb1_ref, g1_ref)
        short_conv(x2_ref, rows0, sw2_ref, sb2_ref, g2_ref)
        short_conv(z_ref, rows0, swz_ref, sbz_ref, zc_ref)
        for n, gate_ref in enumerate((g1_ref, g2_ref)):
            for c in range(nrc):
                rows = slice(c * HY_RC, (c + 1) * HY_RC)
                zb_ref[rows, :] = zc_ref[rows, :].astype(BF16)
            zf_ref[...] = jnp.dot(fwd_ref[...], zb_ref[...], preferred_element_type=F32)
            for c in range(nrc):
                rows = slice(c * HY_RC, (c + 1) * HY_RC)
                rows_i = slice(seq + c * HY_RC, seq + (c + 1) * HY_RC)
                zr = zf_ref[rows, :]
                zi = zf_ref[rows_i, :]
                kr = kr_ref[n, rows, :]
                ki = ki_ref[n, rows, :]
                yr = zr * kr - zi * ki
                yi = zr * ki + zi * kr
                if c == 0:
                    first = lax.broadcasted_iota(jnp.int32, (HY_RC, ct), 0) == 0
                    yr = jnp.where(first, zr * kr, yr)
                    yi = jnp.where(first, zi * ki, yi)
                yb_ref[rows, :] = yr.astype(BF16)
                yb_ref[rows_i, :] = yi.astype(BF16)
            zf_ref[0:seq, :] = jnp.dot(inv_ref[...], yb_ref[...], preferred_element_type=F32)
            for c in range(nrc):
                rows = slice(c * HY_RC, (c + 1) * HY_RC)
                zc = zc_ref[rows, :]
                znew = gate_ref[rows, :] * (zf_ref[rows, :] + zc * bias_ref[n:n + 1, :])
                if n == HYENA_ORDER - 1:
                    o_ref[rows0 + c * HY_RC:rows0 + (c + 1) * HY_RC, :] = znew.astype(o_ref.dtype)
                else:
                    zc_ref[rows, :] = znew


def _hyena(proj, p, *, seq, batch, row0, nb, ct, name):
    nct = D_MODEL // ct
    h2, t01 = _filter_mlp(seq, p)
    fwd, inv, alt = _dft_matrices(seq)
    rb0 = row0 // (nb * seq)
    w3 = p['filt_w3'].reshape(HYENA_FILTER_HIDDEN, 2 * HYENA_ORDER, D_MODEL).transpose(1, 0, 2)
    dec = p['filt_decay'].reshape(2 * HYENA_ORDER, D_MODEL)
    sw = p['short_w']
    sb = p['short_b'].reshape(1, 3 * D_MODEL)
    slab = lambda k: pl.BlockSpec((nb * seq, ct), lambda c, b: (rb0 + b, k * nct + c))
    swk = lambda k: pl.BlockSpec((3, ct), lambda c, b: (0, k * nct + c))
    sbk = lambda k: pl.BlockSpec((1, ct), lambda c, b: (0, k * nct + c))
    const = lambda shape: pl.BlockSpec(shape, lambda c, b: tuple(0 for _ in shape))
    return pl.pallas_call(
        functools.partial(_hyena_kernel, seq=seq, nb=nb, ct=ct),
        grid=(nct, batch // nb),
        in_specs=[slab(0), slab(1), slab(2), swk(0), swk(1), swk(2), sbk(0), sbk(1), sbk(2),
                  const((seq, HYENA_FILTER_HIDDEN)), const((seq, 1)), const((seq, 1)),
                  pl.BlockSpec((2 * HYENA_ORDER, HYENA_FILTER_HIDDEN, ct), lambda c, b: (0, 0, c)),
                  pl.BlockSpec((2 * HYENA_ORDER, ct), lambda c, b: (0, c)),
                  pl.BlockSpec((HYENA_ORDER, ct), lambda c, b: (0, c)),
                  pl.BlockSpec((2 * seq, seq), lambda c, b: (0, 0), pipeline_mode=pl.Buffered(1)),
                  pl.BlockSpec((seq, 2 * seq), lambda c, b: (0, 0), pipeline_mode=pl.Buffered(1))],
        out_specs=pl.BlockSpec((nb * seq, ct), lambda c, b: (b, c)),
        out_shape=jax.ShapeDtypeStruct((batch * seq, D_MODEL), BF16),
        scratch_shapes=[pltpu.VMEM((HYENA_ORDER, seq, ct), F32),
                        pltpu.VMEM((HYENA_ORDER, seq, ct), F32),
                        pltpu.VMEM((seq, ct), BF16),
                        pltpu.VMEM((2 * seq, ct), F32),
                        pltpu.VMEM((2 * seq, ct), BF16),
                        pltpu.VMEM((seq + 2 * SUBLANE, ct), F32),
                        pltpu.VMEM((seq, ct), F32),
                        pltpu.VMEM((seq, ct), F32),
                        pltpu.VMEM((seq, ct), F32)],
        compiler_params=_cparams(("arbitrary", "arbitrary")),
        name=name,
    )(proj, proj, proj, sw, sw, sw, sb, sb, sb, h2, t01, alt, w3, dec, p['filt_bias'], fwd, inv)


def _moe_kernel(ie_ref, ib_ref, ir_ref, x_ref, w1_ref, w3_ref, w2_ref, o_hbm,
                acc_ref, w1b_ref, w3b_ref, w2b_ref, sem):
    w = pl.program_id(0)
    j = pl.program_id(1)
    nj = pl.num_programs(1)
    rows = ir_ref[w]
    ntiles = (rows + (MOE_ROW_TILE - 1)) // MOE_ROW_TILE
    rt = MOE_ROW_TILE

    def partial_out(r):
        off = pl.multiple_of(r * rt, rt)
        xt = x_ref[pl.ds(off, rt), :]
        h1 = jnp.dot(xt, w1b_ref[...], preferred_element_type=F32)
        h3 = jnp.dot(xt, w3b_ref[...], preferred_element_type=F32)
        h = (_silu(h1) * h3).astype(BF16)
        return off, jnp.dot(h, w2b_ref[...], preferred_element_type=F32)

    @pl.when(rows > 0)
    def _():
        w1b_ref[...] = w1_ref[...].astype(BF16)
        w3b_ref[...] = w3_ref[...].astype(BF16)
        w2b_ref[...] = w2_ref[...].astype(BF16)

        def run_tiles(first):
            def one(r):
                off, part = partial_out(r)
                if first:
                    acc_ref[pl.ds(off, rt), :] = part
                else:
                    acc_ref[pl.ds(off, rt), :] += part

            def pair(q, _):
                one(2 * q)
                one(2 * q + 1)
                return 0

            lax.fori_loop(0, ntiles // 2, pair, 0)

            @pl.when(ntiles % 2 == 1)
            def _():
                one(ntiles - 1)

        @pl.when(j == 0)
        def _():
            run_tiles(True)

        @pl.when(j > 0)
        def _():
            run_tiles(False)

        @pl.when(j == nj - 1)
        def _():
            base = ib_ref[w] * MOE_CHUNK

            def tile_copy(off):
                return pltpu.make_async_copy(acc_ref.at[pl.ds(off, rt)],
                                             o_hbm.at[pl.ds(base + off, rt)], sem)

            def start(r, _):
                tile_copy(pl.multiple_of(r * rt, rt)).start()
                return 0

            def wait(r, _):
                tile_copy(pl.multiple_of(r * rt, rt)).wait()
                return 0

            lax.fori_loop(0, ntiles, start, 0)
            lax.fori_loop(0, ntiles, wait, 0)


def _moe_experts(xs, item_expert, item_block, item_rows, p):
    tj = MOE_FF_TILE
    nj = D_FF_EXPERT // tj

    def jeff(w, j, ir):
        return jnp.where(ir[w] > 0, j, nj - 1)

    grid_spec = pltpu.PrefetchScalarGridSpec(
        num_scalar_prefetch=3,
        grid=(MOE_ITEMS, nj),
        in_specs=[pl.BlockSpec((MOE_CHUNK, D_MODEL), lambda w, j, ie, ib, ir: (ib[w], 0),
                               pipeline_mode=pl.Buffered(1)),
                  pl.BlockSpec((None, D_MODEL, tj), lambda w, j, ie, ib, ir: (ie[w], 0, jeff(w, j, ir))),
                  pl.BlockSpec((None, D_MODEL, tj), lambda w, j, ie, ib, ir: (ie[w], 0, jeff(w, j, ir))),
                  pl.BlockSpec((None, tj, D_MODEL), lambda w, j, ie, ib, ir: (ie[w], jeff(w, j, ir), 0))],
        out_specs=pl.BlockSpec(memory_space=pl.ANY),
        scratch_shapes=[pltpu.VMEM((MOE_CHUNK, D_MODEL), F32),
                        pltpu.VMEM((D_MODEL, tj), BF16),
                        pltpu.VMEM((D_MODEL, tj), BF16),
                        pltpu.VMEM((tj, D_MODEL), BF16),
                        pltpu.SemaphoreType.DMA(())],
    )
    return pl.pallas_call(
        _moe_kernel,
        grid_spec=grid_spec,
        out_shape=jax.ShapeDtypeStruct((MOE_ROWS, D_MODEL), F32),
        compiler_params=_cparams(("arbitrary", "arbitrary")),
        name="moe_experts",
    )(item_expert, item_block, item_rows, xs, p['exp_w1'], p['exp_w3'], p['exp_w2'])


ROUTE_TM = 256
R_E0, R_E1, R_P0, R_P1, R_RANK0, R_RANK1 = range(6)


def _route_kernel(lg_ref, o_ref, cnt_ref, carry_ref):
    i = pl.program_id(0)
    tm = ROUTE_TM

    @pl.when(i == 0)
    def _():
        carry_ref[...] = jnp.zeros(carry_ref.shape, F32)

    lane = lax.broadcasted_iota(jnp.int32, (tm, LANE), 1)
    lg = jnp.where(lane < N_EXPERTS, lg_ref[...], -jnp.inf)
    m0 = jnp.max(lg, -1, keepdims=True)
    e0 = jnp.min(jnp.where(lg == m0, lane, LANE), -1, keepdims=True)
    lg1 = jnp.where(lane == e0, -jnp.inf, lg)
    m1 = jnp.max(lg1, -1, keepdims=True)
    e1 = jnp.min(jnp.where(lg1 == m1, lane, LANE), -1, keepdims=True)
    t = jnp.exp(m1 - m0)
    p0 = 1.0 / (1.0 + t)
    p1 = t / (1.0 + t)
    hit = ((lane == e0) | (lane == e1)).astype(BF16)
    r_i = lax.broadcasted_iota(jnp.int32, (tm, tm), 0)
    c_i = lax.broadcasted_iota(jnp.int32, (tm, tm), 1)
    before = (c_i < r_i).astype(BF16)
    pref = jnp.dot(before, hit, preferred_element_type=F32) + carry_ref[0:1, :]
    rank0 = jnp.sum(jnp.where(lane == e0, pref, 0.0), -1, keepdims=True)
    rank1 = jnp.sum(jnp.where(lane == e1, pref, 0.0), -1, keepdims=True)
    carry_ref[0:1, :] = carry_ref[0:1, :] + jnp.sum(hit.astype(F32), axis=0, keepdims=True)
    rec = jnp.zeros((tm, LANE), F32)
    for k, v in ((R_E0, e0.astype(F32)), (R_E1, e1.astype(F32)), (R_P0, p0), (R_P1, p1),
                 (R_RANK0, rank0), (R_RANK1, rank1)):
        rec = jnp.where(lane == k, v, rec)
    o_ref[...] = rec
    cnt_ref[...] = carry_ref[...]


def _route_records(logits):
    return pl.pallas_call(
        _route_kernel,
        grid=(N_TOK // ROUTE_TM,),
        in_specs=[pl.BlockSpec((ROUTE_TM, LANE), lambda i: (i, 0))],
        out_specs=[pl.BlockSpec((ROUTE_TM, LANE), lambda i: (i, 0)),
                   pl.BlockSpec((SUBLANE, LANE), lambda i: (0, 0))],
        out_shape=[jax.ShapeDtypeStruct((N_TOK, LANE), F32),
                   jax.ShapeDtypeStruct((SUBLANE, LANE), F32)],
        scratch_shapes=[pltpu.VMEM((SUBLANE, LANE), F32)],
        compiler_params=_cparams(("arbitrary",)),
        name="moe_route",
    )(logits)


def _route(logits):
    rec, cnt = _route_records(logits)
    e_flat = rec[:, R_E0:R_E1 + 1].astype(jnp.int32).reshape(-1)
    rank = rec[:, R_RANK0:R_RANK1 + 1].astype(jnp.int32).reshape(-1)
    counts = cnt[0, :N_EXPERTS].astype(jnp.int32)
    blocks = (counts + MOE_CHUNK - 1) // MOE_CHUNK
    bend = jnp.cumsum(blocks)
    bstart = bend - blocks
    total = bend[-1]
    pos = bstart[e_flat] * MOE_CHUNK + rank
    w = jnp.arange(MOE_ITEMS, dtype=jnp.int32)
    w_eff = jnp.minimum(w, total - 1)
    item_expert = jnp.minimum(jnp.sum((w_eff[:, None] >= bend[None, :]).astype(jnp.int32), axis=1),
                              N_EXPERTS - 1)
    item_rows = jnp.where(w < total,
                          jnp.clip(counts[item_expert] - (w - bstart[item_expert]) * MOE_CHUNK, 0, MOE_CHUNK),
                          0)
    src = jnp.zeros((MOE_ROWS,), jnp.int32).at[pos].set(
        jnp.arange(N_TOK * TOP_K, dtype=jnp.int32) // TOP_K)
    return rec, pos.reshape(N_TOK, TOP_K), src, item_expert.astype(jnp.int32), \
        w_eff.astype(jnp.int32), item_rows.astype(jnp.int32)


def _postnorm_kernel(x_ref, y0_ref, y1_ref, rec_ref, gate_ref, g_ref, b_ref, o_ref):
    delta = (rec_ref[:, R_P0:R_P0 + 1] * y0_ref[...] + rec_ref[:, R_P1:R_P1 + 1] * y1_ref[...])
    y = DN_ALPHA * x_ref[...] + gate_ref[0] * delta
    o_ref[...] = _layer_norm_rows(y, g_ref[...], b_ref[...])


def _postnorm(x, y0, y1, rec, mods, gate_chunk, g, b):
    tm = 256
    vec = pl.BlockSpec((1, D_MODEL), lambda i: (0, 0))
    rows = pl.BlockSpec((tm, D_MODEL), lambda i: (i, 0))
    return pl.pallas_call(
        _postnorm_kernel,
        grid=(N_TOK // tm,),
        in_specs=[rows, rows, rows, pl.BlockSpec((tm, LANE), lambda i: (i, 0)),
                  _mod_spec(tm, gate_chunk), vec, vec],
        out_specs=rows,
        out_shape=jax.ShapeDtypeStruct((N_TOK, D_MODEL), F32),
        compiler_params=_cparams(("arbitrary",)),
        name="final_postnorm",
    )(x, y0, y1, rec, mods, g.reshape(1, D_MODEL), b.reshape(1, D_MODEL))


def kernel(x_prompt, x_sample, c, c_ctx, cache_l0_k, cache_l0_v, state_l0_lru, l0_ada_w, l0_ada_b, l0_w_in, l0_q_norm, l0_k_norm, l0_lru_conv_w, l0_lru_conv_b, l0_lru_lambda, l0_lru_w_r, l0_lru_b_r, l0_lru_w_i, l0_lru_b_i, l0_w_out, l0_ln1_g, l0_ln1_b, l0_ffn_w1, l0_ffn_w3, l0_ffn_w2, l0_ln2_g, l0_ln2_b, l1_ada_w, l1_ada_b, l1_w_in, l1_short_w, l1_short_b, l1_filt_w1, l1_filt_b1, l1_filt_f1, l1_filt_w2, l1_filt_b2, l1_filt_f2, l1_filt_w3, l1_filt_decay, l1_filt_bias, l1_w_out, l1_ln1_g, l1_ln1_b, l1_router, l1_exp_w1, l1_exp_w3, l1_exp_w2, l1_ln2_g, l1_ln2_b):
    x = jnp.concatenate([x_prompt.reshape(TOK_P, D_MODEL), x_sample.reshape(TOK_S, D_MODEL)], axis=0)
    cond = jnp.concatenate([c_ctx[None, :], c, jnp.zeros((N_COND - 1 - DEC_BATCH, D_MODEL), F32)], axis=0)
    mods0 = _ada(cond, l0_ada_w, l0_ada_b)
    mods1 = _ada(cond, l1_ada_w, l1_ada_b)

    lru_p = dict(conv_w=l0_lru_conv_w, conv_b=l0_lru_conv_b, lam=l0_lru_lambda,
                 w_r=l0_lru_w_r, b_r=l0_lru_b_r, w_i=l0_lru_w_i, b_i=l0_lru_b_i)
    proj0 = _proj(x, mods0, l0_w_in, tn=512, name="l0_in_proj")
    attn_p, new_k, new_v = _attn_context(proj0, l0_q_norm, l0_k_norm)
    attn_s = _attn_latent(proj0, cache_l0_k, cache_l0_v, l0_q_norm, l0_k_norm)
    lru_p_out, new_h = _lru(proj0, jnp.zeros((BATCH, 2, LRU_WIDTH), F32), lru_p,
                            seq=SEQ, batch=BATCH, row0=0, name="lru_context")
    lru_s_out, _ = _lru(proj0, state_l0_lru, lru_p,
                        seq=DEC_SEQ, batch=DEC_BATCH, row0=TOK_P, name="lru_latent")
    attn = jnp.concatenate([attn_p, attn_s], axis=0)
    lru = jnp.concatenate([lru_p_out, lru_s_out], axis=0)
    x = _outproj([attn, lru], l0_w_out, x, mods0, 2, l0_ln1_g, l0_ln1_b, name="l0_out_proj")[0]
    x = _ffn(x, mods0, dict(ffn_w1=l0_ffn_w1, ffn_w3=l0_ffn_w3, ffn_w2=l0_ffn_w2,
                            ln2_g=l0_ln2_g, ln2_b=l0_ln2_b))

    hy_p = dict(short_w=l1_short_w, short_b=l1_short_b, filt_w1=l1_filt_w1, filt_b1=l1_filt_b1,
                filt_f1=l1_filt_f1, filt_w2=l1_filt_w2, filt_b2=l1_filt_b2, filt_f2=l1_filt_f2,
                filt_w3=l1_filt_w3, filt_decay=l1_filt_decay, filt_bias=l1_filt_bias)
    proj1 = _proj(x, mods1, l1_w_in, tn=512, name="l1_in_proj")
    z_p = _hyena(proj1, hy_p, seq=SEQ, batch=BATCH, row0=0, nb=4, ct=512, name="hyena_context")
    z_s = _hyena(proj1, hy_p, seq=DEC_SEQ, batch=DEC_BATCH, row0=TOK_P, nb=1, ct=256, name="hyena_latent")
    z = jnp.concatenate([z_p, z_s], axis=0)
    router = jnp.pad(l1_router, ((0, 0), (0, LANE - N_EXPERTS)))
    x, u, logits = _outproj([z], l1_w_out, x, mods1, 2, l1_ln1_g, l1_ln1_b, name="l1_out_proj",
                            router=router, router_chunks=(3, 4))
    rec, pos, src, item_expert, item_block, item_rows = _route(logits)
    xs = jnp.take(u, src, axis=0)
    ys = _moe_experts(xs, item_expert, item_block, item_rows,
                      dict(exp_w1=l1_exp_w1, exp_w3=l1_exp_w3, exp_w2=l1_exp_w2))
    x = _postnorm(x, jnp.take(ys, pos[:, 0], axis=0), jnp.take(ys, pos[:, 1], axis=0), rec,
                  mods1, 5, l1_ln2_g, l1_ln2_b)

    y_prompt = x[:TOK_P].reshape(BATCH, SEQ, D_MODEL)
    y_sample = x[TOK_P:].reshape(DEC_BATCH, DEC_SEQ, D_MODEL)
    return (y_prompt, y_sample,
            new_k.reshape(BATCH, SEQ, N_KV_HEADS, HEAD_DIM),
            new_v.reshape(BATCH, SEQ, N_KV_HEADS, HEAD_DIM),
            new_h)
```

```python
import functools
import math

import jax
import jax.numpy as jnp
from jax import lax
from jax.experimental import pallas as pl
from jax.experimental.pallas import tpu as pltpu

F32 = jnp.float32
BF16 = jnp.bfloat16

D_MODEL = 2048
BATCH = 16
SEQ = 256
DEC_BATCH = 2
DEC_SEQ = 1024
PAST_LEN = 256
GRID_W = 64
HEAD_DIM = 128
N_Q_HEADS = 8
N_KV_HEADS = 2
Q_GROUP = N_Q_HEADS // N_KV_HEADS
Q_WIDTH = N_Q_HEADS * HEAD_DIM
KV_WIDTH = N_KV_HEADS * HEAD_DIM
ROPE_THETA = 10000.0
LRU_WIDTH = D_MODEL // 2
LRU_BLOCK = 128
LRU_C = 8.0
MIX0_IN = Q_WIDTH + 2 * KV_WIDTH + 2 * LRU_WIDTH
HYENA_ORDER = 2
HYENA_EMB = 33
HYENA_BANDS = (HYENA_EMB - 1) // 2
HYENA_FILTER_HIDDEN = 64
D_FF = 5632
N_EXPERTS = 8
TOP_K = 2
D_FF_EXPERT = 7168
N_MOD = 6
LN_EPS = 1e-5
QK_EPS = 1e-6
DEPTH = 2
DN_ALPHA = (2 * DEPTH) ** 0.25

TOK_P = BATCH * SEQ
TOK_S = DEC_BATCH * DEC_SEQ
N_TOK = TOK_P + TOK_S
N_COND = 8
LANE = 128
SUBLANE = 8
VMEM_LIMIT = 56 * 1024 * 1024

ROW_TILE = 1024
MOE_CHUNK = 2048
MOE_ROW_TILE = 256
MOE_FF_TILE = 256
MOE_ITEMS = N_EXPERTS + (N_TOK * TOP_K) // MOE_CHUNK
MOE_ROWS = MOE_ITEMS * MOE_CHUNK


def _cparams(sem):
    return pltpu.CompilerParams(dimension_semantics=sem, vmem_limit_bytes=VMEM_LIMIT)


def _cond_of_tile(i, tm):
    return jnp.maximum(i * tm // DEC_SEQ - (TOK_P // DEC_SEQ - 1), 0)


def _mod_spec(tm, chunk, width=D_MODEL, col_of=None):
    per = D_MODEL // width
    if col_of is None:
        return pl.BlockSpec((1, 1, width), lambda i, *_: (_cond_of_tile(i, tm), 0, chunk * per))
    return pl.BlockSpec((1, 1, width),
                        lambda i, j, *_: (_cond_of_tile(i, tm), 0, chunk * per + col_of(j)))


def _silu(x):
    return x * jax.nn.sigmoid(x)


def _split_bf16(x):
    hi = x.astype(BF16)
    lo = (x - hi.astype(F32)).astype(BF16)
    return hi, lo


def _pack_bf16_pair(lo, hi):
    lo_bits = lax.bitcast_convert_type(lo.astype(BF16).astype(F32), jnp.uint32) >> 16
    hi_bits = lax.bitcast_convert_type(hi.astype(BF16).astype(F32), jnp.uint32) & jnp.uint32(0xFFFF0000)
    return hi_bits | lo_bits


def _unpack_bf16_pair(words):
    lo = lax.bitcast_convert_type(words << 16, F32)
    hi = lax.bitcast_convert_type(words & jnp.uint32(0xFFFF0000), F32)
    return jnp.concatenate([lo, hi], axis=1).astype(BF16)


def _rows(a, idx):
    return a.at[idx].get(mode="promise_in_bounds")


def _dot3(a, b):
    ah, al = _split_bf16(a)
    bh, bl = _split_bf16(b)
    d = lambda x, y: jnp.dot(x, y, preferred_element_type=F32)
    return d(ah, bh) + (d(ah, bl) + d(al, bh))


def _layer_norm_rows(y, g, b):
    mu = jnp.mean(y, -1, keepdims=True)
    yc = y - mu
    var = jnp.mean(yc * yc, -1, keepdims=True)
    return yc * lax.rsqrt(var + LN_EPS) * g + b


def _ada_kernel(c_ref, w_ref, b_ref, o_ref):
    s = _silu(c_ref[...]).astype(BF16)
    o_ref[...] = jnp.dot(s, w_ref[...].astype(BF16), preferred_element_type=F32) + b_ref[...]


def _ada(cond, w, b):
    tn = 1024
    n = w.shape[1]
    out = pl.pallas_call(
        _ada_kernel,
        grid=(n // tn,),
        in_specs=[pl.BlockSpec((N_COND, D_MODEL), lambda j: (0, 0)),
                  pl.BlockSpec((D_MODEL, tn), lambda j: (0, j)),
                  pl.BlockSpec((1, tn), lambda j: (0, j))],
        out_specs=pl.BlockSpec((N_COND, tn), lambda j: (0, j)),
        out_shape=jax.ShapeDtypeStruct((N_COND, n), F32),
        compiler_params=_cparams(("arbitrary",)),
        name="ada_modulation",
    )(cond, w, b.reshape(1, n))
    return out.reshape(N_COND, 1, n)


def _cast_kernel(x_ref, o_ref):
    o_ref[...] = x_ref[...].astype(o_ref.dtype)


def _cast_bf16(w, rows, name):
    r, c = w.shape
    return pl.pallas_call(
        _cast_kernel,
        grid=(r // rows,),
        in_specs=[pl.BlockSpec((rows, c), lambda i: (i, 0))],
        out_specs=pl.BlockSpec((rows, c), lambda i: (i, 0)),
        out_shape=jax.ShapeDtypeStruct((r, c), BF16),
        compiler_params=_cparams(("arbitrary",)),
        name=name,
    )(w)


def _token_row_specs(n_parts, tm, width, col=None, buffered_once=False):
    kw = dict(pipeline_mode=pl.Buffered(1)) if buffered_once else {}
    col = col or (lambda i, *a: 0)
    if n_parts == 1:
        return [pl.BlockSpec((tm, width), lambda i, *a: (i, col(i, *a)), **kw)]
    n_p = TOK_P // tm
    return [pl.BlockSpec((tm, width), lambda i, *a: (jnp.minimum(i, n_p - 1), col(i, *a)), **kw),
            pl.BlockSpec((tm, width), lambda i, *a: (jnp.maximum(i - n_p, 0), col(i, *a)), **kw)]


def _proj_kernel(*refs, n_x):
    x_refs = refs[:n_x]
    sh_ref, sc_ref, w_ref, o_ref, u_ref, wb_ref = refs[n_x:]
    i = pl.program_id(0)
    j = pl.program_id(1)

    def modulate_from(x_ref):
        u_ref[...] = (x_ref[...] * (1.0 + sc_ref[0]) + sh_ref[0]).astype(BF16)

    @pl.when(j == 0)
    def _():
        if n_x == 1:
            modulate_from(x_refs[0])
        else:
            n_p = TOK_P // u_ref.shape[0]
            pl.when(i < n_p)(lambda: modulate_from(x_refs[0]))
            pl.when(i >= n_p)(lambda: modulate_from(x_refs[1]))

    @pl.when(i == 0)
    def _():
        wb_ref[j] = w_ref[...].astype(BF16)

    o_ref[...] = jnp.dot(u_ref[...], wb_ref[j], preferred_element_type=F32)


def _first_pass_block(i, j, nj):
    return jnp.where(i == 0, j, nj - 1)


def _proj(xs, mods, w, *, tn, name):
    tm = ROW_TILE
    n = w.shape[1]
    nj = n // tn
    return pl.pallas_call(
        functools.partial(_proj_kernel, n_x=len(xs)),
        grid=(N_TOK // tm, nj),
        in_specs=_token_row_specs(len(xs), tm, D_MODEL, buffered_once=True) + [
            _mod_spec(tm, 0), _mod_spec(tm, 1),
            pl.BlockSpec((D_MODEL, tn), lambda i, j: (0, _first_pass_block(i, j, nj)))],
        out_specs=pl.BlockSpec((tm, tn), lambda i, j: (i, j)),
        out_shape=jax.ShapeDtypeStruct((N_TOK, n), F32),
        scratch_shapes=[pltpu.VMEM((tm, D_MODEL), BF16),
                        pltpu.VMEM((nj, D_MODEL, tn), BF16)],
        compiler_params=_cparams(("arbitrary", "arbitrary")),
        name=name,
    )(*xs, mods, mods, w)


def _rms(x, g):
    return x * lax.rsqrt(jnp.mean(x * x, -1, keepdims=True) + QK_EPS) * g


def _dot_nt(a, b):
    return lax.dot_general(a, b, (((1,), (1,)), ((), ())), preferred_element_type=F32)


def _rope(x, cos, sin_signed):
    lane = lax.broadcasted_iota(jnp.int32, x.shape, 1)
    partner = jnp.where(lane % 2 == 0, pltpu.roll(x, HEAD_DIM - 1, 1), pltpu.roll(x, 1, 1))
    return x * cos + partner * sin_signed


def _attn_ctx_kernel(q_ref, k_ref, v_ref, qn_ref, kn_ref, o_ref, ko_ref, vo_ref):
    scale = HEAD_DIM ** -0.5
    kn = _rms(k_ref[...], kn_ref[...])
    v = v_ref[...]
    ko_ref[...] = kn
    vo_ref[...] = v
    kb = kn.astype(BF16)
    vb = v.astype(BF16)
    for g in range(Q_GROUP):
        cols = slice(g * HEAD_DIM, (g + 1) * HEAD_DIM)
        q = _rms(q_ref[:, cols], qn_ref[...]).astype(BF16)
        s = _dot_nt(q, kb) * scale
        p = jnp.exp(s - jnp.max(s, -1, keepdims=True))
        p = p / jnp.sum(p, -1, keepdims=True)
        o = jnp.dot(p.astype(BF16), vb, preferred_element_type=F32)
        o_ref[:, cols] = o.astype(o_ref.dtype)


def _attn_context(proj, q_norm, k_norm):
    qw = Q_GROUP * HEAD_DIM
    k_blk0 = Q_WIDTH // HEAD_DIM
    v_blk0 = (Q_WIDTH + KV_WIDTH) // HEAD_DIM
    vec = pl.BlockSpec((1, HEAD_DIM), lambda b, h: (0, 0))
    return pl.pallas_call(
        _attn_ctx_kernel,
        grid=(BATCH, N_KV_HEADS),
        in_specs=[pl.BlockSpec((SEQ, qw), lambda b, h: (b, h)),
                  pl.BlockSpec((SEQ, HEAD_DIM), lambda b, h: (b, k_blk0 + h)),
                  pl.BlockSpec((SEQ, HEAD_DIM), lambda b, h: (b, v_blk0 + h)),
                  vec, vec],
        out_specs=[pl.BlockSpec((SEQ, qw), lambda b, h: (b, h)),
                   pl.BlockSpec((SEQ, HEAD_DIM), lambda b, h: (b, h)),
                   pl.BlockSpec((SEQ, HEAD_DIM), lambda b, h: (b, h))],
        out_shape=[jax.ShapeDtypeStruct((N_TOK, Q_WIDTH), BF16),
                   jax.ShapeDtypeStruct((TOK_P, KV_WIDTH), F32),
                   jax.ShapeDtypeStruct((TOK_P, KV_WIDTH), F32)],
        compiler_params=_cparams(("arbitrary", "arbitrary")),
        name="attn_context",
    )(proj, proj, proj, q_norm.reshape(1, HEAD_DIM), k_norm.reshape(1, HEAD_DIM))


ATTN_Q_ROWS = 256


def _attn_lat_kernel(q_ref, k_ref, v_ref, ck_ref, cv_ref, qn_ref, kn_ref,
                     cq_ref, sq_ref, ck_tab_ref, sk_tab_ref, dst_ref, o_ref):
    del dst_ref
    scale = HEAD_DIM ** -0.5
    kb = _rope(_rms(k_ref[...], kn_ref[...]), ck_tab_ref[...], sk_tab_ref[...]).astype(BF16)
    vb = v_ref[...].astype(BF16)
    ckb = ck_ref[...].astype(BF16)
    cvb = cv_ref[...].astype(BF16)
    for g in range(Q_GROUP):
        cols = slice(g * HEAD_DIM, (g + 1) * HEAD_DIM)
        q = _rope(_rms(q_ref[:, cols], qn_ref[...]), cq_ref[...], sq_ref[...]).astype(BF16)
        s1 = _dot_nt(q, ckb) * scale
        s2 = _dot_nt(q, kb) * scale
        m = jnp.maximum(jnp.max(s1, -1, keepdims=True), jnp.max(s2, -1, keepdims=True))
        p1 = jnp.exp(s1 - m)
        p2 = jnp.exp(s2 - m)
        den = jnp.sum(p1, -1, keepdims=True) + jnp.sum(p2, -1, keepdims=True)
        o = (jnp.dot((p1 / den).astype(BF16), cvb, preferred_element_type=F32)
             + jnp.dot((p2 / den).astype(BF16), vb, preferred_element_type=F32))
        o_ref[:, cols] = o.astype(o_ref.dtype)


def _rope_tables():
    t = jnp.arange(DEC_SEQ)
    row = (t // GRID_W).astype(F32)
    col = (t % GRID_W).astype(F32)
    n_freq = HEAD_DIM // 4
    inv = 1.0 / (ROPE_THETA ** (jnp.arange(n_freq, dtype=F32) / n_freq))
    ang = jnp.concatenate([row[:, None] * inv, col[:, None] * inv], -1)
    cos = jnp.repeat(jnp.cos(ang), 2, axis=-1)
    sign = jnp.where(jnp.arange(HEAD_DIM) % 2 == 0, -1.0, 1.0).astype(F32)
    sin_signed = jnp.repeat(jnp.sin(ang), 2, axis=-1) * sign
    return cos, sin_signed


_IN_PLACE = pl.BlockSpec(memory_space=pl.ANY)


def _attn_latent(proj, cache_k, cache_v, q_norm, k_norm, dst):
    qw = Q_GROUP * HEAD_DIM
    nq = DEC_SEQ // ATTN_Q_ROWS
    q_row0 = TOK_P // ATTN_Q_ROWS
    kv_row0 = TOK_P // DEC_SEQ
    k_blk0 = Q_WIDTH // HEAD_DIM
    v_blk0 = (Q_WIDTH + KV_WIDTH) // HEAD_DIM
    cos, sin_signed = _rope_tables()
    vec = pl.BlockSpec((1, HEAD_DIM), lambda b, h, c: (0, 0))
    tab_q = pl.BlockSpec((ATTN_Q_ROWS, HEAD_DIM), lambda b, h, c: (c, 0))
    tab_k = pl.BlockSpec((DEC_SEQ, HEAD_DIM), lambda b, h, c: (0, 0))
    ctx = pl.BlockSpec((PAST_LEN, HEAD_DIM), lambda b, h, c: (b, h))
    return pl.pallas_call(
        _attn_lat_kernel,
        grid=(DEC_BATCH, N_KV_HEADS, nq),
        in_specs=[pl.BlockSpec((ATTN_Q_ROWS, qw), lambda b, h, c: (q_row0 + b * nq + c, h)),
                  pl.BlockSpec((DEC_SEQ, HEAD_DIM), lambda b, h, c: (kv_row0 + b, k_blk0 + h)),
                  pl.BlockSpec((DEC_SEQ, HEAD_DIM), lambda b, h, c: (kv_row0 + b, v_blk0 + h)),
                  ctx, ctx, vec, vec, tab_q, tab_q, tab_k, tab_k, _IN_PLACE],
        out_specs=pl.BlockSpec((ATTN_Q_ROWS, qw), lambda b, h, c: (q_row0 + b * nq + c, h)),
        out_shape=jax.ShapeDtypeStruct((N_TOK, Q_WIDTH), BF16),
        input_output_aliases={11: 0},
        compiler_params=_cparams(("arbitrary", "arbitrary", "arbitrary")),
        name="attn_latent",
    )(proj, proj, proj,
      cache_k.reshape(DEC_BATCH * PAST_LEN, KV_WIDTH), cache_v.reshape(DEC_BATCH * PAST_LEN, KV_WIDTH),
      q_norm.reshape(1, HEAD_DIM), k_norm.reshape(1, HEAD_DIM), cos, sin_signed, cos, sin_signed, dst)


LRU_CT = 256
LRU_RC = 128


def _softplus(x):
    return jnp.maximum(x, 0.0) + jnp.log1p(jnp.exp(-jnp.abs(x)))


def _gelu_tanh(x):
    return 0.5 * x * (1.0 + jnp.tanh(math.sqrt(2.0 / math.pi) * (x + 0.044715 * (x * x * x))))


def _lru_kernel(*refs, seq, has_dst):
    xb_ref, gb_ref, cw_ref, cb_ref, lam_ref, wr_ref, br_ref, wi_ref, bi_ref, h0_ref = refs[:10]
    y_ref, hfin_ref, xpad_ref, a_ref, b_ref, hf_ref, hb_ref = refs[10 + int(has_dst):]
    ct = LRU_CT
    zeros = jnp.zeros((SUBLANE, ct), F32)
    xpad_ref[0:SUBLANE, :] = zeros
    xpad_ref[seq + SUBLANE:seq + 2 * SUBLANE, :] = zeros
    xpad_ref[SUBLANE:seq + SUBLANE, :] = xb_ref[...]
    cw = cw_ref[...]
    sp = _softplus(-lam_ref[...])
    for r0 in range(0, seq, LRU_RC):
        xc = cb_ref[...] + cw[0:1, :] * xpad_ref[r0 + 7:r0 + 7 + LRU_RC, :]
        for w in range(1, 4):
            xc = xc + cw[w:w + 1, :] * xpad_ref[r0 + 7 + w:r0 + 7 + w + LRU_RC, :]
        for kb in range(ct // LRU_BLOCK):
            cols = slice(kb * LRU_BLOCK, (kb + 1) * LRU_BLOCK)
            xk = xc[:, cols]
            xkb = xk.astype(BF16)
            for d in range(2):
                r = jax.nn.sigmoid(jnp.dot(xkb, wr_ref[d, kb].astype(BF16), preferred_element_type=F32)
                                   + br_ref[d:d + 1, cols])
                i = jax.nn.sigmoid(jnp.dot(xkb, wi_ref[d, kb].astype(BF16), preferred_element_type=F32)
                                   + bi_ref[d:d + 1, cols])
                log_a = -LRU_C * r * sp[d:d + 1, cols]
                a = jnp.exp(log_a)
                a_ref[d, r0:r0 + LRU_RC, cols] = a
                b_ref[d, r0:r0 + LRU_RC, cols] = jnp.sqrt(-jnp.tanh(log_a) * (a * a + 1.0)) * (i * xk)

    nblk = seq // SUBLANE
    row = lax.broadcasted_iota(jnp.int32, (SUBLANE, ct), 0)

    def body(i, carry):
        hf, hb = carry
        rf = pl.multiple_of(i * SUBLANE, SUBLANE)
        a = a_ref[0, pl.ds(rf, SUBLANE), :]
        b = b_ref[0, pl.ds(rf, SUBLANE), :]
        for s in (1, 2, 4):
            m = row >= s
            a_s = jnp.where(m, pltpu.roll(a, s, 0), 1.0)
            b_s = jnp.where(m, pltpu.roll(b, s, 0), 0.0)
            b = a * b_s + b
            a = a * a_s
        hblk = a * hf + b
        hf_ref[pl.ds(rf, SUBLANE), :] = hblk
        hf = hblk[SUBLANE - 1:SUBLANE, :]

        rb = pl.multiple_of((nblk - 1 - i) * SUBLANE, SUBLANE)
        a = a_ref[1, pl.ds(rb, SUBLANE), :]
        b = b_ref[1, pl.ds(rb, SUBLANE), :]
        for s in (1, 2, 4):
            m = row < SUBLANE - s
            a_s = jnp.where(m, pltpu.roll(a, SUBLANE - s, 0), 1.0)
            b_s = jnp.where(m, pltpu.roll(b, SUBLANE - s, 0), 0.0)
            b = a * b_s + b
            a = a * a_s
        hblk = a * hb + b
        hb_ref[pl.ds(rb, SUBLANE), :] = hblk
        hb = hblk[0:1, :]
        return hf, hb

    hf, hb = lax.fori_loop(0, nblk, body, (h0_ref[0:1, :], h0_ref[1:2, :]))
    hfin_ref[0:1, :] = hf
    hfin_ref[1:2, :] = hb
    for r0 in range(0, seq, LRU_RC):
        rows = slice(r0, r0 + LRU_RC)
        y = (hf_ref[rows, :] + hb_ref[rows, :]) * _gelu_tanh(gb_ref[rows, :])
        y_ref[rows, :] = y.astype(y_ref.dtype)


def _lru(proj, h0, p, *, seq, batch, row0, name, dst=None):
    ct = LRU_CT
    nkb = ct // LRU_BLOCK
    xb_blk0 = (Q_WIDTH + 2 * KV_WIDTH) // ct
    gb_blk0 = (Q_WIDTH + 2 * KV_WIDTH + LRU_WIDTH) // ct
    r0 = row0 // seq
    vec2 = pl.BlockSpec((2, ct), lambda b, c: (0, c))
    wblk = pl.BlockSpec((2, nkb, LRU_BLOCK, LRU_BLOCK), lambda b, c: (0, c, 0, 0))
    extra_specs, extra_args, aliases = [], [], {}
    if dst is not None:
        extra_specs, extra_args, aliases = [_IN_PLACE], [dst], {10: 0}
    return pl.pallas_call(
        functools.partial(_lru_kernel, seq=seq, has_dst=dst is not None),
        grid=(batch, LRU_WIDTH // ct),
        in_specs=[pl.BlockSpec((seq, ct), lambda b, c: (r0 + b, xb_blk0 + c)),
                  pl.BlockSpec((seq, ct), lambda b, c: (r0 + b, gb_blk0 + c)),
                  pl.BlockSpec((4, ct), lambda b, c: (0, c)),
                  pl.BlockSpec((1, ct), lambda b, c: (0, c)),
                  vec2, wblk, vec2, wblk, vec2,
                  pl.BlockSpec((None, 2, ct), lambda b, c: (b, 0, c))] + extra_specs,
        out_specs=[pl.BlockSpec((seq, ct), lambda b, c: (r0 + b, c)),
                   pl.BlockSpec((None, 2, ct), lambda b, c: (b, 0, c))],
        out_shape=[jax.ShapeDtypeStruct((N_TOK, LRU_WIDTH), BF16),
                   jax.ShapeDtypeStruct((batch, 2, LRU_WIDTH), F32)],
        input_output_aliases=aliases,
        scratch_shapes=[pltpu.VMEM((seq + 2 * SUBLANE, ct), F32),
                        pltpu.VMEM((2, seq, ct), F32),
                        pltpu.VMEM((2, seq, ct), F32),
                        pltpu.VMEM((seq, ct), F32),
                        pltpu.VMEM((seq, ct), F32)],
        compiler_params=_cparams(("arbitrary", "arbitrary")),
        name=name,
    )(proj, proj, p['conv_w'], p['conv_b'].reshape(1, LRU_WIDTH), p['lam'],
      p['w_r'], p['b_r'], p['w_i'], p['b_i'], h0, *extra_args)


OUT_TN = 512
OUT_TM = 512


def _outproj_kernel(*refs, n_a, n_res, with_router):
    a_refs = refs[:n_a]
    w_refs = refs[n_a:2 * n_a]
    res_refs = refs[2 * n_a:2 * n_a + n_res]
    gate_ref, g_ref, b_ref = refs[2 * n_a + n_res:2 * n_a + n_res + 3]
    pos = 2 * n_a + n_res + 3
    if with_router:
        sh_ref, sc_ref, rt_ref = refs[pos:pos + 3]
        pos += 3
        o_ref, u_ref, lg_ref, acc_ref = refs[pos:pos + 4]
        pos += 4
    else:
        o_ref, acc_ref = refs[pos:pos + 2]
        pos += 2
    wb_refs = refs[pos:pos + n_a]
    i = pl.program_id(0)
    j = pl.program_id(1)
    nj = pl.num_programs(1)

    @pl.when(i == 0)
    def _():
        for k in range(n_a):
            wb_refs[k][j] = w_refs[k][...].astype(BF16)

    acc = jnp.dot(a_refs[0][...], wb_refs[0][j], preferred_element_type=F32)
    for k in range(1, n_a):
        acc = acc + jnp.dot(a_refs[k][...], wb_refs[k][j], preferred_element_type=F32)
    if n_res == 1:
        res = res_refs[0][...]
    else:
        n_p = TOK_P // acc_ref.shape[1]
        res = jnp.where(i < n_p, res_refs[0][...], res_refs[1][...])
    acc_ref[j] = DN_ALPHA * res + gate_ref[0] * acc

    @pl.when(j == nj - 1)
    def _():
        tm = acc_ref.shape[1]
        n_chunks = acc_ref.shape[0]
        rc = 128

        def chunk(c, _):
            r0 = pl.multiple_of(c * rc, rc)
            ys = [acc_ref[k, pl.ds(r0, rc), :] for k in range(n_chunks)]
            tot = ys[0].sum(-1, keepdims=True)
            for y in ys[1:]:
                tot = tot + y.sum(-1, keepdims=True)
            mu = tot / D_MODEL
            sq = None
            for y in ys:
                t = ((y - mu) * (y - mu)).sum(-1, keepdims=True)
                sq = t if sq is None else sq + t
            rstd = lax.rsqrt(sq / D_MODEL + LN_EPS)
            lg = None
            us = []
            for k, y in enumerate(ys):
                cols = slice(k * OUT_TN, (k + 1) * OUT_TN)
                xn = (y - mu) * rstd * g_ref[:, cols] + b_ref[:, cols]
                o_ref[pl.ds(r0, rc), cols] = xn
                if with_router:
                    u = xn * (1.0 + sc_ref[0][:, cols]) + sh_ref[0][:, cols]
                    us.append(u)
                    t = _dot3(u, rt_ref[cols, :])
                    lg = t if lg is None else lg + t
            if with_router:
                lg_ref[pl.ds(r0, rc), :] = lg
                half = n_chunks // 2
                for k in range(half):
                    cols = slice(k * OUT_TN, (k + 1) * OUT_TN)
                    u_ref[pl.ds(r0, rc), cols] = _pack_bf16_pair(us[k], us[k + half])
            return 0

        lax.fori_loop(0, tm // rc, chunk, 0)


def _outproj(a_list, w, res, mods, gate_chunk, g, b, *, name, router=None, router_chunks=None):
    tm = OUT_TM
    tn = OUT_TN
    nj = D_MODEL // tn
    n_a = len(a_list)
    in_specs = []
    k0 = 0
    w_specs = []
    wb_shapes = []
    for a in a_list:
        ka = a.shape[1]
        in_specs.append(pl.BlockSpec((tm, ka), lambda i, j: (i, 0)))
        w_specs.append(pl.BlockSpec((ka, tn),
                                    lambda i, j, blk=k0 // ka: (blk, _first_pass_block(i, j, nj))))
        wb_shapes.append(pltpu.VMEM((nj, ka, tn), BF16))
        k0 += ka
    in_specs += w_specs
    in_specs += _token_row_specs(len(res), tm, tn, col=lambda i, j: j)
    in_specs += [_mod_spec(tm, gate_chunk, tn, col_of=lambda j: j),
                 pl.BlockSpec((1, D_MODEL), lambda i, j: (0, 0)),
                 pl.BlockSpec((1, D_MODEL), lambda i, j: (0, 0))]
    args = list(a_list) + [w] * n_a + list(res) + [mods, g.reshape(1, D_MODEL), b.reshape(1, D_MODEL)]
    out_specs = [pl.BlockSpec((tm, D_MODEL), lambda i, j: (i, 0))]
    out_shape = [jax.ShapeDtypeStruct((N_TOK, D_MODEL), F32)]
    if router is not None:
        in_specs += [_mod_spec(tm, router_chunks[0]), _mod_spec(tm, router_chunks[1]),
                     pl.BlockSpec((D_MODEL, LANE), lambda i, j: (0, 0))]
        args += [mods, mods, router]
        out_specs += [pl.BlockSpec((tm, D_MODEL // 2), lambda i, j: (i, 0)),
                      pl.BlockSpec((tm, LANE), lambda i, j: (i, 0))]
        out_shape += [jax.ShapeDtypeStruct((N_TOK, D_MODEL // 2), jnp.uint32),
                      jax.ShapeDtypeStruct((N_TOK, LANE), F32)]
    return pl.pallas_call(
        functools.partial(_outproj_kernel, n_a=n_a, n_res=len(res), with_router=router is not None),
        grid=(N_TOK // tm, nj),
        in_specs=in_specs,
        out_specs=out_specs,
        out_shape=out_shape,
        scratch_shapes=[pltpu.VMEM((nj, tm, tn), F32)] + wb_shapes,
        compiler_params=_cparams(("arbitrary", "arbitrary")),
        name=name,
    )(*args)


FFN_TF = 512


def _ffn_kernel(x_ref, sh_ref, sc_ref, gate_ref, g_ref, b_ref, w1_ref, w3_ref, w2_ref, o_ref, u_ref):
    j = pl.program_id(1)
    nj = pl.num_programs(1)

    @pl.when(j == 0)
    def _():
        u_ref[...] = (x_ref[...] * (1.0 + sc_ref[0]) + sh_ref[0]).astype(BF16)
        o_ref[...] = jnp.zeros(o_ref.shape, F32)

    u = u_ref[...]
    h1 = jnp.dot(u, w1_ref[...], preferred_element_type=F32)
    h3 = jnp.dot(u, w3_ref[...], preferred_element_type=F32)
    h = (_silu(h1) * h3).astype(BF16)
    o_ref[...] += jnp.dot(h, w2_ref[...], preferred_element_type=F32)

    @pl.when(j == nj - 1)
    def _():
        rc = 128

        def chunk(c, _):
            r0 = pl.multiple_of(c * rc, rc)
            y = DN_ALPHA * x_ref[pl.ds(r0, rc), :] + gate_ref[0] * o_ref[pl.ds(r0, rc), :]
            o_ref[pl.ds(r0, rc), :] = _layer_norm_rows(y, g_ref[...], b_ref[...])
            return 0

        lax.fori_loop(0, o_ref.shape[0] // rc, chunk, 0)


def _ffn(x, mods, p):
    tm = ROW_TILE
    tf = FFN_TF
    vec = pl.BlockSpec((1, D_MODEL), lambda i, j: (0, 0))
    return pl.pallas_call(
        _ffn_kernel,
        grid=(N_TOK // tm, D_FF // tf),
        in_specs=[pl.BlockSpec((tm, D_MODEL), lambda i, j: (i, 0), pipeline_mode=pl.Buffered(1)),
                  _mod_spec(tm, 3), _mod_spec(tm, 4), _mod_spec(tm, 5), vec, vec,
                  pl.BlockSpec((D_MODEL, tf), lambda i, j: (0, j)),
                  pl.BlockSpec((D_MODEL, tf), lambda i, j: (0, j)),
                  pl.BlockSpec((tf, D_MODEL), lambda i, j: (j, 0))],
        out_specs=pl.BlockSpec((tm, D_MODEL), lambda i, j: (i, 0)),
        out_shape=jax.ShapeDtypeStruct((N_TOK, D_MODEL), F32),
        scratch_shapes=[pltpu.VMEM((tm, D_MODEL), BF16)],
        compiler_params=_cparams(("arbitrary", "arbitrary")),
        name="ffn_swiglu",
    )(x, mods, mods, mods, p['ln2_g'].reshape(1, D_MODEL), p['ln2_b'].reshape(1, D_MODEL),
      _cast_bf16(p['ffn_w1'], 256, "ffn_w1_bf16"), _cast_bf16(p['ffn_w3'], 256, "ffn_w3_bf16"),
      _cast_bf16(p['ffn_w2'], 512, "ffn_w2_bf16"))


def _filter_mlp_kernel(feat_ref, w1_ref, b1_ref, f1_ref, w2_ref, b2_ref, f2_ref, o_ref):
    h = jnp.sin(f1_ref[...] * (_dot3(feat_ref[...], w1_ref[...]) + b1_ref[...]))
    o_ref[...] = jnp.sin(f2_ref[...] * (_dot3(h, w2_ref[...]) + b2_ref[...]))


def _filter_features(seq):
    t = jnp.arange(seq, dtype=F32)
    t01 = t / (seq - 1)
    w = 2.0 * math.pi * t / seq
    f = jnp.linspace(1e-4, HYENA_BANDS - 1, HYENA_BANDS, dtype=F32)
    fw = w[:, None] * f[None, :]
    feat = jnp.concatenate([t01[:, None], jnp.cos(fw), -jnp.sin(fw)], -1)
    return jnp.pad(feat, ((0, 0), (0, LANE - HYENA_EMB))), t01[:, None]


def _filter_mlp(seq, p):
    feat, t01 = _filter_features(seq)
    hid = HYENA_FILTER_HIDDEN
    w1 = jnp.pad(p['filt_w1'], ((0, LANE - HYENA_EMB), (0, 0)))
    row = lambda v: v.reshape(1, hid)
    h2 = pl.pallas_call(
        _filter_mlp_kernel,
        out_shape=jax.ShapeDtypeStruct((seq, hid), F32),
        compiler_params=pltpu.CompilerParams(vmem_limit_bytes=VMEM_LIMIT),
        name=f"hyena_filter_mlp_{seq}",
    )(feat, w1, row(p['filt_b1']), row(p['filt_f1']), p['filt_w2'], row(p['filt_b2']), row(p['filt_f2']))
    return h2, t01


def _dft_matrices(seq):
    n = 2 * seq
    f = jnp.arange(seq, dtype=jnp.int32)
    k = (f[:, None] * f[None, :]) % n
    ang = k.astype(F32) * (math.pi / seq)
    c = jnp.cos(ang)
    s = -jnp.sin(ang)
    alt = jnp.where(f % 2 == 0, 1.0, -1.0).astype(F32)
    s_fwd = s.at[0, :].set(alt)
    s_inv = s.at[:, 0].set(alt)
    fwd = jnp.concatenate([c, s_fwd], axis=0).astype(BF16)
    inv = jnp.concatenate([c, s_inv], axis=1).astype(BF16)
    return fwd, inv, alt[:, None]


HY_RC = 128


def _hyena_kernel(*refs, seq, nb, ct, has_dst):
    (x1_ref, x2_ref, z_ref, sw1_ref, sw2_ref, swz_ref, sb1_ref, sb2_ref, sbz_ref,
     h2_ref, t01_ref, alt_ref, w3_ref, dec_ref, bias_ref, fwd_ref, inv_ref) = refs[:17]
    (o_ref, kr_ref, ki_ref, zb_ref, zf_ref, yb_ref, pad_ref, g1_ref, g2_ref,
     zc_ref) = refs[17 + int(has_dst):]
    nrc = seq // HY_RC

    @pl.when(pl.program_id(1) == 0)
    def _():
        t01 = t01_ref[...]
        h2 = h2_ref[...]
        row0 = lax.broadcasted_iota(jnp.int32, (seq, ct), 0) == 0
        wf = jnp.where(row0, 1.0, 2.0) / (2.0 * seq)
        for n in range(HYENA_ORDER):
            kpos = _dot3(h2, w3_ref[n]) * jnp.exp(-t01 * jnp.abs(dec_ref[n:n + 1, :]))
            kneg = _dot3(h2, w3_ref[HYENA_ORDER + n]) * jnp.exp(
                -t01 * jnp.abs(dec_ref[HYENA_ORDER + n:HYENA_ORDER + n + 1, :]))
            kneg = jnp.where(row0, 0.0, kneg)
            ksum = kpos + kneg
            kdif = kpos - kneg
            kr = jnp.dot(fwd_ref[0:seq, :], ksum.astype(BF16), preferred_element_type=F32)
            ki = jnp.dot(fwd_ref[seq:2 * seq, :], kdif.astype(BF16), preferred_element_type=F32)
            nyq = jnp.sum(alt_ref[...] * ksum, axis=0, keepdims=True) / (2.0 * seq)
            kr_ref[n] = kr * wf
            ki_ref[n] = jnp.where(row0, nyq, ki * wf)

    zeros = jnp.zeros((SUBLANE, ct), F32)
    pad_ref[0:SUBLANE, :] = zeros
    pad_ref[seq + SUBLANE:seq + 2 * SUBLANE, :] = zeros

    def short_conv(src_ref, rows0, w_ref, b_ref, dst_ref):
        pad_ref[SUBLANE:seq + SUBLANE, :] = src_ref[rows0:rows0 + seq, :]
        w = w_ref[...]
        for c in range(nrc):
            r0 = c * HY_RC
            acc = b_ref[...] + w[0:1, :] * pad_ref[r0 + 7:r0 + 7 + HY_RC, :]
            acc = acc + w[1:2, :] * pad_ref[r0 + 8:r0 + 8 + HY_RC, :]
            acc = acc + w[2:3, :] * pad_ref[r0 + 9:r0 + 9 + HY_RC, :]
            dst_ref[r0:r0 + HY_RC, :] = acc

    for bi in range(nb):
        rows0 = bi * seq
        short_conv(x1_ref, rows0, sw1_ref, sb1_ref, g1_ref)
        short_conv(x2_ref, rows0, sw2_ref, sb2_ref, g2_ref)
        short_conv(z_ref, rows0, swz_ref, sbz_ref, zc_ref)
        for n, gate_ref in enumerate((g1_ref, g2_ref)):
            for c in range(nrc):
                rows = slice(c * HY_RC, (c + 1) * HY_RC)
                zb_ref[rows, :] = zc_ref[rows, :].astype(BF16)
            zf_ref[...] = jnp.dot(fwd_ref[...], zb_ref[...], preferred_element_type=F32)
            for c in range(nrc):
                rows = slice(c * HY_RC, (c + 1) * HY_RC)
                rows_i = slice(seq + c * HY_RC, seq + (c + 1) * HY_RC)
                zr = zf_ref[rows, :]
                zi = zf_ref[rows_i, :]
                kr = kr_ref[n, rows, :]
                ki = ki_ref[n, rows, :]
                yr = zr * kr - zi * ki
                yi = zr * ki + zi * kr
                if c == 0:
                    first = lax.broadcasted_iota(jnp.int32, (HY_RC, ct), 0) == 0
                    yr = jnp.where(first, zr * kr, yr)
                    yi = jnp.where(first, zi * ki, yi)
                yb_ref[rows, :] = yr.astype(BF16)
                yb_ref[rows_i, :] = yi.astype(BF16)
            zf_ref[0:seq, :] = jnp.dot(inv_ref[...], yb_ref[...], preferred_element_type=F32)
            for c in range(nrc):
                rows = slice(c * HY_RC, (c + 1) * HY_RC)
                zc = zc_ref[rows, :]
                znew = gate_ref[rows, :] * (zf_ref[rows, :] + zc * bias_ref[n:n + 1, :])
                if n == HYENA_ORDER - 1:
                    o_ref[rows0 + c * HY_RC:rows0 + (c + 1) * HY_RC, :] = znew.astype(o_ref.dtype)
                else:
                    zc_ref[rows, :] = znew


def _hyena(proj, p, *, seq, batch, row0, nb, ct, name, dst=None):
    nct = D_MODEL // ct
    extra_specs, extra_args, aliases = [], [], {}
    if dst is not None:
        extra_specs, extra_args, aliases = [_IN_PLACE], [dst], {17: 0}
    h2, t01 = _filter_mlp(seq, p)
    fwd, inv, alt = _dft_matrices(seq)
    rb0 = row0 // (nb * seq)
    w3 = p['filt_w3'].reshape(HYENA_FILTER_HIDDEN, 2 * HYENA_ORDER, D_MODEL).transpose(1, 0, 2)
    dec = p['filt_decay'].reshape(2 * HYENA_ORDER, D_MODEL)
    sw = p['short_w']
    sb = p['short_b'].reshape(1, 3 * D_MODEL)
    slab = lambda k: pl.BlockSpec((nb * seq, ct), lambda c, b: (rb0 + b, k * nct + c))
    swk = lambda k: pl.BlockSpec((3, ct), lambda c, b: (0, k * nct + c))
    sbk = lambda k: pl.BlockSpec((1, ct), lambda c, b: (0, k * nct + c))
    const = lambda shape: pl.BlockSpec(shape, lambda c, b: tuple(0 for _ in shape))
    return pl.pallas_call(
        functools.partial(_hyena_kernel, seq=seq, nb=nb, ct=ct, has_dst=dst is not None),
        grid=(nct, batch // nb),
        in_specs=[slab(0), slab(1), slab(2), swk(0), swk(1), swk(2), sbk(0), sbk(1), sbk(2),
                  const((seq, HYENA_FILTER_HIDDEN)), const((seq, 1)), const((seq, 1)),
                  pl.BlockSpec((2 * HYENA_ORDER, HYENA_FILTER_HIDDEN, ct), lambda c, b: (0, 0, c)),
                  pl.BlockSpec((2 * HYENA_ORDER, ct), lambda c, b: (0, c)),
                  pl.BlockSpec((HYENA_ORDER, ct), lambda c, b: (0, c)),
                  pl.BlockSpec((2 * seq, seq), lambda c, b: (0, 0), pipeline_mode=pl.Buffered(1)),
                  pl.BlockSpec((seq, 2 * seq), lambda c, b: (0, 0), pipeline_mode=pl.Buffered(1))]
        + extra_specs,
        out_specs=pl.BlockSpec((nb * seq, ct), lambda c, b: (rb0 + b, c)),
        out_shape=jax.ShapeDtypeStruct((N_TOK, D_MODEL), BF16),
        input_output_aliases=aliases,
        scratch_shapes=[pltpu.VMEM((HYENA_ORDER, seq, ct), F32),
                        pltpu.VMEM((HYENA_ORDER, seq, ct), F32),
                        pltpu.VMEM((seq, ct), BF16),
                        pltpu.VMEM((2 * seq, ct), F32),
                        pltpu.VMEM((2 * seq, ct), BF16),
                        pltpu.VMEM((seq + 2 * SUBLANE, ct), F32),
                        pltpu.VMEM((seq, ct), F32),
                        pltpu.VMEM((seq, ct), F32),
                        pltpu.VMEM((seq, ct), F32)],
        compiler_params=_cparams(("arbitrary", "arbitrary")),
        name=name,
    )(proj, proj, proj, sw, sw, sw, sb, sb, sb, h2, t01, alt, w3, dec, p['filt_bias'], fwd, inv,
      *extra_args)


def _moe_kernel(ie_ref, ib_ref, ir_ref, x_ref, w1_ref, w3_ref, w2_ref, o_hbm,
                acc_ref, w1b_ref, w3b_ref, w2b_ref, sem):
    w = pl.program_id(0)
    j = pl.program_id(1)
    nj = pl.num_programs(1)
    rows = ir_ref[w]
    ntiles = (rows + (MOE_ROW_TILE - 1)) // MOE_ROW_TILE
    rt = MOE_ROW_TILE

    def partial_out(r):
        off = pl.multiple_of(r * rt, rt)
        xt = _unpack_bf16_pair(x_ref[pl.ds(off, rt), :])
        h1 = jnp.dot(xt, w1b_ref[...], preferred_element_type=F32)
        h3 = jnp.dot(xt, w3b_ref[...], preferred_element_type=F32)
        h = (_silu(h1) * h3).astype(BF16)
        return off, jnp.dot(h, w2b_ref[...], preferred_element_type=F32)

    @pl.when(rows > 0)
    def _():
        w1b_ref[...] = w1_ref[...].astype(BF16)
        w3b_ref[...] = w3_ref[...].astype(BF16)
        w2b_ref[...] = w2_ref[...].astype(BF16)

        def run_tiles(first):
            def one(r):
                off, part = partial_out(r)
                if first:
                    acc_ref[pl.ds(off, rt), :] = part
                else:
                    acc_ref[pl.ds(off, rt), :] += part

            def pair(q, _):
                one(2 * q)
                one(2 * q + 1)
                return 0

            lax.fori_loop(0, ntiles // 2, pair, 0)

            @pl.when(ntiles % 2 == 1)
            def _():
                one(ntiles - 1)

        @pl.when(j == 0)
        def _():
            run_tiles(True)

        @pl.when(j > 0)
        def _():
            run_tiles(False)

        @pl.when(j == nj - 1)
        def _():
            base = ib_ref[w] * MOE_CHUNK

            def tile_copy(off):
                return pltpu.make_async_copy(acc_ref.at[pl.ds(off, rt)],
                                             o_hbm.at[pl.ds(base + off, rt)], sem)

            def start(r, _):
                tile_copy(pl.multiple_of(r * rt, rt)).start()
                return 0

            def wait(r, _):
                tile_copy(pl.multiple_of(r * rt, rt)).wait()
                return 0

            lax.fori_loop(0, ntiles, start, 0)
            lax.fori_loop(0, ntiles, wait, 0)


def _moe_experts(xs, item_expert, item_block, item_rows, p):
    tj = MOE_FF_TILE
    nj = D_FF_EXPERT // tj

    def jeff(w, j, ir):
        return jnp.where(ir[w] > 0, j, nj - 1)

    grid_spec = pltpu.PrefetchScalarGridSpec(
        num_scalar_prefetch=3,
        grid=(MOE_ITEMS, nj),
        in_specs=[pl.BlockSpec((MOE_CHUNK, D_MODEL // 2), lambda w, j, ie, ib, ir: (ib[w], 0),
                               pipeline_mode=pl.Buffered(1)),
                  pl.BlockSpec((None, D_MODEL, tj), lambda w, j, ie, ib, ir: (ie[w], 0, jeff(w, j, ir))),
                  pl.BlockSpec((None, D_MODEL, tj), lambda w, j, ie, ib, ir: (ie[w], 0, jeff(w, j, ir))),
                  pl.BlockSpec((None, tj, D_MODEL), lambda w, j, ie, ib, ir: (ie[w], jeff(w, j, ir), 0))],
        out_specs=pl.BlockSpec(memory_space=pl.ANY),
        scratch_shapes=[pltpu.VMEM((MOE_CHUNK, D_MODEL), F32),
                        pltpu.VMEM((D_MODEL, tj), BF16),
                        pltpu.VMEM((D_MODEL, tj), BF16),
                        pltpu.VMEM((tj, D_MODEL), BF16),
                        pltpu.SemaphoreType.DMA(())],
    )
    return pl.pallas_call(
        _moe_kernel,
        grid_spec=grid_spec,
        out_shape=jax.ShapeDtypeStruct((MOE_ROWS, D_MODEL), F32),
        compiler_params=_cparams(("arbitrary", "arbitrary")),
        name="moe_experts",
    )(item_expert, item_block, item_rows, xs, p['exp_w1'], p['exp_w3'], p['exp_w2'])


ROUTE_TM = 256
R_E0, R_E1, R_P0, R_P1, R_RANK0, R_RANK1 = range(6)


def _route_kernel(lg_ref, o_ref, cnt_ref, carry_ref):
    i = pl.program_id(0)
    tm = ROUTE_TM

    @pl.when(i == 0)
    def _():
        carry_ref[...] = jnp.zeros(carry_ref.shape, F32)

    lane = lax.broadcasted_iota(jnp.int32, (tm, LANE), 1)
    lg = jnp.where(lane < N_EXPERTS, lg_ref[...], -jnp.inf)
    m0 = jnp.max(lg, -1, keepdims=True)
    e0 = jnp.min(jnp.where(lg == m0, lane, LANE), -1, keepdims=True)
    lg1 = jnp.where(lane == e0, -jnp.inf, lg)
    m1 = jnp.max(lg1, -1, keepdims=True)
    e1 = jnp.min(jnp.where(lg1 == m1, lane, LANE), -1, keepdims=True)
    t = jnp.exp(m1 - m0)
    p0 = 1.0 / (1.0 + t)
    p1 = t / (1.0 + t)
    hit = ((lane == e0) | (lane == e1)).astype(BF16)
    r_i = lax.broadcasted_iota(jnp.int32, (tm, tm), 0)
    c_i = lax.broadcasted_iota(jnp.int32, (tm, tm), 1)
    before = (c_i < r_i).astype(BF16)
    pref = jnp.dot(before, hit, preferred_element_type=F32) + carry_ref[0:1, :]
    rank0 = jnp.sum(jnp.where(lane == e0, pref, 0.0), -1, keepdims=True)
    rank1 = jnp.sum(jnp.where(lane == e1, pref, 0.0), -1, keepdims=True)
    carry_ref[0:1, :] = carry_ref[0:1, :] + jnp.sum(hit.astype(F32), axis=0, keepdims=True)
    rec = jnp.zeros((tm, LANE), F32)
    for k, v in ((R_E0, e0.astype(F32)), (R_E1, e1.astype(F32)), (R_P0, p0), (R_P1, p1),
                 (R_RANK0, rank0), (R_RANK1, rank1)):
        rec = jnp.where(lane == k, v, rec)
    o_ref[...] = rec
    cnt_ref[...] = carry_ref[...]


def _route_records(logits):
    return pl.pallas_call(
        _route_kernel,
        grid=(N_TOK // ROUTE_TM,),
        in_specs=[pl.BlockSpec((ROUTE_TM, LANE), lambda i: (i, 0))],
        out_specs=[pl.BlockSpec((ROUTE_TM, LANE), lambda i: (i, 0)),
                   pl.BlockSpec((SUBLANE, LANE), lambda i: (0, 0))],
        out_shape=[jax.ShapeDtypeStruct((N_TOK, LANE), F32),
                   jax.ShapeDtypeStruct((SUBLANE, LANE), F32)],
        scratch_shapes=[pltpu.VMEM((SUBLANE, LANE), F32)],
        compiler_params=_cparams(("arbitrary",)),
        name="moe_route",
    )(logits)


def _route(logits):
    rec, cnt = _route_records(logits)
    e_flat = rec[:, R_E0:R_E1 + 1].astype(jnp.int32).reshape(-1)
    rank = rec[:, R_RANK0:R_RANK1 + 1].astype(jnp.int32).reshape(-1)
    counts = cnt[0, :N_EXPERTS].astype(jnp.int32)
    blocks = (counts + MOE_CHUNK - 1) // MOE_CHUNK
    bend = jnp.cumsum(blocks)
    bstart = bend - blocks
    total = bend[-1]
    pos = bstart[e_flat] * MOE_CHUNK + rank
    w = jnp.arange(MOE_ITEMS, dtype=jnp.int32)
    w_eff = jnp.minimum(w, total - 1)
    item_expert = jnp.minimum(jnp.sum((w_eff[:, None] >= bend[None, :]).astype(jnp.int32), axis=1),
                              N_EXPERTS - 1)
    item_rows = jnp.where(w < total,
                          jnp.clip(counts[item_expert] - (w - bstart[item_expert]) * MOE_CHUNK, 0, MOE_CHUNK),
                          0)
    src = jnp.zeros((MOE_ROWS,), jnp.int32).at[pos].set(
        jnp.arange(N_TOK * TOP_K, dtype=jnp.int32) // TOP_K)
    return rec, pos.reshape(N_TOK, TOP_K), src, item_expert.astype(jnp.int32), \
        w_eff.astype(jnp.int32), item_rows.astype(jnp.int32)


def _postnorm_kernel(x_ref, y0_ref, y1_ref, rec_ref, gate_ref, g_ref, b_ref, op_ref, os_ref):
    i = pl.program_id(0)
    n_p = TOK_P // x_ref.shape[0]
    delta = (rec_ref[:, R_P0:R_P0 + 1] * y0_ref[...] + rec_ref[:, R_P1:R_P1 + 1] * y1_ref[...])
    y = DN_ALPHA * x_ref[...] + gate_ref[0] * delta
    out = _layer_norm_rows(y, g_ref[...], b_ref[...])

    @pl.when(i < n_p)
    def _():
        op_ref[...] = out

    @pl.when(i >= n_p)
    def _():
        os_ref[...] = out


def _postnorm(x, y0, y1, rec, mods, gate_chunk, g, b):
    tm = 256
    vec = pl.BlockSpec((1, D_MODEL), lambda i: (0, 0))
    rows = pl.BlockSpec((tm, D_MODEL), lambda i: (i, 0))
    return pl.pallas_call(
        _postnorm_kernel,
        grid=(N_TOK // tm,),
        in_specs=[rows, rows, rows, pl.BlockSpec((tm, LANE), lambda i: (i, 0)),
                  _mod_spec(tm, gate_chunk), vec, vec],
        out_specs=_token_row_specs(2, tm, D_MODEL),
        out_shape=[jax.ShapeDtypeStruct((TOK_P, D_MODEL), F32),
                   jax.ShapeDtypeStruct((TOK_S, D_MODEL), F32)],
        compiler_params=_cparams(("arbitrary",)),
        name="final_postnorm",
    )(x, y0, y1, rec, mods, g.reshape(1, D_MODEL), b.reshape(1, D_MODEL))


def kernel(x_prompt, x_sample, c, c_ctx, cache_l0_k, cache_l0_v, state_l0_lru, l0_ada_w, l0_ada_b, l0_w_in, l0_q_norm, l0_k_norm, l0_lru_conv_w, l0_lru_conv_b, l0_lru_lambda, l0_lru_w_r, l0_lru_b_r, l0_lru_w_i, l0_lru_b_i, l0_w_out, l0_ln1_g, l0_ln1_b, l0_ffn_w1, l0_ffn_w3, l0_ffn_w2, l0_ln2_g, l0_ln2_b, l1_ada_w, l1_ada_b, l1_w_in, l1_short_w, l1_short_b, l1_filt_w1, l1_filt_b1, l1_filt_f1, l1_filt_w2, l1_filt_b2, l1_filt_f2, l1_filt_w3, l1_filt_decay, l1_filt_bias, l1_w_out, l1_ln1_g, l1_ln1_b, l1_router, l1_exp_w1, l1_exp_w3, l1_exp_w2, l1_ln2_g, l1_ln2_b):
    x_in = [x_prompt.reshape(TOK_P, D_MODEL), x_sample.reshape(TOK_S, D_MODEL)]
    cond = jnp.concatenate([c_ctx[None, :], c, jnp.zeros((N_COND - 1 - DEC_BATCH, D_MODEL), F32)], axis=0)
    mods0 = _ada(cond, l0_ada_w, l0_ada_b)
    mods1 = _ada(cond, l1_ada_w, l1_ada_b)

    lru_p = dict(conv_w=l0_lru_conv_w, conv_b=l0_lru_conv_b, lam=l0_lru_lambda,
                 w_r=l0_lru_w_r, b_r=l0_lru_b_r, w_i=l0_lru_w_i, b_i=l0_lru_b_i)
    proj0 = _proj(x_in, mods0, l0_w_in, tn=512, name="l0_in_proj")
    attn, new_k, new_v = _attn_context(proj0, l0_q_norm, l0_k_norm)
    attn = _attn_latent(proj0, cache_l0_k, cache_l0_v, l0_q_norm, l0_k_norm, attn)
    lru, new_h = _lru(proj0, jnp.zeros((BATCH, 2, LRU_WIDTH), F32), lru_p,
                      seq=SEQ, batch=BATCH, row0=0, name="lru_context")
    lru, _ = _lru(proj0, state_l0_lru, lru_p,
                  seq=DEC_SEQ, batch=DEC_BATCH, row0=TOK_P, name="lru_latent", dst=lru)
    x = _outproj([attn, lru], l0_w_out, x_in, mods0, 2, l0_ln1_g, l0_ln1_b, name="l0_out_proj")[0]
    x = _ffn(x, mods0, dict(ffn_w1=l0_ffn_w1, ffn_w3=l0_ffn_w3, ffn_w2=l0_ffn_w2,
                            ln2_g=l0_ln2_g, ln2_b=l0_ln2_b))

    hy_p = dict(short_w=l1_short_w, short_b=l1_short_b, filt_w1=l1_filt_w1, filt_b1=l1_filt_b1,
                filt_f1=l1_filt_f1, filt_w2=l1_filt_w2, filt_b2=l1_filt_b2, filt_f2=l1_filt_f2,
                filt_w3=l1_filt_w3, filt_decay=l1_filt_decay, filt_bias=l1_filt_bias)
    proj1 = _proj([x], mods1, l1_w_in, tn=512, name="l1_in_proj")
    z = _hyena(proj1, hy_p, seq=SEQ, batch=BATCH, row0=0, nb=4, ct=512, name="hyena_context")
    z = _hyena(proj1, hy_p, seq=DEC_SEQ, batch=DEC_BATCH, row0=TOK_P, nb=1, ct=256, name="hyena_latent",
               dst=z)
    router = jnp.pad(l1_router, ((0, 0), (0, LANE - N_EXPERTS)))
    x, u, logits = _outproj([z], l1_w_out, [x], mods1, 2, l1_ln1_g, l1_ln1_b, name="l1_out_proj",
                            router=router, router_chunks=(3, 4))
    rec, pos, src, item_expert, item_block, item_rows = _route(logits)
    xs = _rows(u, src)
    ys = _moe_experts(xs, item_expert, item_block, item_rows,
                      dict(exp_w1=l1_exp_w1, exp_w3=l1_exp_w3, exp_w2=l1_exp_w2))
    y_prompt, y_sample = _postnorm(x, _rows(ys, pos[:, 0]), _rows(ys, pos[:, 1]), rec,
                                   mods1, 5, l1_ln2_g, l1_ln2_b)
    return (y_prompt.reshape(BATCH, SEQ, D_MODEL), y_sample.reshape(DEC_BATCH, DEC_SEQ, D_MODEL),
            new_k.reshape(BATCH, SEQ, N_KV_HEADS, HEAD_DIM),
            new_v.reshape(BATCH, SEQ, N_KV_HEADS, HEAD_DIM),
            new_h)
```

```python
import functools
import math

import jax
import jax.numpy as jnp
from jax import lax
from jax.experimental import pallas as pl
from jax.experimental.pallas import tpu as pltpu

F32 = jnp.float32
BF16 = jnp.bfloat16

D_MODEL = 2048
BATCH = 16
SEQ = 256
DEC_BATCH = 2
DEC_SEQ = 1024
PAST_LEN = 256
GRID_W = 64
HEAD_DIM = 128
N_Q_HEADS = 8
N_KV_HEADS = 2
Q_GROUP = N_Q_HEADS // N_KV_HEADS
Q_WIDTH = N_Q_HEADS * HEAD_DIM
KV_WIDTH = N_KV_HEADS * HEAD_DIM
ROPE_THETA = 10000.0
LRU_WIDTH = D_MODEL // 2
LRU_BLOCK = 128
LRU_C = 8.0
MIX0_IN = Q_WIDTH + 2 * KV_WIDTH + 2 * LRU_WIDTH
HYENA_ORDER = 2
HYENA_EMB = 33
HYENA_BANDS = (HYENA_EMB - 1) // 2
HYENA_FILTER_HIDDEN = 64
D_FF = 5632
N_EXPERTS = 8
TOP_K = 2
D_FF_EXPERT = 7168
N_MOD = 6
LN_EPS = 1e-5
QK_EPS = 1e-6
DEPTH = 2
DN_ALPHA = (2 * DEPTH) ** 0.25

TOK_P = BATCH * SEQ
TOK_S = DEC_BATCH * DEC_SEQ
N_TOK = TOK_P + TOK_S
N_COND = 8
LANE = 128
SUBLANE = 8
VMEM_LIMIT = 56 * 1024 * 1024

ROW_TILE = 1024
MOE_CHUNK = 2048
MOE_ROW_TILE = 256
MOE_FF_TILE = 256
MOE_ITEMS = N_EXPERTS + (N_TOK * TOP_K) // MOE_CHUNK
MOE_ROWS = MOE_ITEMS * MOE_CHUNK


def _cparams(sem):
    return pltpu.CompilerParams(dimension_semantics=sem, vmem_limit_bytes=VMEM_LIMIT)


def _cond_of_tile(i, tm):
    return jnp.maximum(i * tm // DEC_SEQ - (TOK_P // DEC_SEQ - 1), 0)


def _mod_spec(tm, chunk, width=D_MODEL, col_of=None):
    per = D_MODEL // width
    if col_of is None:
        return pl.BlockSpec((1, 1, width), lambda i, *_: (_cond_of_tile(i, tm), 0, chunk * per))
    return pl.BlockSpec((1, 1, width),
                        lambda i, j, *_: (_cond_of_tile(i, tm), 0, chunk * per + col_of(j)))


def _silu(x):
    return x * jax.nn.sigmoid(x)


def _split_bf16(x):
    hi = x.astype(BF16)
    lo = (x - hi.astype(F32)).astype(BF16)
    return hi, lo


def _pack_bf16_pair(lo, hi):
    lo_bits = lax.bitcast_convert_type(lo.astype(BF16).astype(F32), jnp.uint32) >> 16
    hi_bits = lax.bitcast_convert_type(hi.astype(BF16).astype(F32), jnp.uint32) & jnp.uint32(0xFFFF0000)
    return hi_bits | lo_bits


def _unpack_bf16_pair(words):
    lo = lax.bitcast_convert_type(words << 16, F32)
    hi = lax.bitcast_convert_type(words & jnp.uint32(0xFFFF0000), F32)
    return jnp.concatenate([lo, hi], axis=1).astype(BF16)


def _rows(a, idx):
    return a.at[idx].get(mode="promise_in_bounds")


def _dot3(a, b):
    ah, al = _split_bf16(a)
    bh, bl = _split_bf16(b)
    d = lambda x, y: jnp.dot(x, y, preferred_element_type=F32)
    return d(ah, bh) + (d(ah, bl) + d(al, bh))


def _layer_norm_rows(y, g, b):
    mu = jnp.mean(y, -1, keepdims=True)
    yc = y - mu
    var = jnp.mean(yc * yc, -1, keepdims=True)
    return yc * lax.rsqrt(var + LN_EPS) * g + b


def _ada_kernel(c_ref, w_ref, b_ref, o_ref):
    s = _silu(c_ref[...]).astype(BF16)
    o_ref[...] = jnp.dot(s, w_ref[...].astype(BF16), preferred_element_type=F32) + b_ref[...]


def _ada(cond, w, b):
    tn = 1024
    n = w.shape[1]
    out = pl.pallas_call(
        _ada_kernel,
        grid=(n // tn,),
        in_specs=[pl.BlockSpec((N_COND, D_MODEL), lambda j: (0, 0)),
                  pl.BlockSpec((D_MODEL, tn), lambda j: (0, j)),
                  pl.BlockSpec((1, tn), lambda j: (0, j))],
        out_specs=pl.BlockSpec((N_COND, tn), lambda j: (0, j)),
        out_shape=jax.ShapeDtypeStruct((N_COND, n), F32),
        compiler_params=_cparams(("arbitrary",)),
        name="ada_modulation",
    )(cond, w, b.reshape(1, n))
    return out.reshape(N_COND, 1, n)


def _cast_kernel(x_ref, o_ref):
    o_ref[...] = x_ref[...].astype(o_ref.dtype)


def _cast_bf16(w, rows, name):
    r, c = w.shape
    return pl.pallas_call(
        _cast_kernel,
        grid=(r // rows,),
        in_specs=[pl.BlockSpec((rows, c), lambda i: (i, 0))],
        out_specs=pl.BlockSpec((rows, c), lambda i: (i, 0)),
        out_shape=jax.ShapeDtypeStruct((r, c), BF16),
        compiler_params=_cparams(("arbitrary",)),
        name=name,
    )(w)


def _token_row_specs(n_parts, tm, width, col=None, buffered_once=False):
    kw = dict(pipeline_mode=pl.Buffered(1)) if buffered_once else {}
    col = col or (lambda i, *a: 0)
    if n_parts == 1:
        return [pl.BlockSpec((tm, width), lambda i, *a: (i, col(i, *a)), **kw)]
    n_p = TOK_P // tm
    return [pl.BlockSpec((tm, width), lambda i, *a: (jnp.minimum(i, n_p - 1), col(i, *a)), **kw),
            pl.BlockSpec((tm, width), lambda i, *a: (jnp.maximum(i - n_p, 0), col(i, *a)), **kw)]


def _proj_kernel(*refs, n_x):
    x_refs = refs[:n_x]
    sh_ref, sc_ref, w_ref, o_ref, u_ref, wb_ref = refs[n_x:]
    i = pl.program_id(0)
    j = pl.program_id(1)

    def modulate_from(x_ref):
        u_ref[...] = (x_ref[...] * (1.0 + sc_ref[0]) + sh_ref[0]).astype(BF16)

    @pl.when(j == 0)
    def _():
        if n_x == 1:
            modulate_from(x_refs[0])
        else:
            n_p = TOK_P // u_ref.shape[0]
            pl.when(i < n_p)(lambda: modulate_from(x_refs[0]))
            pl.when(i >= n_p)(lambda: modulate_from(x_refs[1]))

    @pl.when(i == 0)
    def _():
        wb_ref[j] = w_ref[...].astype(BF16)

    o_ref[...] = jnp.dot(u_ref[...], wb_ref[j], preferred_element_type=F32)


def _first_pass_block(i, j, nj):
    return jnp.where(i == 0, j, nj - 1)


def _proj(xs, mods, w, *, tn, name):
    tm = ROW_TILE
    n = w.shape[1]
    nj = n // tn
    return pl.pallas_call(
        functools.partial(_proj_kernel, n_x=len(xs)),
        grid=(N_TOK // tm, nj),
        in_specs=_token_row_specs(len(xs), tm, D_MODEL, buffered_once=True) + [
            _mod_spec(tm, 0), _mod_spec(tm, 1),
            pl.BlockSpec((D_MODEL, tn), lambda i, j: (0, _first_pass_block(i, j, nj)))],
        out_specs=pl.BlockSpec((tm, tn), lambda i, j: (i, j)),
        out_shape=jax.ShapeDtypeStruct((N_TOK, n), F32),
        scratch_shapes=[pltpu.VMEM((tm, D_MODEL), BF16),
                        pltpu.VMEM((nj, D_MODEL, tn), BF16)],
        compiler_params=_cparams(("arbitrary", "arbitrary")),
        name=name,
    )(*xs, mods, mods, w)


def _rms(x, g):
    return x * lax.rsqrt(jnp.mean(x * x, -1, keepdims=True) + QK_EPS) * g


def _dot_nt(a, b):
    return lax.dot_general(a, b, (((1,), (1,)), ((), ())), preferred_element_type=F32)


def _rope(x, cos, sin_signed):
    lane = lax.broadcasted_iota(jnp.int32, x.shape, 1)
    partner = jnp.where(lane % 2 == 0, pltpu.roll(x, HEAD_DIM - 1, 1), pltpu.roll(x, 1, 1))
    return x * cos + partner * sin_signed


def _attn_ctx_kernel(q_ref, k_ref, v_ref, qn_ref, kn_ref, o_ref, ko_ref, vo_ref):
    scale = HEAD_DIM ** -0.5
    kn = _rms(k_ref[...], kn_ref[...])
    v = v_ref[...]
    ko_ref[...] = kn
    vo_ref[...] = v
    kb = kn.astype(BF16)
    vb = v.astype(BF16)
    for g in range(Q_GROUP):
        cols = slice(g * HEAD_DIM, (g + 1) * HEAD_DIM)
        q = _rms(q_ref[:, cols], qn_ref[...]).astype(BF16)
        s = _dot_nt(q, kb) * scale
        p = jnp.exp(s - jnp.max(s, -1, keepdims=True))
        p = p / jnp.sum(p, -1, keepdims=True)
        o = jnp.dot(p.astype(BF16), vb, preferred_element_type=F32)
        o_ref[:, cols] = o.astype(o_ref.dtype)


def _attn_context(proj, q_norm, k_norm):
    qw = Q_GROUP * HEAD_DIM
    k_blk0 = Q_WIDTH // HEAD_DIM
    v_blk0 = (Q_WIDTH + KV_WIDTH) // HEAD_DIM
    vec = pl.BlockSpec((1, HEAD_DIM), lambda b, h: (0, 0))
    return pl.pallas_call(
        _attn_ctx_kernel,
        grid=(BATCH, N_KV_HEADS),
        in_specs=[pl.BlockSpec((SEQ, qw), lambda b, h: (b, h)),
                  pl.BlockSpec((SEQ, HEAD_DIM), lambda b, h: (b, k_blk0 + h)),
                  pl.BlockSpec((SEQ, HEAD_DIM), lambda b, h: (b, v_blk0 + h)),
                  vec, vec],
        out_specs=[pl.BlockSpec((SEQ, qw), lambda b, h: (b, h)),
                   pl.BlockSpec((SEQ, HEAD_DIM), lambda b, h: (b, h)),
                   pl.BlockSpec((SEQ, HEAD_DIM), lambda b, h: (b, h))],
        out_shape=[jax.ShapeDtypeStruct((N_TOK, Q_WIDTH), BF16),
                   jax.ShapeDtypeStruct((TOK_P, KV_WIDTH), F32),
                   jax.ShapeDtypeStruct((TOK_P, KV_WIDTH), F32)],
        compiler_params=_cparams(("arbitrary", "arbitrary")),
        name="attn_context",
    )(proj, proj, proj, q_norm.reshape(1, HEAD_DIM), k_norm.reshape(1, HEAD_DIM))


ATTN_Q_ROWS = 256


def _attn_lat_kernel(q_ref, k_ref, v_ref, ck_ref, cv_ref, qn_ref, kn_ref,
                     cq_ref, sq_ref, ck_tab_ref, sk_tab_ref, dst_ref, o_ref):
    del dst_ref
    scale = HEAD_DIM ** -0.5
    kb = _rope(_rms(k_ref[...], kn_ref[...]), ck_tab_ref[...], sk_tab_ref[...]).astype(BF16)
    vb = v_ref[...].astype(BF16)
    ckb = ck_ref[...].astype(BF16)
    cvb = cv_ref[...].astype(BF16)
    for g in range(Q_GROUP):
        cols = slice(g * HEAD_DIM, (g + 1) * HEAD_DIM)
        q = _rope(_rms(q_ref[:, cols], qn_ref[...]), cq_ref[...], sq_ref[...]).astype(BF16)
        s1 = _dot_nt(q, ckb) * scale
        s2 = _dot_nt(q, kb) * scale
        m = jnp.maximum(jnp.max(s1, -1, keepdims=True), jnp.max(s2, -1, keepdims=True))
        p1 = jnp.exp(s1 - m)
        p2 = jnp.exp(s2 - m)
        den = jnp.sum(p1, -1, keepdims=True) + jnp.sum(p2, -1, keepdims=True)
        o = (jnp.dot((p1 / den).astype(BF16), cvb, preferred_element_type=F32)
             + jnp.dot((p2 / den).astype(BF16), vb, preferred_element_type=F32))
        o_ref[:, cols] = o.astype(o_ref.dtype)


def _rope_tables():
    t = jnp.arange(DEC_SEQ)
    row = (t // GRID_W).astype(F32)
    col = (t % GRID_W).astype(F32)
    n_freq = HEAD_DIM // 4
    inv = 1.0 / (ROPE_THETA ** (jnp.arange(n_freq, dtype=F32) / n_freq))
    ang = jnp.concatenate([row[:, None] * inv, col[:, None] * inv], -1)
    cos = jnp.repeat(jnp.cos(ang), 2, axis=-1)
    sign = jnp.where(jnp.arange(HEAD_DIM) % 2 == 0, -1.0, 1.0).astype(F32)
    sin_signed = jnp.repeat(jnp.sin(ang), 2, axis=-1) * sign
    return cos, sin_signed


_IN_PLACE = pl.BlockSpec(memory_space=pl.ANY)


def _attn_latent(proj, cache_k, cache_v, q_norm, k_norm, dst):
    qw = Q_GROUP * HEAD_DIM
    nq = DEC_SEQ // ATTN_Q_ROWS
    q_row0 = TOK_P // ATTN_Q_ROWS
    kv_row0 = TOK_P // DEC_SEQ
    k_blk0 = Q_WIDTH // HEAD_DIM
    v_blk0 = (Q_WIDTH + KV_WIDTH) // HEAD_DIM
    cos, sin_signed = _rope_tables()
    vec = pl.BlockSpec((1, HEAD_DIM), lambda b, h, c: (0, 0))
    tab_q = pl.BlockSpec((ATTN_Q_ROWS, HEAD_DIM), lambda b, h, c: (c, 0))
    tab_k = pl.BlockSpec((DEC_SEQ, HEAD_DIM), lambda b, h, c: (0, 0))
    ctx = pl.BlockSpec((PAST_LEN, HEAD_DIM), lambda b, h, c: (b, h))
    return pl.pallas_call(
        _attn_lat_kernel,
        grid=(DEC_BATCH, N_KV_HEADS, nq),
        in_specs=[pl.BlockSpec((ATTN_Q_ROWS, qw), lambda b, h, c: (q_row0 + b * nq + c, h)),
                  pl.BlockSpec((DEC_SEQ, HEAD_DIM), lambda b, h, c: (kv_row0 + b, k_blk0 + h)),
                  pl.BlockSpec((DEC_SEQ, HEAD_DIM), lambda b, h, c: (kv_row0 + b, v_blk0 + h)),
                  ctx, ctx, vec, vec, tab_q, tab_q, tab_k, tab_k, _IN_PLACE],
        out_specs=pl.BlockSpec((ATTN_Q_ROWS, qw), lambda b, h, c: (q_row0 + b * nq + c, h)),
        out_shape=jax.ShapeDtypeStruct((N_TOK, Q_WIDTH), BF16),
        input_output_aliases={11: 0},
        compiler_params=_cparams(("arbitrary", "arbitrary", "arbitrary")),
        name="attn_latent",
    )(proj, proj, proj,
      cache_k.reshape(DEC_BATCH * PAST_LEN, KV_WIDTH), cache_v.reshape(DEC_BATCH * PAST_LEN, KV_WIDTH),
      q_norm.reshape(1, HEAD_DIM), k_norm.reshape(1, HEAD_DIM), cos, sin_signed, cos, sin_signed, dst)


LRU_CT = 512
assert (Q_WIDTH + 2 * KV_WIDTH) % LRU_CT == 0 and LRU_WIDTH % LRU_CT == 0


def _sigmoid_tanh(x):
    return 0.5 * (jnp.tanh(0.5 * x) + 1.0)
LRU_RC = 128


def _softplus(x):
    return jnp.maximum(x, 0.0) + jnp.log1p(jnp.exp(-jnp.abs(x)))


def _gelu_tanh(x):
    return 0.5 * x * (1.0 + jnp.tanh(math.sqrt(2.0 / math.pi) * (x + 0.044715 * (x * x * x))))


def _lru_kernel(*refs, seq, has_dst):
    xb_ref, gb_ref, cw_ref, cb_ref, lam_ref, wr_ref, br_ref, wi_ref, bi_ref, h0_ref = refs[:10]
    y_ref, hfin_ref, xpad_ref, a_ref, b_ref, hf_ref, hb_ref = refs[10 + int(has_dst):]
    ct = LRU_CT
    zeros = jnp.zeros((SUBLANE, ct), F32)
    xpad_ref[0:SUBLANE, :] = zeros
    xpad_ref[seq + SUBLANE:seq + 2 * SUBLANE, :] = zeros
    xpad_ref[SUBLANE:seq + SUBLANE, :] = xb_ref[...]
    cw = cw_ref[...]
    sp = _softplus(-lam_ref[...])
    for r0 in range(0, seq, LRU_RC):
        xc = cb_ref[...] + cw[0:1, :] * xpad_ref[r0 + 7:r0 + 7 + LRU_RC, :]
        for w in range(1, 4):
            xc = xc + cw[w:w + 1, :] * xpad_ref[r0 + 7 + w:r0 + 7 + w + LRU_RC, :]
        for kb in range(ct // LRU_BLOCK):
            cols = slice(kb * LRU_BLOCK, (kb + 1) * LRU_BLOCK)
            xk = xc[:, cols]
            xkb = xk.astype(BF16)
            for d in range(2):
                r = _sigmoid_tanh(jnp.dot(xkb, wr_ref[d, kb].astype(BF16), preferred_element_type=F32)
                                  + br_ref[d:d + 1, cols])
                i = _sigmoid_tanh(jnp.dot(xkb, wi_ref[d, kb].astype(BF16), preferred_element_type=F32)
                                  + bi_ref[d:d + 1, cols])
                log_a = -LRU_C * r * sp[d:d + 1, cols]
                a = jnp.exp(log_a)
                a_ref[d, r0:r0 + LRU_RC, cols] = a
                b_ref[d, r0:r0 + LRU_RC, cols] = jnp.sqrt(-jnp.tanh(log_a) * (a * a + 1.0)) * (i * xk)

    nblk = seq // SUBLANE
    row = lax.broadcasted_iota(jnp.int32, (SUBLANE, ct), 0)

    def body(i, carry):
        hf, hb = carry
        rf = pl.multiple_of(i * SUBLANE, SUBLANE)
        a = a_ref[0, pl.ds(rf, SUBLANE), :]
        b = b_ref[0, pl.ds(rf, SUBLANE), :]
        for s in (1, 2, 4):
            m = row >= s
            a_s = jnp.where(m, pltpu.roll(a, s, 0), 1.0)
            b_s = jnp.where(m, pltpu.roll(b, s, 0), 0.0)
            b = a * b_s + b
            a = a * a_s
        hblk = a * hf + b
        hf_ref[pl.ds(rf, SUBLANE), :] = hblk
        hf = hblk[SUBLANE - 1:SUBLANE, :]

        rb = pl.multiple_of((nblk - 1 - i) * SUBLANE, SUBLANE)
        a = a_ref[1, pl.ds(rb, SUBLANE), :]
        b = b_ref[1, pl.ds(rb, SUBLANE), :]
        for s in (1, 2, 4):
            m = row < SUBLANE - s
            a_s = jnp.where(m, pltpu.roll(a, SUBLANE - s, 0), 1.0)
            b_s = jnp.where(m, pltpu.roll(b, SUBLANE - s, 0), 0.0)
            b = a * b_s + b
            a = a * a_s
        hblk = a * hb + b
        hb_ref[pl.ds(rb, SUBLANE), :] = hblk
        hb = hblk[0:1, :]
        return hf, hb

    hf, hb = lax.fori_loop(0, nblk, body, (h0_ref[0:1, :], h0_ref[1:2, :]))
    hfin_ref[0:1, :] = hf
    hfin_ref[1:2, :] = hb
    for r0 in range(0, seq, LRU_RC):
        rows = slice(r0, r0 + LRU_RC)
        y = (hf_ref[rows, :] + hb_ref[rows, :]) * _gelu_tanh(gb_ref[rows, :])
        y_ref[rows, :] = y.astype(y_ref.dtype)


def _lru(proj, h0, p, *, seq, batch, row0, name, dst=None):
    ct = LRU_CT
    nkb = ct // LRU_BLOCK
    xb_blk0 = (Q_WIDTH + 2 * KV_WIDTH) // ct
    gb_blk0 = (Q_WIDTH + 2 * KV_WIDTH + LRU_WIDTH) // ct
    r0 = row0 // seq
    vec2 = pl.BlockSpec((2, ct), lambda b, c: (0, c))
    wblk = pl.BlockSpec((2, nkb, LRU_BLOCK, LRU_BLOCK), lambda b, c: (0, c, 0, 0))
    extra_specs, extra_args, aliases = [], [], {}
    if dst is not None:
        extra_specs, extra_args, aliases = [_IN_PLACE], [dst], {10: 0}
    return pl.pallas_call(
        functools.partial(_lru_kernel, seq=seq, has_dst=dst is not None),
        grid=(batch, LRU_WIDTH // ct),
        in_specs=[pl.BlockSpec((seq, ct), lambda b, c: (r0 + b, xb_blk0 + c)),
                  pl.BlockSpec((seq, ct), lambda b, c: (r0 + b, gb_blk0 + c)),
                  pl.BlockSpec((4, ct), lambda b, c: (0, c)),
                  pl.BlockSpec((1, ct), lambda b, c: (0, c)),
                  vec2, wblk, vec2, wblk, vec2,
                  pl.BlockSpec((None, 2, ct), lambda b, c: (b, 0, c))] + extra_specs,
        out_specs=[pl.BlockSpec((seq, ct), lambda b, c: (r0 + b, c)),
                   pl.BlockSpec((None, 2, ct), lambda b, c: (b, 0, c))],
        out_shape=[jax.ShapeDtypeStruct((N_TOK, LRU_WIDTH), BF16),
                   jax.ShapeDtypeStruct((batch, 2, LRU_WIDTH), F32)],
        input_output_aliases=aliases,
        scratch_shapes=[pltpu.VMEM((seq + 2 * SUBLANE, ct), F32),
                        pltpu.VMEM((2, seq, ct), F32),
                        pltpu.VMEM((2, seq, ct), F32),
                        pltpu.VMEM((seq, ct), F32),
                        pltpu.VMEM((seq, ct), F32)],
        compiler_params=_cparams(("arbitrary", "arbitrary")),
        name=name,
    )(proj, proj, p['conv_w'], p['conv_b'].reshape(1, LRU_WIDTH), p['lam'],
      p['w_r'], p['b_r'], p['w_i'], p['b_i'], h0, *extra_args)


OUT_TN = 1024
OUT_TM = 512


def _outproj_kernel(*refs, n_a, n_res, with_router):
    a_refs = refs[:n_a]
    w_refs = refs[n_a:2 * n_a]
    res_refs = refs[2 * n_a:2 * n_a + n_res]
    gate_ref, g_ref, b_ref = refs[2 * n_a + n_res:2 * n_a + n_res + 3]
    pos = 2 * n_a + n_res + 3
    if with_router:
        sh_ref, sc_ref, rt_ref = refs[pos:pos + 3]
        pos += 3
        o_ref, u_ref, lg_ref, acc_ref = refs[pos:pos + 4]
        pos += 4
    else:
        o_ref, acc_ref = refs[pos:pos + 2]
        pos += 2
    wb_refs = refs[pos:pos + n_a]
    i = pl.program_id(0)
    j = pl.program_id(1)
    nj = pl.num_programs(1)

    @pl.when(i == 0)
    def _():
        for k in range(n_a):
            wb_refs[k][j] = w_refs[k][...].astype(BF16)

    acc = jnp.dot(a_refs[0][...], wb_refs[0][j], preferred_element_type=F32)
    for k in range(1, n_a):
        acc = acc + jnp.dot(a_refs[k][...], wb_refs[k][j], preferred_element_type=F32)
    if n_res == 1:
        res = res_refs[0][...]
    else:
        n_p = TOK_P // acc_ref.shape[1]
        res = jnp.where(i < n_p, res_refs[0][...], res_refs[1][...])
    acc_ref[j] = DN_ALPHA * res + gate_ref[0] * acc

    @pl.when(j == nj - 1)
    def _():
        tm = acc_ref.shape[1]
        n_chunks = acc_ref.shape[0]
        rc = 128

        def chunk(c, _):
            r0 = pl.multiple_of(c * rc, rc)
            ys = [acc_ref[k, pl.ds(r0, rc), :] for k in range(n_chunks)]
            tot = ys[0].sum(-1, keepdims=True)
            for y in ys[1:]:
                tot = tot + y.sum(-1, keepdims=True)
            mu = tot / D_MODEL
            sq = None
            for y in ys:
                t = ((y - mu) * (y - mu)).sum(-1, keepdims=True)
                sq = t if sq is None else sq + t
            rstd = lax.rsqrt(sq / D_MODEL + LN_EPS)
            lg = None
            us = []
            for k, y in enumerate(ys):
                cols = slice(k * OUT_TN, (k + 1) * OUT_TN)
                xn = (y - mu) * rstd * g_ref[:, cols] + b_ref[:, cols]
                o_ref[pl.ds(r0, rc), cols] = xn
                if with_router:
                    u = xn * (1.0 + sc_ref[0][:, cols]) + sh_ref[0][:, cols]
                    us.append(u)
                    t = _dot3(u, rt_ref[cols, :])
                    lg = t if lg is None else lg + t
            if with_router:
                lg_ref[pl.ds(r0, rc), :] = lg
                half = n_chunks // 2
                for k in range(half):
                    cols = slice(k * OUT_TN, (k + 1) * OUT_TN)
                    u_ref[pl.ds(r0, rc), cols] = _pack_bf16_pair(us[k], us[k + half])
            return 0

        lax.fori_loop(0, tm // rc, chunk, 0)


def _outproj(a_list, w, res, mods, gate_chunk, g, b, *, name, router=None, router_chunks=None):
    tm = OUT_TM
    tn = OUT_TN
    nj = D_MODEL // tn
    n_a = len(a_list)
    in_specs = []
    k0 = 0
    w_specs = []
    wb_shapes = []
    for a in a_list:
        ka = a.shape[1]
        in_specs.append(pl.BlockSpec((tm, ka), lambda i, j: (i, 0)))
        w_specs.append(pl.BlockSpec((ka, tn),
                                    lambda i, j, blk=k0 // ka: (blk, _first_pass_block(i, j, nj))))
        wb_shapes.append(pltpu.VMEM((nj, ka, tn), BF16))
        k0 += ka
    in_specs += w_specs
    in_specs += _token_row_specs(len(res), tm, tn, col=lambda i, j: j)
    in_specs += [_mod_spec(tm, gate_chunk, tn, col_of=lambda j: j),
                 pl.BlockSpec((1, D_MODEL), lambda i, j: (0, 0)),
                 pl.BlockSpec((1, D_MODEL), lambda i, j: (0, 0))]
    args = list(a_list) + [w] * n_a + list(res) + [mods, g.reshape(1, D_MODEL), b.reshape(1, D_MODEL)]
    out_specs = [pl.BlockSpec((tm, D_MODEL), lambda i, j: (i, 0))]
    out_shape = [jax.ShapeDtypeStruct((N_TOK, D_MODEL), F32)]
    if router is not None:
        in_specs += [_mod_spec(tm, router_chunks[0]), _mod_spec(tm, router_chunks[1]),
                     pl.BlockSpec((D_MODEL, LANE), lambda i, j: (0, 0))]
        args += [mods, mods, router]
        out_specs += [pl.BlockSpec((tm, D_MODEL // 2), lambda i, j: (i, 0)),
                      pl.BlockSpec((tm, LANE), lambda i, j: (i, 0))]
        out_shape += [jax.ShapeDtypeStruct((N_TOK, D_MODEL // 2), jnp.uint32),
                      jax.ShapeDtypeStruct((N_TOK, LANE), F32)]
    return pl.pallas_call(
        functools.partial(_outproj_kernel, n_a=n_a, n_res=len(res), with_router=router is not None),
        grid=(N_TOK // tm, nj),
        in_specs=in_specs,
        out_specs=out_specs,
        out_shape=out_shape,
        scratch_shapes=[pltpu.VMEM((nj, tm, tn), F32)] + wb_shapes,
        compiler_params=_cparams(("arbitrary", "arbitrary")),
        name=name,
    )(*args)


FFN_TF = 512


def _ffn_kernel(x_ref, sh_ref, sc_ref, gate_ref, g_ref, b_ref, w1_ref, w3_ref, w2_ref, o_ref, u_ref):
    j = pl.program_id(1)
    nj = pl.num_programs(1)

    @pl.when(j == 0)
    def _():
        u_ref[...] = (x_ref[...] * (1.0 + sc_ref[0]) + sh_ref[0]).astype(BF16)
        o_ref[...] = jnp.zeros(o_ref.shape, F32)

    u = u_ref[...]
    h1 = jnp.dot(u, w1_ref[...], preferred_element_type=F32)
    h3 = jnp.dot(u, w3_ref[...], preferred_element_type=F32)
    h = (_silu(h1) * h3).astype(BF16)
    o_ref[...] += jnp.dot(h, w2_ref[...], preferred_element_type=F32)

    @pl.when(j == nj - 1)
    def _():
        rc = 128

        def chunk(c, _):
            r0 = pl.multiple_of(c * rc, rc)
            y = DN_ALPHA * x_ref[pl.ds(r0, rc), :] + gate_ref[0] * o_ref[pl.ds(r0, rc), :]
            o_ref[pl.ds(r0, rc), :] = _layer_norm_rows(y, g_ref[...], b_ref[...])
            return 0

        lax.fori_loop(0, o_ref.shape[0] // rc, chunk, 0)


def _ffn(x, mods, p):
    tm = ROW_TILE
    tf = FFN_TF
    vec = pl.BlockSpec((1, D_MODEL), lambda i, j: (0, 0))
    return pl.pallas_call(
        _ffn_kernel,
        grid=(N_TOK // tm, D_FF // tf),
        in_specs=[pl.BlockSpec((tm, D_MODEL), lambda i, j: (i, 0), pipeline_mode=pl.Buffered(1)),
                  _mod_spec(tm, 3), _mod_spec(tm, 4), _mod_spec(tm, 5), vec, vec,
                  pl.BlockSpec((D_MODEL, tf), lambda i, j: (0, j)),
                  pl.BlockSpec((D_MODEL, tf), lambda i, j: (0, j)),
                  pl.BlockSpec((tf, D_MODEL), lambda i, j: (j, 0))],
        out_specs=pl.BlockSpec((tm, D_MODEL), lambda i, j: (i, 0)),
        out_shape=jax.ShapeDtypeStruct((N_TOK, D_MODEL), F32),
        scratch_shapes=[pltpu.VMEM((tm, D_MODEL), BF16)],
        compiler_params=_cparams(("arbitrary", "arbitrary")),
        name="ffn_swiglu",
    )(x, mods, mods, mods, p['ln2_g'].reshape(1, D_MODEL), p['ln2_b'].reshape(1, D_MODEL),
      _cast_bf16(p['ffn_w1'], 256, "ffn_w1_bf16"), _cast_bf16(p['ffn_w3'], 256, "ffn_w3_bf16"),
      _cast_bf16(p['ffn_w2'], 512, "ffn_w2_bf16"))


def _filter_mlp_kernel(feat_ref, w1_ref, b1_ref, f1_ref, w2_ref, b2_ref, f2_ref, o_ref):
    h = jnp.sin(f1_ref[...] * (_dot3(feat_ref[...], w1_ref[...]) + b1_ref[...]))
    o_ref[...] = jnp.sin(f2_ref[...] * (_dot3(h, w2_ref[...]) + b2_ref[...]))


def _filter_features(seq):
    t = jnp.arange(seq, dtype=F32)
    t01 = t / (seq - 1)
    w = 2.0 * math.pi * t / seq
    f = jnp.linspace(1e-4, HYENA_BANDS - 1, HYENA_BANDS, dtype=F32)
    fw = w[:, None] * f[None, :]
    feat = jnp.concatenate([t01[:, None], jnp.cos(fw), -jnp.sin(fw)], -1)
    return jnp.pad(feat, ((0, 0), (0, LANE - HYENA_EMB))), t01[:, None]


def _filter_mlp(seq, p):
    feat, t01 = _filter_features(seq)
    hid = HYENA_FILTER_HIDDEN
    w1 = jnp.pad(p['filt_w1'], ((0, LANE - HYENA_EMB), (0, 0)))
    row = lambda v: v.reshape(1, hid)
    h2 = pl.pallas_call(
        _filter_mlp_kernel,
        out_shape=jax.ShapeDtypeStruct((seq, hid), F32),
        compiler_params=pltpu.CompilerParams(vmem_limit_bytes=VMEM_LIMIT),
        name=f"hyena_filter_mlp_{seq}",
    )(feat, w1, row(p['filt_b1']), row(p['filt_f1']), p['filt_w2'], row(p['filt_b2']), row(p['filt_f2']))
    return h2, t01


def _dft_matrices(seq):
    n = 2 * seq
    f = jnp.arange(seq, dtype=jnp.int32)
    k = (f[:, None] * f[None, :]) % n
    ang = k.astype(F32) * (math.pi / seq)
    c = jnp.cos(ang)
    s = -jnp.sin(ang)
    alt = jnp.where(f % 2 == 0, 1.0, -1.0).astype(F32)
    s_fwd = s.at[0, :].set(alt)
    s_inv = s.at[:, 0].set(alt)
    fwd = jnp.concatenate([c, s_fwd], axis=0).astype(BF16)
    inv = jnp.concatenate([c, s_inv], axis=1).astype(BF16)
    return fwd, inv, alt[:, None]


HY_RC = 128


def _hyena_kernel(*refs, seq, nb, ct, has_dst):
    (x1_ref, x2_ref, z_ref, sw1_ref, sw2_ref, swz_ref, sb1_ref, sb2_ref, sbz_ref,
     h2_ref, t01_ref, alt_ref, w3_ref, dec_ref, bias_ref, fwd_ref, inv_ref) = refs[:17]
    (o_ref, kr_ref, ki_ref, zb_ref, zf_ref, yb_ref, pad_ref, g1_ref, g2_ref,
     zc_ref) = refs[17 + int(has_dst):]
    nrc = seq // HY_RC

    @pl.when(pl.program_id(1) == 0)
    def _():
        t01 = t01_ref[...]
        h2 = h2_ref[...]
        row0 = lax.broadcasted_iota(jnp.int32, (seq, ct), 0) == 0
        wf = jnp.where(row0, 1.0, 2.0) / (2.0 * seq)
        for n in range(HYENA_ORDER):
            kpos = _dot3(h2, w3_ref[n]) * jnp.exp(-t01 * jnp.abs(dec_ref[n:n + 1, :]))
            kneg = _dot3(h2, w3_ref[HYENA_ORDER + n]) * jnp.exp(
                -t01 * jnp.abs(dec_ref[HYENA_ORDER + n:HYENA_ORDER + n + 1, :]))
            kneg = jnp.where(row0, 0.0, kneg)
            ksum = kpos + kneg
            kdif = kpos - kneg
            kr = jnp.dot(fwd_ref[0:seq, :], ksum.astype(BF16), preferred_element_type=F32)
            ki = jnp.dot(fwd_ref[seq:2 * seq, :], kdif.astype(BF16), preferred_element_type=F32)
            nyq = jnp.sum(alt_ref[...] * ksum, axis=0, keepdims=True) / (2.0 * seq)
            kr_ref[n] = kr * wf
            ki_ref[n] = jnp.where(row0, nyq, ki * wf)

    zeros = jnp.zeros((SUBLANE, ct), F32)
    pad_ref[0:SUBLANE, :] = zeros
    pad_ref[seq + SUBLANE:seq + 2 * SUBLANE, :] = zeros

    def short_conv(src_ref, rows0, w_ref, b_ref, dst_ref):
        pad_ref[SUBLANE:seq + SUBLANE, :] = src_ref[rows0:rows0 + seq, :]
        w = w_ref[...]
        for c in range(nrc):
            r0 = c * HY_RC
            acc = b_ref[...] + w[0:1, :] * pad_ref[r0 + 7:r0 + 7 + HY_RC, :]
            acc = acc + w[1:2, :] * pad_ref[r0 + 8:r0 + 8 + HY_RC, :]
            acc = acc + w[2:3, :] * pad_ref[r0 + 9:r0 + 9 + HY_RC, :]
            dst_ref[r0:r0 + HY_RC, :] = acc

    for bi in range(nb):
        rows0 = bi * seq
        short_conv(x1_ref, rows0, sw1_ref, sb1_ref, g1_ref)
        short_conv(x2_ref, rows0, sw2_ref, sb2_ref, g2_ref)
        short_conv(z_ref, rows0, swz_ref, sbz_ref, zc_ref)
        for n, gate_ref in enumerate((g1_ref, g2_ref)):
            for c in range(nrc):
                rows = slice(c * HY_RC, (c + 1) * HY_RC)
                zb_ref[rows, :] = zc_ref[rows, :].astype(BF16)
            zf_ref[...] = jnp.dot(fwd_ref[...], zb_ref[...], preferred_element_type=F32)
            for c in range(nrc):
                rows = slice(c * HY_RC, (c + 1) * HY_RC)
                rows_i = slice(seq + c * HY_RC, seq + (c + 1) * HY_RC)
                zr = zf_ref[rows, :]
                zi = zf_ref[rows_i, :]
                kr = kr_ref[n, rows, :]
                ki = ki_ref[n, rows, :]
                yr = zr * kr - zi * ki
                yi = zr * ki + zi * kr
                if c == 0:
                    first = lax.broadcasted_iota(jnp.int32, (HY_RC, ct), 0) == 0
                    yr = jnp.where(first, zr * kr, yr)
                    yi = jnp.where(first, zi * ki, yi)
                yb_ref[rows, :] = yr.astype(BF16)
                yb_ref[rows_i, :] = yi.astype(BF16)
            zf_ref[0:seq, :] = jnp.dot(inv_ref[...], yb_ref[...], preferred_element_type=F32)
            for c in range(nrc):
                rows = slice(c * HY_RC, (c + 1) * HY_RC)
                zc = zc_ref[rows, :]
                znew = gate_ref[rows, :] * (zf_ref[rows, :] + zc * bias_ref[n:n + 1, :])
                if n == HYENA_ORDER - 1:
                    o_ref[rows0 + c * HY_RC:rows0 + (c + 1) * HY_RC, :] = znew.astype(o_ref.dtype)
                else:
                    zc_ref[rows, :] = znew


def _hyena(proj, p, *, seq, batch, row0, nb, ct, name, dst=None):
    nct = D_MODEL // ct
    extra_specs, extra_args, aliases = [], [], {}
    if dst is not None:
        extra_specs, extra_args, aliases = [_IN_PLACE], [dst], {17: 0}
    h2, t01 = _filter_mlp(seq, p)
    fwd, inv, alt = _dft_matrices(seq)
    rb0 = row0 // (nb * seq)
    w3 = p['filt_w3'].reshape(HYENA_FILTER_HIDDEN, 2 * HYENA_ORDER, D_MODEL).transpose(1, 0, 2)
    dec = p['filt_decay'].reshape(2 * HYENA_ORDER, D_MODEL)
    sw = p['short_w']
    sb = p['short_b'].reshape(1, 3 * D_MODEL)
    slab = lambda k: pl.BlockSpec((nb * seq, ct), lambda c, b: (rb0 + b, k * nct + c))
    swk = lambda k: pl.BlockSpec((3, ct), lambda c, b: (0, k * nct + c))
    sbk = lambda k: pl.BlockSpec((1, ct), lambda c, b: (0, k * nct + c))
    const = lambda shape: pl.BlockSpec(shape, lambda c, b: tuple(0 for _ in shape))
    return pl.pallas_call(
        functools.partial(_hyena_kernel, seq=seq, nb=nb, ct=ct, has_dst=dst is not None),
        grid=(nct, batch // nb),
        in_specs=[slab(0), slab(1), slab(2), swk(0), swk(1), swk(2), sbk(0), sbk(1), sbk(2),
                  const((seq, HYENA_FILTER_HIDDEN)), const((seq, 1)), const((seq, 1)),
                  pl.BlockSpec((2 * HYENA_ORDER, HYENA_FILTER_HIDDEN, ct), lambda c, b: (0, 0, c)),
                  pl.BlockSpec((2 * HYENA_ORDER, ct), lambda c, b: (0, c)),
                  pl.BlockSpec((HYENA_ORDER, ct), lambda c, b: (0, c)),
                  pl.BlockSpec((2 * seq, seq), lambda c, b: (0, 0), pipeline_mode=pl.Buffered(1)),
                  pl.BlockSpec((seq, 2 * seq), lambda c, b: (0, 0), pipeline_mode=pl.Buffered(1))]
        + extra_specs,
        out_specs=pl.BlockSpec((nb * seq, ct), lambda c, b: (rb0 + b, c)),
        out_shape=jax.ShapeDtypeStruct((N_TOK, D_MODEL), BF16),
        input_output_aliases=aliases,
        scratch_shapes=[pltpu.VMEM((HYENA_ORDER, seq, ct), F32),
                        pltpu.VMEM((HYENA_ORDER, seq, ct), F32),
                        pltpu.VMEM((seq, ct), BF16),
                        pltpu.VMEM((2 * seq, ct), F32),
                        pltpu.VMEM((2 * seq, ct), BF16),
                        pltpu.VMEM((seq + 2 * SUBLANE, ct), F32),
                        pltpu.VMEM((seq, ct), F32),
                        pltpu.VMEM((seq, ct), F32),
                        pltpu.VMEM((seq, ct), F32)],
        compiler_params=_cparams(("arbitrary", "arbitrary")),
        name=name,
    )(proj, proj, proj, sw, sw, sw, sb, sb, sb, h2, t01, alt, w3, dec, p['filt_bias'], fwd, inv,
      *extra_args)


def _moe_kernel(ie_ref, ib_ref, ir_ref, x_ref, w1_ref, w3_ref, w2_ref, o_hbm,
                acc_ref, w1b_ref, w3b_ref, w2b_ref, sem):
    w = pl.program_id(0)
    j = pl.program_id(1)
    nj = pl.num_programs(1)
    rows = ir_ref[w]
    ntiles = (rows + (MOE_ROW_TILE - 1)) // MOE_ROW_TILE
    rt = MOE_ROW_TILE

    def partial_out(r):
        off = r * rt if isinstance(r, int) else pl.multiple_of(r * rt, rt)
        xt = _unpack_bf16_pair(x_ref[pl.ds(off, rt), :])
        h1 = jnp.dot(xt, w1b_ref[...], preferred_element_type=F32)
        h3 = jnp.dot(xt, w3b_ref[...], preferred_element_type=F32)
        h = (_silu(h1) * h3).astype(BF16)
        return off, jnp.dot(h, w2b_ref[...], preferred_element_type=F32)

    def cast_weights():
        w1b_ref[...] = w1_ref[...].astype(BF16)
        w3b_ref[...] = w3_ref[...].astype(BF16)
        w2b_ref[...] = w2_ref[...].astype(BF16)

    @pl.when(rows > 0)
    def _():
        def run_tiles(first):
            def one(r):
                off, part = partial_out(r)
                if first:
                    acc_ref[pl.ds(off, rt), :] = part
                else:
                    acc_ref[pl.ds(off, rt), :] += part

            def pair(q, _):
                one(2 * q)
                one(2 * q + 1)
                return 0

            @pl.when(ntiles >= 2)
            def _():
                cast_weights()
                one(0)
                one(1)
                lax.fori_loop(1, ntiles // 2, pair, 0)

                @pl.when(ntiles % 2 == 1)
                def _():
                    one(ntiles - 1)

            @pl.when(ntiles == 1)
            def _():
                cast_weights()
                one(0)

        @pl.when(j == 0)
        def _():
            run_tiles(True)

        @pl.when(j > 0)
        def _():
            run_tiles(False)

        @pl.when(j == nj - 1)
        def _():
            base = ib_ref[w] * MOE_CHUNK

            def tile_copy(off):
                return pltpu.make_async_copy(acc_ref.at[pl.ds(off, rt)],
                                             o_hbm.at[pl.ds(base + off, rt)], sem)

            def start(r, _):
                tile_copy(pl.multiple_of(r * rt, rt)).start()
                return 0

            def wait(r, _):
                tile_copy(pl.multiple_of(r * rt, rt)).wait()
                return 0

            lax.fori_loop(0, ntiles, start, 0)
            lax.fori_loop(0, ntiles, wait, 0)


def _moe_experts(xs, item_expert, item_block, item_rows, p):
    tj = MOE_FF_TILE
    nj = D_FF_EXPERT // tj

    def jeff(w, j, ir):
        return jnp.where(ir[w] > 0, j, nj - 1)

    grid_spec = pltpu.PrefetchScalarGridSpec(
        num_scalar_prefetch=3,
        grid=(MOE_ITEMS, nj),
        in_specs=[pl.BlockSpec((MOE_CHUNK, D_MODEL // 2), lambda w, j, ie, ib, ir: (ib[w], 0),
                               pipeline_mode=pl.Buffered(1)),
                  pl.BlockSpec((None, D_MODEL, tj), lambda w, j, ie, ib, ir: (ie[w], 0, jeff(w, j, ir))),
                  pl.BlockSpec((None, D_MODEL, tj), lambda w, j, ie, ib, ir: (ie[w], 0, jeff(w, j, ir))),
                  pl.BlockSpec((None, tj, D_MODEL), lambda w, j, ie, ib, ir: (ie[w], jeff(w, j, ir), 0))],
        out_specs=pl.BlockSpec(memory_space=pl.ANY),
        scratch_shapes=[pltpu.VMEM((MOE_CHUNK, D_MODEL), F32),
                        pltpu.VMEM((D_MODEL, tj), BF16),
                        pltpu.VMEM((D_MODEL, tj), BF16),
                        pltpu.VMEM((tj, D_MODEL), BF16),
                        pltpu.SemaphoreType.DMA(())],
    )
    return pl.pallas_call(
        _moe_kernel,
        grid_spec=grid_spec,
        out_shape=jax.ShapeDtypeStruct((MOE_ROWS, D_MODEL), F32),
        compiler_params=_cparams(("arbitrary", "arbitrary")),
        name="moe_experts",
    )(item_expert, item_block, item_rows, xs, p['exp_w1'], p['exp_w3'], p['exp_w2'])


ROUTE_TM = 256
R_E0, R_E1, R_P0, R_P1, R_RANK0, R_RANK1 = range(6)


def _route_kernel(lg_ref, o_ref, cnt_ref, carry_ref):
    i = pl.program_id(0)
    tm = ROUTE_TM

    @pl.when(i == 0)
    def _():
        carry_ref[...] = jnp.zeros(carry_ref.shape, F32)

    lane = lax.broadcasted_iota(jnp.int32, (tm, LANE), 1)
    lg = jnp.where(lane < N_EXPERTS, lg_ref[...], -jnp.inf)
    m0 = jnp.max(lg, -1, keepdims=True)
    e0 = jnp.min(jnp.where(lg == m0, lane, LANE), -1, keepdims=True)
    lg1 = jnp.where(lane == e0, -jnp.inf, lg)
    m1 = jnp.max(lg1, -1, keepdims=True)
    e1 = jnp.min(jnp.where(lg1 == m1, lane, LANE), -1, keepdims=True)
    t = jnp.exp(m1 - m0)
    p0 = 1.0 / (1.0 + t)
    p1 = t / (1.0 + t)
    hit = ((lane == e0) | (lane == e1)).astype(BF16)
    r_i = lax.broadcasted_iota(jnp.int32, (tm, tm), 0)
    c_i = lax.broadcasted_iota(jnp.int32, (tm, tm), 1)
    before = (c_i < r_i).astype(BF16)
    pref = jnp.dot(before, hit, preferred_element_type=F32) + carry_ref[0:1, :]
    rank0 = jnp.sum(jnp.where(lane == e0, pref, 0.0), -1, keepdims=True)
    rank1 = jnp.sum(jnp.where(lane == e1, pref, 0.0), -1, keepdims=True)
    carry_ref[0:1, :] = carry_ref[0:1, :] + jnp.sum(hit.astype(F32), axis=0, keepdims=True)
    rec = jnp.zeros((tm, LANE), F32)
    for k, v in ((R_E0, e0.astype(F32)), (R_E1, e1.astype(F32)), (R_P0, p0), (R_P1, p1),
                 (R_RANK0, rank0), (R_RANK1, rank1)):
        rec = jnp.where(lane == k, v, rec)
    o_ref[...] = rec
    cnt_ref[...] = carry_ref[...]


def _route_records(logits):
    return pl.pallas_call(
        _route_kernel,
        grid=(N_TOK // ROUTE_TM,),
        in_specs=[pl.BlockSpec((ROUTE_TM, LANE), lambda i: (i, 0))],
        out_specs=[pl.BlockSpec((ROUTE_TM, LANE), lambda i: (i, 0)),
                   pl.BlockSpec((SUBLANE, LANE), lambda i: (0, 0))],
        out_shape=[jax.ShapeDtypeStruct((N_TOK, LANE), F32),
                   jax.ShapeDtypeStruct((SUBLANE, LANE), F32)],
        scratch_shapes=[pltpu.VMEM((SUBLANE, LANE), F32)],
        compiler_params=_cparams(("arbitrary",)),
        name="moe_route",
    )(logits)


def _route(logits):
    rec, cnt = _route_records(logits)
    e_flat = rec[:, R_E0:R_E1 + 1].astype(jnp.int32).reshape(-1)
    rank = rec[:, R_RANK0:R_RANK1 + 1].astype(jnp.int32).reshape(-1)
    counts = cnt[0, :N_EXPERTS].astype(jnp.int32)
    blocks = (counts + MOE_CHUNK - 1) // MOE_CHUNK
    bend = jnp.cumsum(blocks)
    bstart = bend - blocks
    total = bend[-1]
    pos = bstart[e_flat] * MOE_CHUNK + rank
    w = jnp.arange(MOE_ITEMS, dtype=jnp.int32)
    w_eff = jnp.minimum(w, total - 1)
    item_expert = jnp.minimum(jnp.sum((w_eff[:, None] >= bend[None, :]).astype(jnp.int32), axis=1),
                              N_EXPERTS - 1)
    item_rows = jnp.where(w < total,
                          jnp.clip(counts[item_expert] - (w - bstart[item_expert]) * MOE_CHUNK, 0, MOE_CHUNK),
                          0)
    return rec, pos.reshape(N_TOK, TOP_K), item_expert.astype(jnp.int32), \
        w_eff.astype(jnp.int32), item_rows.astype(jnp.int32)


DISPATCH_TM = 256


def _dispatch_kernel(pos_ref, ib_ref, ir_ref, u_ref, o_hbm, zero_ref, sem, zsem):
    i = pl.program_id(0)
    tm = DISPATCH_TM
    rt = MOE_ROW_TILE

    @pl.when(i == 0)
    def _():
        zero_ref[...] = jnp.zeros(zero_ref.shape, zero_ref.dtype)

        def tail_copy(w):
            last_tile = (ir_ref[w] - 1) // rt
            start = pl.multiple_of(ib_ref[w] * MOE_CHUNK + last_tile * rt, rt)
            return pltpu.make_async_copy(zero_ref, o_hbm.at[pl.ds(start, rt)], zsem)

        for w in range(MOE_ITEMS):
            pl.when(ir_ref[w] > 0)(lambda w=w: tail_copy(w).start())
        for w in range(MOE_ITEMS):
            pl.when(ir_ref[w] > 0)(lambda w=w: tail_copy(w).wait())

    def row_copy(r, dst_row):
        return pltpu.make_async_copy(u_ref.at[pl.ds(r, 1)], o_hbm.at[pl.ds(dst_row, 1)], sem)

    def start(r, _):
        a = (i * tm + r) * TOP_K
        for k in range(TOP_K):
            row_copy(r, pos_ref[a + k]).start(priority=k)
        return 0

    def wait(r, _):
        row_copy(0, 0).wait()
        return 0

    lax.fori_loop(0, tm, start, 0, unroll=8)
    lax.fori_loop(0, tm * TOP_K, wait, 0, unroll=8)


def _dispatch(pos_flat, item_block, item_rows, u):
    width = u.shape[1]
    grid_spec = pltpu.PrefetchScalarGridSpec(
        num_scalar_prefetch=3,
        grid=(N_TOK // DISPATCH_TM,),
        in_specs=[pl.BlockSpec((DISPATCH_TM, width), lambda i, pos, ib, ir: (i, 0))],
        out_specs=pl.BlockSpec(memory_space=pl.ANY),
        scratch_shapes=[pltpu.VMEM((MOE_ROW_TILE, width), u.dtype),
                        pltpu.SemaphoreType.DMA(()),
                        pltpu.SemaphoreType.DMA(())],
    )
    return pl.pallas_call(
        _dispatch_kernel,
        grid_spec=grid_spec,
        out_shape=jax.ShapeDtypeStruct((MOE_ROWS, width), u.dtype),
        compiler_params=_cparams(("arbitrary",)),
        name="moe_dispatch",
    )(pos_flat, item_block, item_rows, u)


def _postnorm_kernel(x_ref, y0_ref, y1_ref, rec_ref, gate_ref, g_ref, b_ref, op_ref, os_ref):
    i = pl.program_id(0)
    n_p = TOK_P // x_ref.shape[0]
    delta = (rec_ref[:, R_P0:R_P0 + 1] * y0_ref[...] + rec_ref[:, R_P1:R_P1 + 1] * y1_ref[...])
    y = DN_ALPHA * x_ref[...] + gate_ref[0] * delta
    out = _layer_norm_rows(y, g_ref[...], b_ref[...])

    @pl.when(i < n_p)
    def _():
        op_ref[...] = out

    @pl.when(i >= n_p)
    def _():
        os_ref[...] = out


def _postnorm(x, y0, y1, rec, mods, gate_chunk, g, b):
    tm = 256
    vec = pl.BlockSpec((1, D_MODEL), lambda i: (0, 0))
    rows = pl.BlockSpec((tm, D_MODEL), lambda i: (i, 0))
    return pl.pallas_call(
        _postnorm_kernel,
        grid=(N_TOK // tm,),
        in_specs=[rows, rows, rows, pl.BlockSpec((tm, LANE), lambda i: (i, 0)),
                  _mod_spec(tm, gate_chunk), vec, vec],
        out_specs=_token_row_specs(2, tm, D_MODEL),
        out_shape=[jax.ShapeDtypeStruct((TOK_P, D_MODEL), F32),
                   jax.ShapeDtypeStruct((TOK_S, D_MODEL), F32)],
        compiler_params=_cparams(("arbitrary",)),
        name="final_postnorm",
    )(x, y0, y1, rec, mods, g.reshape(1, D_MODEL), b.reshape(1, D_MODEL))


def kernel(x_prompt, x_sample, c, c_ctx, cache_l0_k, cache_l0_v, state_l0_lru, l0_ada_w, l0_ada_b, l0_w_in, l0_q_norm, l0_k_norm, l0_lru_conv_w, l0_lru_conv_b, l0_lru_lambda, l0_lru_w_r, l0_lru_b_r, l0_lru_w_i, l0_lru_b_i, l0_w_out, l0_ln1_g, l0_ln1_b, l0_ffn_w1, l0_ffn_w3, l0_ffn_w2, l0_ln2_g, l0_ln2_b, l1_ada_w, l1_ada_b, l1_w_in, l1_short_w, l1_short_b, l1_filt_w1, l1_filt_b1, l1_filt_f1, l1_filt_w2, l1_filt_b2, l1_filt_f2, l1_filt_w3, l1_filt_decay, l1_filt_bias, l1_w_out, l1_ln1_g, l1_ln1_b, l1_router, l1_exp_w1, l1_exp_w3, l1_exp_w2, l1_ln2_g, l1_ln2_b):
    x_in = [x_prompt.reshape(TOK_P, D_MODEL), x_sample.reshape(TOK_S, D_MODEL)]
    cond = jnp.concatenate([c_ctx[None, :], c, jnp.zeros((N_COND - 1 - DEC_BATCH, D_MODEL), F32)], axis=0)
    mods0 = _ada(cond, l0_ada_w, l0_ada_b)
    mods1 = _ada(cond, l1_ada_w, l1_ada_b)

    lru_p = dict(conv_w=l0_lru_conv_w, conv_b=l0_lru_conv_b, lam=l0_lru_lambda,
                 w_r=l0_lru_w_r, b_r=l0_lru_b_r, w_i=l0_lru_w_i, b_i=l0_lru_b_i)
    proj0 = _proj(x_in, mods0, l0_w_in, tn=512, name="l0_in_proj")
    attn, new_k, new_v = _attn_context(proj0, l0_q_norm, l0_k_norm)
    attn = _attn_latent(proj0, cache_l0_k, cache_l0_v, l0_q_norm, l0_k_norm, attn)
    lru, new_h = _lru(proj0, jnp.zeros((BATCH, 2, LRU_WIDTH), F32), lru_p,
                      seq=SEQ, batch=BATCH, row0=0, name="lru_context")
    lru, _ = _lru(proj0, state_l0_lru, lru_p,
                  seq=DEC_SEQ, batch=DEC_BATCH, row0=TOK_P, name="lru_latent", dst=lru)
    x = _outproj([attn, lru], l0_w_out, x_in, mods0, 2, l0_ln1_g, l0_ln1_b, name="l0_out_proj")[0]
    x = _ffn(x, mods0, dict(ffn_w1=l0_ffn_w1, ffn_w3=l0_ffn_w3, ffn_w2=l0_ffn_w2,
                            ln2_g=l0_ln2_g, ln2_b=l0_ln2_b))

    hy_p = dict(short_w=l1_short_w, short_b=l1_short_b, filt_w1=l1_filt_w1, filt_b1=l1_filt_b1,
                filt_f1=l1_filt_f1, filt_w2=l1_filt_w2, filt_b2=l1_filt_b2, filt_f2=l1_filt_f2,
                filt_w3=l1_filt_w3, filt_decay=l1_filt_decay, filt_bias=l1_filt_bias)
    proj1 = _proj([x], mods1, l1_w_in, tn=512, name="l1_in_proj")
    z = _hyena(proj1, hy_p, seq=SEQ, batch=BATCH, row0=0, nb=2, ct=1024, name="hyena_context")
    z = _hyena(proj1, hy_p, seq=DEC_SEQ, batch=DEC_BATCH, row0=TOK_P, nb=1, ct=512, name="hyena_latent",
               dst=z)
    router = jnp.pad(l1_router, ((0, 0), (0, LANE - N_EXPERTS)))
    x, u, logits = _outproj([z], l1_w_out, [x], mods1, 2, l1_ln1_g, l1_ln1_b, name="l1_out_proj",
                            router=router, router_chunks=(3, 4))
    rec, pos, item_expert, item_block, item_rows = _route(logits)
    xs = _dispatch(pos.reshape(-1), item_block, item_rows, u)
    ys = _moe_experts(xs, item_expert, item_block, item_rows,
                      dict(exp_w1=l1_exp_w1, exp_w3=l1_exp_w3, exp_w2=l1_exp_w2))
    y_prompt, y_sample = _postnorm(x, _rows(ys, pos[:, 0]), _rows(ys, pos[:, 1]), rec,
                                   mods1, 5, l1_ln2_g, l1_ln2_b)
    return (y_prompt.reshape(BATCH, SEQ, D_MODEL), y_sample.reshape(DEC_BATCH, DEC_SEQ, D_MODEL),
            new_k.reshape(BATCH, SEQ, N_KV_HEADS, HEAD_DIM),
            new_v.reshape(BATCH, SEQ, N_KV_HEADS, HEAD_DIM),
            new_h)
```

```python
import functools
import math

import jax
import jax.numpy as jnp
from jax import lax
from jax.experimental import pallas as pl
from jax.experimental.pallas import tpu as pltpu

F32 = jnp.float32
BF16 = jnp.bfloat16

D_MODEL = 2048
BATCH = 16
SEQ = 256
DEC_BATCH = 2
DEC_SEQ = 1024
PAST_LEN = 256
GRID_W = 64
HEAD_DIM = 128
N_Q_HEADS = 8
N_KV_HEADS = 2
Q_GROUP = N_Q_HEADS // N_KV_HEADS
Q_WIDTH = N_Q_HEADS * HEAD_DIM
KV_WIDTH = N_KV_HEADS * HEAD_DIM
ROPE_THETA = 10000.0
LRU_WIDTH = D_MODEL // 2
LRU_BLOCK = 128
LRU_C = 8.0
MIX0_IN = Q_WIDTH + 2 * KV_WIDTH + 2 * LRU_WIDTH
HYENA_ORDER = 2
HYENA_EMB = 33
HYENA_BANDS = (HYENA_EMB - 1) // 2
HYENA_FILTER_HIDDEN = 64
D_FF = 5632
N_EXPERTS = 8
TOP_K = 2
D_FF_EXPERT = 7168
N_MOD = 6
LN_EPS = 1e-5
QK_EPS = 1e-6
DEPTH = 2
DN_ALPHA = (2 * DEPTH) ** 0.25

TOK_P = BATCH * SEQ
TOK_S = DEC_BATCH * DEC_SEQ
N_TOK = TOK_P + TOK_S
N_COND = 8
LANE = 128
SUBLANE = 8
VMEM_LIMIT = 56 * 1024 * 1024

ROW_TILE = 1024
MOE_CHUNK = 2048
MOE_ROW_TILE = 256
MOE_FF_TILE = 256
MOE_ITEMS = N_EXPERTS + (N_TOK * TOP_K) // MOE_CHUNK
MOE_ROWS = MOE_ITEMS * MOE_CHUNK


def _cparams(sem):
    return pltpu.CompilerParams(dimension_semantics=sem, vmem_limit_bytes=VMEM_LIMIT)


def _cond_of_tile(i, tm):
    return jnp.maximum(i * tm // DEC_SEQ - (TOK_P // DEC_SEQ - 1), 0)


def _mod_spec(tm, chunk, width=D_MODEL, col_of=None):
    per = D_MODEL // width
    if col_of is None:
        return pl.BlockSpec((1, 1, width), lambda i, *_: (_cond_of_tile(i, tm), 0, chunk * per))
    return pl.BlockSpec((1, 1, width),
                        lambda i, j, *_: (_cond_of_tile(i, tm), 0, chunk * per + col_of(j)))


def _silu(x):
    return x * jax.nn.sigmoid(x)


def _split_bf16(x):
    hi = x.astype(BF16)
    lo = (x - hi.astype(F32)).astype(BF16)
    return hi, lo


def _pack_bf16_pair(lo, hi):
    lo_bits = lax.bitcast_convert_type(lo.astype(BF16).astype(F32), jnp.uint32) >> 16
    hi_bits = lax.bitcast_convert_type(hi.astype(BF16).astype(F32), jnp.uint32) & jnp.uint32(0xFFFF0000)
    return hi_bits | lo_bits


def _unpack_bf16_pair(words):
    lo = lax.bitcast_convert_type(words << 16, F32)
    hi = lax.bitcast_convert_type(words & jnp.uint32(0xFFFF0000), F32)
    return jnp.concatenate([lo, hi], axis=1).astype(BF16)


def _rows(a, idx):
    return a.at[idx].get(mode="promise_in_bounds")


def _dot3(a, b):
    ah, al = _split_bf16(a)
    bh, bl = _split_bf16(b)
    d = lambda x, y: jnp.dot(x, y, preferred_element_type=F32)
    return d(ah, bh) + (d(ah, bl) + d(al, bh))


def _layer_norm_rows(y, g, b):
    mu = jnp.mean(y, -1, keepdims=True)
    yc = y - mu
    var = jnp.mean(yc * yc, -1, keepdims=True)
    return yc * lax.rsqrt(var + LN_EPS) * g + b


def _ada_kernel(c_ref, w_ref, b_ref, o_ref):
    s = _silu(c_ref[...]).astype(BF16)
    o_ref[...] = jnp.dot(s, w_ref[...].astype(BF16), preferred_element_type=F32) + b_ref[...]


def _ada(cond, w, b):
    tn = 1024
    n = w.shape[1]
    out = pl.pallas_call(
        _ada_kernel,
        grid=(n // tn,),
        in_specs=[pl.BlockSpec((N_COND, D_MODEL), lambda j: (0, 0)),
                  pl.BlockSpec((D_MODEL, tn), lambda j: (0, j)),
                  pl.BlockSpec((1, tn), lambda j: (0, j))],
        out_specs=pl.BlockSpec((N_COND, tn), lambda j: (0, j)),
        out_shape=jax.ShapeDtypeStruct((N_COND, n), F32),
        compiler_params=_cparams(("arbitrary",)),
        name="ada_modulation",
    )(cond, w, b.reshape(1, n))
    return out.reshape(N_COND, 1, n)


def _cast_kernel(x_ref, o_ref):
    o_ref[...] = x_ref[...].astype(o_ref.dtype)


def _cast_bf16(w, rows, name):
    r, c = w.shape
    return pl.pallas_call(
        _cast_kernel,
        grid=(r // rows,),
        in_specs=[pl.BlockSpec((rows, c), lambda i: (i, 0))],
        out_specs=pl.BlockSpec((rows, c), lambda i: (i, 0)),
        out_shape=jax.ShapeDtypeStruct((r, c), BF16),
        compiler_params=_cparams(("arbitrary",)),
        name=name,
    )(w)


def _token_row_specs(n_parts, tm, width, col=None, buffered_once=False):
    kw = dict(pipeline_mode=pl.Buffered(1)) if buffered_once else {}
    col = col or (lambda i, *a: 0)
    if n_parts == 1:
        return [pl.BlockSpec((tm, width), lambda i, *a: (i, col(i, *a)), **kw)]
    n_p = TOK_P // tm
    return [pl.BlockSpec((tm, width), lambda i, *a: (jnp.minimum(i, n_p - 1), col(i, *a)), **kw),
            pl.BlockSpec((tm, width), lambda i, *a: (jnp.maximum(i - n_p, 0), col(i, *a)), **kw)]


def _proj_kernel(*refs, n_x):
    x_refs = refs[:n_x]
    sh_ref, sc_ref, w_ref, o_ref, u_ref, wb_ref = refs[n_x:]
    i = pl.program_id(0)
    j = pl.program_id(1)

    def modulate_from(x_ref):
        u_ref[...] = (x_ref[...] * (1.0 + sc_ref[0]) + sh_ref[0]).astype(BF16)

    @pl.when(j == 0)
    def _():
        if n_x == 1:
            modulate_from(x_refs[0])
        else:
            n_p = TOK_P // u_ref.shape[0]
            pl.when(i < n_p)(lambda: modulate_from(x_refs[0]))
            pl.when(i >= n_p)(lambda: modulate_from(x_refs[1]))

    @pl.when(i == 0)
    def _():
        wb_ref[j] = w_ref[...].astype(BF16)

    o_ref[...] = jnp.dot(u_ref[...], wb_ref[j], preferred_element_type=F32)


def _first_pass_block(i, j, nj):
    return jnp.where(i == 0, j, nj - 1)


def _proj(xs, mods, w, *, tn, name):
    tm = ROW_TILE
    n = w.shape[1]
    nj = n // tn
    return pl.pallas_call(
        functools.partial(_proj_kernel, n_x=len(xs)),
        grid=(N_TOK // tm, nj),
        in_specs=_token_row_specs(len(xs), tm, D_MODEL, buffered_once=True) + [
            _mod_spec(tm, 0), _mod_spec(tm, 1),
            pl.BlockSpec((D_MODEL, tn), lambda i, j: (0, _first_pass_block(i, j, nj)))],
        out_specs=pl.BlockSpec((tm, tn), lambda i, j: (i, j)),
        out_shape=jax.ShapeDtypeStruct((N_TOK, n), F32),
        scratch_shapes=[pltpu.VMEM((tm, D_MODEL), BF16),
                        pltpu.VMEM((nj, D_MODEL, tn), BF16)],
        compiler_params=_cparams(("arbitrary", "arbitrary")),
        name=name,
    )(*xs, mods, mods, w)


def _rms(x, g):
    return x * lax.rsqrt(jnp.mean(x * x, -1, keepdims=True) + QK_EPS) * g


def _dot_nt(a, b):
    return lax.dot_general(a, b, (((1,), (1,)), ((), ())), preferred_element_type=F32)


def _rope(x, cos, sin_signed):
    lane = lax.broadcasted_iota(jnp.int32, x.shape, 1)
    partner = jnp.where(lane % 2 == 0, pltpu.roll(x, HEAD_DIM - 1, 1), pltpu.roll(x, 1, 1))
    return x * cos + partner * sin_signed


def _attn_ctx_kernel(q_ref, k_ref, v_ref, qn_ref, kn_ref, o_ref, ko_ref, vo_ref):
    scale = HEAD_DIM ** -0.5
    kn = _rms(k_ref[...], kn_ref[...])
    v = v_ref[...]
    ko_ref[...] = kn
    vo_ref[...] = v
    kb = kn.astype(BF16)
    vb = v.astype(BF16)
    for g in range(Q_GROUP):
        cols = slice(g * HEAD_DIM, (g + 1) * HEAD_DIM)
        q = _rms(q_ref[:, cols], qn_ref[...]).astype(BF16)
        s = _dot_nt(q, kb) * scale
        p = jnp.exp(s - jnp.max(s, -1, keepdims=True))
        p = p / jnp.sum(p, -1, keepdims=True)
        o = jnp.dot(p.astype(BF16), vb, preferred_element_type=F32)
        o_ref[:, cols] = o.astype(o_ref.dtype)


def _attn_context(proj, q_norm, k_norm):
    qw = Q_GROUP * HEAD_DIM
    k_blk0 = Q_WIDTH // HEAD_DIM
    v_blk0 = (Q_WIDTH + KV_WIDTH) // HEAD_DIM
    vec = pl.BlockSpec((1, HEAD_DIM), lambda b, h: (0, 0))
    return pl.pallas_call(
        _attn_ctx_kernel,
        grid=(BATCH, N_KV_HEADS),
        in_specs=[pl.BlockSpec((SEQ, qw), lambda b, h: (b, h)),
                  pl.BlockSpec((SEQ, HEAD_DIM), lambda b, h: (b, k_blk0 + h)),
                  pl.BlockSpec((SEQ, HEAD_DIM), lambda b, h: (b, v_blk0 + h)),
                  vec, vec],
        out_specs=[pl.BlockSpec((SEQ, qw), lambda b, h: (b, h)),
                   pl.BlockSpec((SEQ, HEAD_DIM), lambda b, h: (b, h)),
                   pl.BlockSpec((SEQ, HEAD_DIM), lambda b, h: (b, h))],
        out_shape=[jax.ShapeDtypeStruct((N_TOK, Q_WIDTH), BF16),
                   jax.ShapeDtypeStruct((TOK_P, KV_WIDTH), F32),
                   jax.ShapeDtypeStruct((TOK_P, KV_WIDTH), F32)],
        compiler_params=_cparams(("arbitrary", "arbitrary")),
        name="attn_context",
    )(proj, proj, proj, q_norm.reshape(1, HEAD_DIM), k_norm.reshape(1, HEAD_DIM))


ATTN_Q_ROWS = 256


def _attn_lat_kernel(q_ref, k_ref, v_ref, ck_ref, cv_ref, qn_ref, kn_ref,
                     cq_ref, sq_ref, ck_tab_ref, sk_tab_ref, dst_ref, o_ref):
    del dst_ref
    scale = HEAD_DIM ** -0.5
    kb = _rope(_rms(k_ref[...], kn_ref[...]), ck_tab_ref[...], sk_tab_ref[...]).astype(BF16)
    vb = v_ref[...].astype(BF16)
    ckb = ck_ref[...].astype(BF16)
    cvb = cv_ref[...].astype(BF16)
    for g in range(Q_GROUP):
        cols = slice(g * HEAD_DIM, (g + 1) * HEAD_DIM)
        q = _rope(_rms(q_ref[:, cols], qn_ref[...]), cq_ref[...], sq_ref[...]).astype(BF16)
        s1 = _dot_nt(q, ckb) * scale
        s2 = _dot_nt(q, kb) * scale
        m = jnp.maximum(jnp.max(s1, -1, keepdims=True), jnp.max(s2, -1, keepdims=True))
        p1 = jnp.exp(s1 - m)
        p2 = jnp.exp(s2 - m)
        den = jnp.sum(p1, -1, keepdims=True) + jnp.sum(p2, -1, keepdims=True)
        o = (jnp.dot((p1 / den).astype(BF16), cvb, preferred_element_type=F32)
             + jnp.dot((p2 / den).astype(BF16), vb, preferred_element_type=F32))
        o_ref[:, cols] = o.astype(o_ref.dtype)


def _rope_tables():
    t = jnp.arange(DEC_SEQ)
    row = (t // GRID_W).astype(F32)
    col = (t % GRID_W).astype(F32)
    n_freq = HEAD_DIM // 4
    inv = 1.0 / (ROPE_THETA ** (jnp.arange(n_freq, dtype=F32) / n_freq))
    ang = jnp.concatenate([row[:, None] * inv, col[:, None] * inv], -1)
    cos = jnp.repeat(jnp.cos(ang), 2, axis=-1)
    sign = jnp.where(jnp.arange(HEAD_DIM) % 2 == 0, -1.0, 1.0).astype(F32)
    sin_signed = jnp.repeat(jnp.sin(ang), 2, axis=-1) * sign
    return cos, sin_signed


_IN_PLACE = pl.BlockSpec(memory_space=pl.ANY)


def _attn_latent(proj, cache_k, cache_v, q_norm, k_norm, dst):
    qw = Q_GROUP * HEAD_DIM
    nq = DEC_SEQ // ATTN_Q_ROWS
    q_row0 = TOK_P // ATTN_Q_ROWS
    kv_row0 = TOK_P // DEC_SEQ
    k_blk0 = Q_WIDTH // HEAD_DIM
    v_blk0 = (Q_WIDTH + KV_WIDTH) // HEAD_DIM
    cos, sin_signed = _rope_tables()
    vec = pl.BlockSpec((1, HEAD_DIM), lambda b, h, c: (0, 0))
    tab_q = pl.BlockSpec((ATTN_Q_ROWS, HEAD_DIM), lambda b, h, c: (c, 0))
    tab_k = pl.BlockSpec((DEC_SEQ, HEAD_DIM), lambda b, h, c: (0, 0))
    ctx = pl.BlockSpec((PAST_LEN, HEAD_DIM), lambda b, h, c: (b, h))
    return pl.pallas_call(
        _attn_lat_kernel,
        grid=(DEC_BATCH, N_KV_HEADS, nq),
        in_specs=[pl.BlockSpec((ATTN_Q_ROWS, qw), lambda b, h, c: (q_row0 + b * nq + c, h)),
                  pl.BlockSpec((DEC_SEQ, HEAD_DIM), lambda b, h, c: (kv_row0 + b, k_blk0 + h)),
                  pl.BlockSpec((DEC_SEQ, HEAD_DIM), lambda b, h, c: (kv_row0 + b, v_blk0 + h)),
                  ctx, ctx, vec, vec, tab_q, tab_q, tab_k, tab_k, _IN_PLACE],
        out_specs=pl.BlockSpec((ATTN_Q_ROWS, qw), lambda b, h, c: (q_row0 + b * nq + c, h)),
        out_shape=jax.ShapeDtypeStruct((N_TOK, Q_WIDTH), BF16),
        input_output_aliases={11: 0},
        compiler_params=_cparams(("arbitrary", "arbitrary", "arbitrary")),
        name="attn_latent",
    )(proj, proj, proj,
      cache_k.reshape(DEC_BATCH * PAST_LEN, KV_WIDTH), cache_v.reshape(DEC_BATCH * PAST_LEN, KV_WIDTH),
      q_norm.reshape(1, HEAD_DIM), k_norm.reshape(1, HEAD_DIM), cos, sin_signed, cos, sin_signed, dst)


LRU_CT = 512
assert (Q_WIDTH + 2 * KV_WIDTH) % LRU_CT == 0 and LRU_WIDTH % LRU_CT == 0


def _sigmoid_tanh(x):
    return 0.5 * (jnp.tanh(0.5 * x) + 1.0)
LRU_RC = 128


def _softplus(x):
    return jnp.maximum(x, 0.0) + jnp.log1p(jnp.exp(-jnp.abs(x)))


def _gelu_tanh(x):
    return 0.5 * x * (1.0 + jnp.tanh(math.sqrt(2.0 / math.pi) * (x + 0.044715 * (x * x * x))))


def _lru_kernel(*refs, seq, has_dst):
    xb_ref, gb_ref, cw_ref, cb_ref, lam_ref, wr_ref, br_ref, wi_ref, bi_ref, h0_ref = refs[:10]
    y_ref, hfin_ref, xpad_ref, a_ref, b_ref, hf_ref, hb_ref = refs[10 + int(has_dst):]
    ct = LRU_CT
    zeros = jnp.zeros((SUBLANE, ct), F32)
    xpad_ref[0:SUBLANE, :] = zeros
    xpad_ref[seq + SUBLANE:seq + 2 * SUBLANE, :] = zeros
    xpad_ref[SUBLANE:seq + SUBLANE, :] = xb_ref[...]
    cw = cw_ref[...]
    sp = _softplus(-lam_ref[...])
    for r0 in range(0, seq, LRU_RC):
        xc = cb_ref[...] + cw[0:1, :] * xpad_ref[r0 + 7:r0 + 7 + LRU_RC, :]
        for w in range(1, 4):
            xc = xc + cw[w:w + 1, :] * xpad_ref[r0 + 7 + w:r0 + 7 + w + LRU_RC, :]
        for kb in range(ct // LRU_BLOCK):
            cols = slice(kb * LRU_BLOCK, (kb + 1) * LRU_BLOCK)
            xk = xc[:, cols]
            xkb = xk.astype(BF16)
            for d in range(2):
                r = _sigmoid_tanh(jnp.dot(xkb, wr_ref[d, kb].astype(BF16), preferred_element_type=F32)
                                  + br_ref[d:d + 1, cols])
                i = _sigmoid_tanh(jnp.dot(xkb, wi_ref[d, kb].astype(BF16), preferred_element_type=F32)
                                  + bi_ref[d:d + 1, cols])
                log_a = -LRU_C * r * sp[d:d + 1, cols]
                a = jnp.exp(log_a)
                a_ref[d, r0:r0 + LRU_RC, cols] = a
                b_ref[d, r0:r0 + LRU_RC, cols] = jnp.sqrt(-jnp.tanh(log_a) * (a * a + 1.0)) * (i * xk)

    nblk = seq // SUBLANE
    row = lax.broadcasted_iota(jnp.int32, (SUBLANE, ct), 0)

    def body(i, carry):
        hf, hb = carry
        rf = pl.multiple_of(i * SUBLANE, SUBLANE)
        a = a_ref[0, pl.ds(rf, SUBLANE), :]
        b = b_ref[0, pl.ds(rf, SUBLANE), :]
        for s in (1, 2, 4):
            m = row >= s
            a_s = jnp.where(m, pltpu.roll(a, s, 0), 1.0)
            b_s = jnp.where(m, pltpu.roll(b, s, 0), 0.0)
            b = a * b_s + b
            a = a * a_s
        hblk = a * hf + b
        hf_ref[pl.ds(rf, SUBLANE), :] = hblk
        hf = hblk[SUBLANE - 1:SUBLANE, :]

        rb = pl.multiple_of((nblk - 1 - i) * SUBLANE, SUBLANE)
        a = a_ref[1, pl.ds(rb, SUBLANE), :]
        b = b_ref[1, pl.ds(rb, SUBLANE), :]
        for s in (1, 2, 4):
            m = row < SUBLANE - s
            a_s = jnp.where(m, pltpu.roll(a, SUBLANE - s, 0), 1.0)
            b_s = jnp.where(m, pltpu.roll(b, SUBLANE - s, 0), 0.0)
            b = a * b_s + b
            a = a * a_s
        hblk = a * hb + b
        hb_ref[pl.ds(rb, SUBLANE), :] = hblk
        hb = hblk[0:1, :]
        return hf, hb

    hf, hb = lax.fori_loop(0, nblk, body, (h0_ref[0:1, :], h0_ref[1:2, :]))
    hfin_ref[0:1, :] = hf
    hfin_ref[1:2, :] = hb
    for r0 in range(0, seq, LRU_RC):
        rows = slice(r0, r0 + LRU_RC)
        y = (hf_ref[rows, :] + hb_ref[rows, :]) * _gelu_tanh(gb_ref[rows, :])
        y_ref[rows, :] = y.astype(y_ref.dtype)


def _lru(proj, h0, p, *, seq, batch, row0, name, dst=None):
    ct = LRU_CT
    nkb = ct // LRU_BLOCK
    xb_blk0 = (Q_WIDTH + 2 * KV_WIDTH) // ct
    gb_blk0 = (Q_WIDTH + 2 * KV_WIDTH + LRU_WIDTH) // ct
    r0 = row0 // seq
    vec2 = pl.BlockSpec((2, ct), lambda b, c: (0, c))
    wblk = pl.BlockSpec((2, nkb, LRU_BLOCK, LRU_BLOCK), lambda b, c: (0, c, 0, 0))
    extra_specs, extra_args, aliases = [], [], {}
    if dst is not None:
        extra_specs, extra_args, aliases = [_IN_PLACE], [dst], {10: 0}
    return pl.pallas_call(
        functools.partial(_lru_kernel, seq=seq, has_dst=dst is not None),
        grid=(batch, LRU_WIDTH // ct),
        in_specs=[pl.BlockSpec((seq, ct), lambda b, c: (r0 + b, xb_blk0 + c)),
                  pl.BlockSpec((seq, ct), lambda b, c: (r0 + b, gb_blk0 + c)),
                  pl.BlockSpec((4, ct), lambda b, c: (0, c)),
                  pl.BlockSpec((1, ct), lambda b, c: (0, c)),
                  vec2, wblk, vec2, wblk, vec2,
                  pl.BlockSpec((None, 2, ct), lambda b, c: (b, 0, c))] + extra_specs,
        out_specs=[pl.BlockSpec((seq, ct), lambda b, c: (r0 + b, c)),
                   pl.BlockSpec((None, 2, ct), lambda b, c: (b, 0, c))],
        out_shape=[jax.ShapeDtypeStruct((N_TOK, LRU_WIDTH), BF16),
                   jax.ShapeDtypeStruct((batch, 2, LRU_WIDTH), F32)],
        input_output_aliases=aliases,
        scratch_shapes=[pltpu.VMEM((seq + 2 * SUBLANE, ct), F32),
                        pltpu.VMEM((2, seq, ct), F32),
                        pltpu.VMEM((2, seq, ct), F32),
                        pltpu.VMEM((seq, ct), F32),
                        pltpu.VMEM((seq, ct), F32)],
        compiler_params=_cparams(("arbitrary", "arbitrary")),
        name=name,
    )(proj, proj, p['conv_w'], p['conv_b'].reshape(1, LRU_WIDTH), p['lam'],
      p['w_r'], p['b_r'], p['w_i'], p['b_i'], h0, *extra_args)


OUT_TN = 1024
OUT_TM = 512


def _outproj_kernel(*refs, n_a, n_res, with_router):
    a_refs = refs[:n_a]
    w_refs = refs[n_a:2 * n_a]
    res_refs = refs[2 * n_a:2 * n_a + n_res]
    gate_ref, g_ref, b_ref = refs[2 * n_a + n_res:2 * n_a + n_res + 3]
    pos = 2 * n_a + n_res + 3
    if with_router:
        sh_ref, sc_ref, rt_ref = refs[pos:pos + 3]
        pos += 3
        o_ref, u_ref, lg_ref, acc_ref = refs[pos:pos + 4]
        pos += 4
    else:
        o_ref, acc_ref = refs[pos:pos + 2]
        pos += 2
    wb_refs = refs[pos:pos + n_a]
    i = pl.program_id(0)
    j = pl.program_id(1)
    nj = pl.num_programs(1)

    @pl.when(i == 0)
    def _():
        for k in range(n_a):
            wb_refs[k][j] = w_refs[k][...].astype(BF16)

    acc = jnp.dot(a_refs[0][...], wb_refs[0][j], preferred_element_type=F32)
    for k in range(1, n_a):
        acc = acc + jnp.dot(a_refs[k][...], wb_refs[k][j], preferred_element_type=F32)
    if n_res == 1:
        res = res_refs[0][...]
    else:
        n_p = TOK_P // acc_ref.shape[1]
        res = jnp.where(i < n_p, res_refs[0][...], res_refs[1][...])
    acc_ref[j] = DN_ALPHA * res + gate_ref[0] * acc

    @pl.when(j == nj - 1)
    def _():
        tm = acc_ref.shape[1]
        n_chunks = acc_ref.shape[0]
        rc = 128

        def chunk(c, _):
            r0 = pl.multiple_of(c * rc, rc)
            ys = [acc_ref[k, pl.ds(r0, rc), :] for k in range(n_chunks)]
            tot = ys[0].sum(-1, keepdims=True)
            for y in ys[1:]:
                tot = tot + y.sum(-1, keepdims=True)
            mu = tot / D_MODEL
            sq = None
            for y in ys:
                t = ((y - mu) * (y - mu)).sum(-1, keepdims=True)
                sq = t if sq is None else sq + t
            rstd = lax.rsqrt(sq / D_MODEL + LN_EPS)
            lg = None
            us = []
            for k, y in enumerate(ys):
                cols = slice(k * OUT_TN, (k + 1) * OUT_TN)
                xn = (y - mu) * rstd * g_ref[:, cols] + b_ref[:, cols]
                o_ref[pl.ds(r0, rc), cols] = xn
                if with_router:
                    u = xn * (1.0 + sc_ref[0][:, cols]) + sh_ref[0][:, cols]
                    us.append(u)
                    t = _dot3(u, rt_ref[cols, :])
                    lg = t if lg is None else lg + t
            if with_router:
                lg_ref[pl.ds(r0, rc), :] = lg
                half = n_chunks // 2
                for k in range(half):
                    cols = slice(k * OUT_TN, (k + 1) * OUT_TN)
                    u_ref[pl.ds(r0, rc), cols] = _pack_bf16_pair(us[k], us[k + half])
            return 0

        lax.fori_loop(0, tm // rc, chunk, 0)


def _outproj(a_list, w, res, mods, gate_chunk, g, b, *, name, router=None, router_chunks=None):
    tm = OUT_TM
    tn = OUT_TN
    nj = D_MODEL // tn
    n_a = len(a_list)
    in_specs = []
    k0 = 0
    w_specs = []
    wb_shapes = []
    for a in a_list:
        ka = a.shape[1]
        in_specs.append(pl.BlockSpec((tm, ka), lambda i, j: (i, 0)))
        w_specs.append(pl.BlockSpec((ka, tn),
                                    lambda i, j, blk=k0 // ka: (blk, _first_pass_block(i, j, nj))))
        wb_shapes.append(pltpu.VMEM((nj, ka, tn), BF16))
        k0 += ka
    in_specs += w_specs
    in_specs += _token_row_specs(len(res), tm, tn, col=lambda i, j: j)
    in_specs += [_mod_spec(tm, gate_chunk, tn, col_of=lambda j: j),
                 pl.BlockSpec((1, D_MODEL), lambda i, j: (0, 0)),
                 pl.BlockSpec((1, D_MODEL), lambda i, j: (0, 0))]
    args = list(a_list) + [w] * n_a + list(res) + [mods, g.reshape(1, D_MODEL), b.reshape(1, D_MODEL)]
    out_specs = [pl.BlockSpec((tm, D_MODEL), lambda i, j: (i, 0))]
    out_shape = [jax.ShapeDtypeStruct((N_TOK, D_MODEL), F32)]
    if router is not None:
        in_specs += [_mod_spec(tm, router_chunks[0]), _mod_spec(tm, router_chunks[1]),
                     pl.BlockSpec((D_MODEL, LANE), lambda i, j: (0, 0))]
        args += [mods, mods, router]
        out_specs += [pl.BlockSpec((tm, D_MODEL // 2), lambda i, j: (i, 0)),
                      pl.BlockSpec((tm, LANE), lambda i, j: (i, 0))]
        out_shape += [jax.ShapeDtypeStruct((N_TOK, D_MODEL // 2), jnp.uint32),
                      jax.ShapeDtypeStruct((N_TOK, LANE), F32)]
    return pl.pallas_call(
        functools.partial(_outproj_kernel, n_a=n_a, n_res=len(res), with_router=router is not None),
        grid=(N_TOK // tm, nj),
        in_specs=in_specs,
        out_specs=out_specs,
        out_shape=out_shape,
        scratch_shapes=[pltpu.VMEM((nj, tm, tn), F32)] + wb_shapes,
        compiler_params=_cparams(("arbitrary", "arbitrary")),
        name=name,
    )(*args)


FFN_TF = 512


def _ffn_kernel(x_ref, sh_ref, sc_ref, gate_ref, g_ref, b_ref, w1_ref, w3_ref, w2_ref, o_ref, u_ref):
    j = pl.program_id(1)
    nj = pl.num_programs(1)

    @pl.when(j == 0)
    def _():
        u_ref[...] = (x_ref[...] * (1.0 + sc_ref[0]) + sh_ref[0]).astype(BF16)
        o_ref[...] = jnp.zeros(o_ref.shape, F32)

    u = u_ref[...]
    h1 = jnp.dot(u, w1_ref[...], preferred_element_type=F32)
    h3 = jnp.dot(u, w3_ref[...], preferred_element_type=F32)
    h = (_silu(h1) * h3).astype(BF16)
    o_ref[...] += jnp.dot(h, w2_ref[...], preferred_element_type=F32)

    @pl.when(j == nj - 1)
    def _():
        rc = 128

        def chunk(c, _):
            r0 = pl.multiple_of(c * rc, rc)
            y = DN_ALPHA * x_ref[pl.ds(r0, rc), :] + gate_ref[0] * o_ref[pl.ds(r0, rc), :]
            o_ref[pl.ds(r0, rc), :] = _layer_norm_rows(y, g_ref[...], b_ref[...])
            return 0

        lax.fori_loop(0, o_ref.shape[0] // rc, chunk, 0)


def _ffn(x, mods, p):
    tm = ROW_TILE
    tf = FFN_TF
    vec = pl.BlockSpec((1, D_MODEL), lambda i, j: (0, 0))
    return pl.pallas_call(
        _ffn_kernel,
        grid=(N_TOK // tm, D_FF // tf),
        in_specs=[pl.BlockSpec((tm, D_MODEL), lambda i, j: (i, 0), pipeline_mode=pl.Buffered(1)),
                  _mod_spec(tm, 3), _mod_spec(tm, 4), _mod_spec(tm, 5), vec, vec,
                  pl.BlockSpec((D_MODEL, tf), lambda i, j: (0, j)),
                  pl.BlockSpec((D_MODEL, tf), lambda i, j: (0, j)),
                  pl.BlockSpec((tf, D_MODEL), lambda i, j: (j, 0))],
        out_specs=pl.BlockSpec((tm, D_MODEL), lambda i, j: (i, 0)),
        out_shape=jax.ShapeDtypeStruct((N_TOK, D_MODEL), F32),
        scratch_shapes=[pltpu.VMEM((tm, D_MODEL), BF16)],
        compiler_params=_cparams(("arbitrary", "arbitrary")),
        name="ffn_swiglu",
    )(x, mods, mods, mods, p['ln2_g'].reshape(1, D_MODEL), p['ln2_b'].reshape(1, D_MODEL),
      _cast_bf16(p['ffn_w1'], 256, "ffn_w1_bf16"), _cast_bf16(p['ffn_w3'], 256, "ffn_w3_bf16"),
      _cast_bf16(p['ffn_w2'], 512, "ffn_w2_bf16"))


def _filter_mlp_kernel(feat_ref, w1_ref, b1_ref, f1_ref, w2_ref, b2_ref, f2_ref, o_ref):
    h = jnp.sin(f1_ref[...] * (_dot3(feat_ref[...], w1_ref[...]) + b1_ref[...]))
    o_ref[...] = jnp.sin(f2_ref[...] * (_dot3(h, w2_ref[...]) + b2_ref[...]))


def _filter_features(seq):
    t = jnp.arange(seq, dtype=F32)
    t01 = t / (seq - 1)
    w = 2.0 * math.pi * t / seq
    f = jnp.linspace(1e-4, HYENA_BANDS - 1, HYENA_BANDS, dtype=F32)
    fw = w[:, None] * f[None, :]
    feat = jnp.concatenate([t01[:, None], jnp.cos(fw), -jnp.sin(fw)], -1)
    return jnp.pad(feat, ((0, 0), (0, LANE - HYENA_EMB))), t01[:, None]


def _filter_mlp(seq, p):
    feat, t01 = _filter_features(seq)
    hid = HYENA_FILTER_HIDDEN
    w1 = jnp.pad(p['filt_w1'], ((0, LANE - HYENA_EMB), (0, 0)))
    row = lambda v: v.reshape(1, hid)
    h2 = pl.pallas_call(
        _filter_mlp_kernel,
        out_shape=jax.ShapeDtypeStruct((seq, hid), F32),
        compiler_params=pltpu.CompilerParams(vmem_limit_bytes=VMEM_LIMIT),
        name=f"hyena_filter_mlp_{seq}",
    )(feat, w1, row(p['filt_b1']), row(p['filt_f1']), p['filt_w2'], row(p['filt_b2']), row(p['filt_f2']))
    return h2, t01


def _dft_matrices(seq):
    n = 2 * seq
    f = jnp.arange(seq, dtype=jnp.int32)

    def table(freqs):
        ang = ((freqs[:, None] * f[None, :]) % n).astype(F32) * (math.pi / seq)
        return jnp.cos(ang), jnp.sin(ang)

    step = 32
    ca, sa = table(jnp.arange(seq // step, dtype=jnp.int32) * step)
    cb, sb = table(jnp.arange(step, dtype=jnp.int32))
    c = (ca[:, None, :] * cb[None, :, :] - sa[:, None, :] * sb[None, :, :]).reshape(seq, seq)
    s = -(sa[:, None, :] * cb[None, :, :] + ca[:, None, :] * sb[None, :, :]).reshape(seq, seq)
    alt = jnp.where(f % 2 == 0, 1.0, -1.0).astype(F32)
    s_fwd = s.at[0, :].set(alt)
    s_inv = s.at[:, 0].set(alt)
    fwd = jnp.concatenate([c, s_fwd], axis=0).astype(BF16)
    inv = jnp.concatenate([c, s_inv], axis=1).astype(BF16)
    return fwd, inv, alt[:, None]


HY_RC = 128


def _hyena_kernel(*refs, seq, nb, ct, has_dst):
    (x1_ref, x2_ref, z_ref, sw1_ref, sw2_ref, swz_ref, sb1_ref, sb2_ref, sbz_ref,
     h2_ref, t01_ref, alt_ref, dec_ref, bias_ref, fwd_ref, inv_ref) = refs[:16]
    w3_ref = refs[16:16 + 2 * HYENA_ORDER]
    n_in = 16 + 2 * HYENA_ORDER + int(has_dst)
    (o_ref, kr_ref, ki_ref, zb_ref, zf_ref, yb_ref, pad_ref, g1_ref, g2_ref,
     zc_ref) = refs[n_in:]
    nrc = seq // HY_RC

    @pl.when(pl.program_id(1) == 0)
    def _():
        t01 = t01_ref[...]
        h2 = h2_ref[...]
        row0 = lax.broadcasted_iota(jnp.int32, (seq, ct), 0) == 0
        wf = jnp.where(row0, 1.0, 2.0) / (2.0 * seq)
        for n in range(HYENA_ORDER):
            kpos = _dot3(h2, w3_ref[n][...]) * jnp.exp(-t01 * jnp.abs(dec_ref[n:n + 1, :]))
            kneg = _dot3(h2, w3_ref[HYENA_ORDER + n][...]) * jnp.exp(
                -t01 * jnp.abs(dec_ref[HYENA_ORDER + n:HYENA_ORDER + n + 1, :]))
            kneg = jnp.where(row0, 0.0, kneg)
            ksum = kpos + kneg
            kdif = kpos - kneg
            kr = jnp.dot(fwd_ref[0:seq, :], ksum.astype(BF16), preferred_element_type=F32)
            ki = jnp.dot(fwd_ref[seq:2 * seq, :], kdif.astype(BF16), preferred_element_type=F32)
            nyq = jnp.sum(alt_ref[...] * ksum, axis=0, keepdims=True) / (2.0 * seq)
            kr_ref[n] = kr * wf
            ki_ref[n] = jnp.where(row0, nyq, ki * wf)

    zeros = jnp.zeros((SUBLANE, ct), F32)
    pad_ref[0:SUBLANE, :] = zeros
    pad_ref[seq + SUBLANE:seq + 2 * SUBLANE, :] = zeros

    def short_conv(src_ref, rows0, w_ref, b_ref, dst_ref):
        pad_ref[SUBLANE:seq + SUBLANE, :] = src_ref[rows0:rows0 + seq, :]
        w = w_ref[...]
        for c in range(nrc):
            r0 = c * HY_RC
            acc = b_ref[...] + w[0:1, :] * pad_ref[r0 + 7:r0 + 7 + HY_RC, :]
            acc = acc + w[1:2, :] * pad_ref[r0 + 8:r0 + 8 + HY_RC, :]
            acc = acc + w[2:3, :] * pad_ref[r0 + 9:r0 + 9 + HY_RC, :]
            dst_ref[r0:r0 + HY_RC, :] = acc

    for bi in range(nb):
        rows0 = bi * seq
        short_conv(x1_ref, rows0, sw1_ref, sb1_ref, g1_ref)
        short_conv(x2_ref, rows0, sw2_ref, sb2_ref, g2_ref)
        short_conv(z_ref, rows0, swz_ref, sbz_ref, zc_ref)
        for n, gate_ref in enumerate((g1_ref, g2_ref)):
            for c in range(nrc):
                rows = slice(c * HY_RC, (c + 1) * HY_RC)
                zb_ref[rows, :] = zc_ref[rows, :].astype(BF16)
            zf_ref[...] = jnp.dot(fwd_ref[...], zb_ref[...], preferred_element_type=F32)
            for c in range(nrc):
                rows = slice(c * HY_RC, (c + 1) * HY_RC)
                rows_i = slice(seq + c * HY_RC, seq + (c + 1) * HY_RC)
                zr = zf_ref[rows, :]
                zi = zf_ref[rows_i, :]
                kr = kr_ref[n, rows, :]
                ki = ki_ref[n, rows, :]
                yr = zr * kr - zi * ki
                yi = zr * ki + zi * kr
                if c == 0:
                    first = lax.broadcasted_iota(jnp.int32, (HY_RC, ct), 0) == 0
                    yr = jnp.where(first, zr * kr, yr)
                    yi = jnp.where(first, zi * ki, yi)
                yb_ref[rows, :] = yr.astype(BF16)
                yb_ref[rows_i, :] = yi.astype(BF16)
            zf_ref[0:seq, :] = jnp.dot(inv_ref[...], yb_ref[...], preferred_element_type=F32)
            for c in range(nrc):
                rows = slice(c * HY_RC, (c + 1) * HY_RC)
                zc = zc_ref[rows, :]
                znew = gate_ref[rows, :] * (zf_ref[rows, :] + zc * bias_ref[n:n + 1, :])
                if n == HYENA_ORDER - 1:
                    o_ref[rows0 + c * HY_RC:rows0 + (c + 1) * HY_RC, :] = znew.astype(o_ref.dtype)
                else:
                    zc_ref[rows, :] = znew


def _hyena(proj, p, *, seq, batch, row0, nb, ct, name, dst=None):
    nct = D_MODEL // ct
    extra_specs, extra_args, aliases = [], [], {}
    if dst is not None:
        extra_specs, extra_args, aliases = [_IN_PLACE], [dst], {16 + 2 * HYENA_ORDER: 0}
    h2, t01 = _filter_mlp(seq, p)
    fwd, inv, alt = _dft_matrices(seq)
    rb0 = row0 // (nb * seq)
    w3 = p['filt_w3']
    w3k = lambda k: pl.BlockSpec((HYENA_FILTER_HIDDEN, ct), lambda c, b: (0, k * nct + c))
    dec = p['filt_decay'].reshape(2 * HYENA_ORDER, D_MODEL)
    sw = p['short_w']
    sb = p['short_b'].reshape(1, 3 * D_MODEL)
    slab = lambda k: pl.BlockSpec((nb * seq, ct), lambda c, b: (rb0 + b, k * nct + c))
    swk = lambda k: pl.BlockSpec((3, ct), lambda c, b: (0, k * nct + c))
    sbk = lambda k: pl.BlockSpec((1, ct), lambda c, b: (0, k * nct + c))
    const = lambda shape: pl.BlockSpec(shape, lambda c, b: tuple(0 for _ in shape))
    return pl.pallas_call(
        functools.partial(_hyena_kernel, seq=seq, nb=nb, ct=ct, has_dst=dst is not None),
        grid=(nct, batch // nb),
        in_specs=[slab(0), slab(1), slab(2), swk(0), swk(1), swk(2), sbk(0), sbk(1), sbk(2),
                  const((seq, HYENA_FILTER_HIDDEN)), const((seq, 1)), const((seq, 1)),
                  pl.BlockSpec((2 * HYENA_ORDER, ct), lambda c, b: (0, c)),
                  pl.BlockSpec((HYENA_ORDER, ct), lambda c, b: (0, c)),
                  pl.BlockSpec((2 * seq, seq), lambda c, b: (0, 0), pipeline_mode=pl.Buffered(1)),
                  pl.BlockSpec((seq, 2 * seq), lambda c, b: (0, 0), pipeline_mode=pl.Buffered(1))]
        + [w3k(k) for k in range(2 * HYENA_ORDER)] + extra_specs,
        out_specs=pl.BlockSpec((nb * seq, ct), lambda c, b: (rb0 + b, c)),
        out_shape=jax.ShapeDtypeStruct((N_TOK, D_MODEL), BF16),
        input_output_aliases=aliases,
        scratch_shapes=[pltpu.VMEM((HYENA_ORDER, seq, ct), F32),
                        pltpu.VMEM((HYENA_ORDER, seq, ct), F32),
                        pltpu.VMEM((seq, ct), BF16),
                        pltpu.VMEM((2 * seq, ct), F32),
                        pltpu.VMEM((2 * seq, ct), BF16),
                        pltpu.VMEM((seq + 2 * SUBLANE, ct), F32),
                        pltpu.VMEM((seq, ct), F32),
                        pltpu.VMEM((seq, ct), F32),
                        pltpu.VMEM((seq, ct), F32)],
        compiler_params=_cparams(("arbitrary", "arbitrary")),
        name=name,
    )(proj, proj, proj, sw, sw, sw, sb, sb, sb, h2, t01, alt, dec, p['filt_bias'], fwd, inv,
      *([w3] * (2 * HYENA_ORDER)), *extra_args)


def _moe_kernel(ie_ref, ib_ref, ir_ref, x_ref, w1_ref, w3_ref, w2_ref, o_hbm,
                acc_ref, w1b_ref, w3b_ref, w2b_ref, sem):
    w = pl.program_id(0)
    j = pl.program_id(1)
    nj = pl.num_programs(1)
    rows = ir_ref[w]
    ntiles = (rows + (MOE_ROW_TILE - 1)) // MOE_ROW_TILE
    rt = MOE_ROW_TILE

    def partial_out(r):
        off = r * rt if isinstance(r, int) else pl.multiple_of(r * rt, rt)
        xt = _unpack_bf16_pair(x_ref[pl.ds(off, rt), :])
        h1 = jnp.dot(xt, w1b_ref[...], preferred_element_type=F32)
        h3 = jnp.dot(xt, w3b_ref[...], preferred_element_type=F32)
        h = (_silu(h1) * h3).astype(BF16)
        return off, jnp.dot(h, w2b_ref[...], preferred_element_type=F32)

    def cast_weights():
        w1b_ref[...] = w1_ref[...].astype(BF16)
        w3b_ref[...] = w3_ref[...].astype(BF16)
        w2b_ref[...] = w2_ref[...].astype(BF16)

    @pl.when(rows > 0)
    def _():
        def run_tiles(first):
            def one(r):
                off, part = partial_out(r)
                if first:
                    acc_ref[pl.ds(off, rt), :] = part
                else:
                    acc_ref[pl.ds(off, rt), :] += part

            def pair(q, _):
                one(2 * q)
                one(2 * q + 1)
                return 0

            @pl.when(ntiles >= 2)
            def _():
                cast_weights()
                one(0)
                one(1)
                lax.fori_loop(1, ntiles // 2, pair, 0)

                @pl.when(ntiles % 2 == 1)
                def _():
                    one(ntiles - 1)

            @pl.when(ntiles == 1)
            def _():
                cast_weights()
                one(0)

        @pl.when(j == 0)
        def _():
            run_tiles(True)

        @pl.when(j > 0)
        def _():
            run_tiles(False)

        @pl.when(j == nj - 1)
        def _():
            base = ib_ref[w] * MOE_CHUNK

            def tile_copy(off):
                return pltpu.make_async_copy(acc_ref.at[pl.ds(off, rt)],
                                             o_hbm.at[pl.ds(base + off, rt)], sem)

            def start(r, _):
                tile_copy(pl.multiple_of(r * rt, rt)).start()
                return 0

            def wait(r, _):
                tile_copy(pl.multiple_of(r * rt, rt)).wait()
                return 0

            lax.fori_loop(0, ntiles, start, 0)
            lax.fori_loop(0, ntiles, wait, 0)


def _moe_experts(xs, item_expert, item_block, item_rows, p):
    tj = MOE_FF_TILE
    nj = D_FF_EXPERT // tj

    def jeff(w, j, ir):
        return jnp.where(ir[w] > 0, j, nj - 1)

    grid_spec = pltpu.PrefetchScalarGridSpec(
        num_scalar_prefetch=3,
        grid=(MOE_ITEMS, nj),
        in_specs=[pl.BlockSpec((MOE_CHUNK, D_MODEL // 2), lambda w, j, ie, ib, ir: (ib[w], 0),
                               pipeline_mode=pl.Buffered(1)),
                  pl.BlockSpec((None, D_MODEL, tj), lambda w, j, ie, ib, ir: (ie[w], 0, jeff(w, j, ir))),
                  pl.BlockSpec((None, D_MODEL, tj), lambda w, j, ie, ib, ir: (ie[w], 0, jeff(w, j, ir))),
                  pl.BlockSpec((None, tj, D_MODEL), lambda w, j, ie, ib, ir: (ie[w], jeff(w, j, ir), 0))],
        out_specs=pl.BlockSpec(memory_space=pl.ANY),
        scratch_shapes=[pltpu.VMEM((MOE_CHUNK, D_MODEL), F32),
                        pltpu.VMEM((D_MODEL, tj), BF16),
                        pltpu.VMEM((D_MODEL, tj), BF16),
                        pltpu.VMEM((tj, D_MODEL), BF16),
                        pltpu.SemaphoreType.DMA(())],
    )
    return pl.pallas_call(
        _moe_kernel,
        grid_spec=grid_spec,
        out_shape=jax.ShapeDtypeStruct((MOE_ROWS, D_MODEL), F32),
        compiler_params=_cparams(("arbitrary", "arbitrary")),
        name="moe_experts",
    )(item_expert, item_block, item_rows, xs, p['exp_w1'], p['exp_w3'], p['exp_w2'])


ROUTE_TM = 256
R_E0, R_E1, R_P0, R_P1, R_RANK0, R_RANK1 = range(6)


def _route_kernel(lg_ref, o_ref, cnt_ref, carry_ref):
    i = pl.program_id(0)
    tm = ROUTE_TM

    @pl.when(i == 0)
    def _():
        carry_ref[...] = jnp.zeros(carry_ref.shape, F32)

    lane = lax.broadcasted_iota(jnp.int32, (tm, LANE), 1)
    lg = jnp.where(lane < N_EXPERTS, lg_ref[...], -jnp.inf)
    m0 = jnp.max(lg, -1, keepdims=True)
    e0 = jnp.min(jnp.where(lg == m0, lane, LANE), -1, keepdims=True)
    lg1 = jnp.where(lane == e0, -jnp.inf, lg)
    m1 = jnp.max(lg1, -1, keepdims=True)
    e1 = jnp.min(jnp.where(lg1 == m1, lane, LANE), -1, keepdims=True)
    t = jnp.exp(m1 - m0)
    p0 = 1.0 / (1.0 + t)
    p1 = t / (1.0 + t)
    hit = ((lane == e0) | (lane == e1)).astype(BF16)
    r_i = lax.broadcasted_iota(jnp.int32, (tm, tm), 0)
    c_i = lax.broadcasted_iota(jnp.int32, (tm, tm), 1)
    before = (c_i < r_i).astype(BF16)
    pref = jnp.dot(before, hit, preferred_element_type=F32) + carry_ref[0:1, :]
    rank0 = jnp.sum(jnp.where(lane == e0, pref, 0.0), -1, keepdims=True)
    rank1 = jnp.sum(jnp.where(lane == e1, pref, 0.0), -1, keepdims=True)
    carry_ref[0:1, :] = carry_ref[0:1, :] + jnp.sum(hit.astype(F32), axis=0, keepdims=True)
    rec = jnp.zeros((tm, LANE), F32)
    for k, v in ((R_E0, e0.astype(F32)), (R_E1, e1.astype(F32)), (R_P0, p0), (R_P1, p1),
                 (R_RANK0, rank0), (R_RANK1, rank1)):
        rec = jnp.where(lane == k, v, rec)
    o_ref[...] = rec
    cnt_ref[...] = carry_ref[...]


def _route_records(logits):
    return pl.pallas_call(
        _route_kernel,
        grid=(N_TOK // ROUTE_TM,),
        in_specs=[pl.BlockSpec((ROUTE_TM, LANE), lambda i: (i, 0))],
        out_specs=[pl.BlockSpec((ROUTE_TM, LANE), lambda i: (i, 0)),
                   pl.BlockSpec((SUBLANE, LANE), lambda i: (0, 0))],
        out_shape=[jax.ShapeDtypeStruct((N_TOK, LANE), F32),
                   jax.ShapeDtypeStruct((SUBLANE, LANE), F32)],
        scratch_shapes=[pltpu.VMEM((SUBLANE, LANE), F32)],
        compiler_params=_cparams(("arbitrary",)),
        name="moe_route",
    )(logits)


def _route(logits):
    rec, cnt = _route_records(logits)
    e_flat = rec[:, R_E0:R_E1 + 1].astype(jnp.int32).reshape(-1)
    rank = rec[:, R_RANK0:R_RANK1 + 1].astype(jnp.int32).reshape(-1)
    counts = cnt[0, :N_EXPERTS].astype(jnp.int32)
    blocks = (counts + MOE_CHUNK - 1) // MOE_CHUNK
    bend = jnp.cumsum(blocks)
    bstart = bend - blocks
    total = bend[-1]
    pos = bstart[e_flat] * MOE_CHUNK + rank
    w = jnp.arange(MOE_ITEMS, dtype=jnp.int32)
    w_eff = jnp.minimum(w, total - 1)
    item_expert = jnp.minimum(jnp.sum((w_eff[:, None] >= bend[None, :]).astype(jnp.int32), axis=1),
                              N_EXPERTS - 1)
    item_rows = jnp.where(w < total,
                          jnp.clip(counts[item_expert] - (w - bstart[item_expert]) * MOE_CHUNK, 0, MOE_CHUNK),
                          0)
    return rec, pos.reshape(N_TOK, TOP_K), item_expert.astype(jnp.int32), \
        w_eff.astype(jnp.int32), item_rows.astype(jnp.int32)


DISPATCH_TM = 256


def _dispatch_kernel(pos_ref, ib_ref, ir_ref, u_ref, o_hbm, zero_ref, sem, zsem):
    i = pl.program_id(0)
    tm = DISPATCH_TM
    rt = MOE_ROW_TILE

    @pl.when(i == 0)
    def _():
        zero_ref[...] = jnp.zeros(zero_ref.shape, zero_ref.dtype)

        def tail_copy(w):
            last_tile = (ir_ref[w] - 1) // rt
            start = pl.multiple_of(ib_ref[w] * MOE_CHUNK + last_tile * rt, rt)
            return pltpu.make_async_copy(zero_ref, o_hbm.at[pl.ds(start, rt)], zsem)

        for w in range(MOE_ITEMS):
            pl.when(ir_ref[w] > 0)(lambda w=w: tail_copy(w).start())
        for w in range(MOE_ITEMS):
            pl.when(ir_ref[w] > 0)(lambda w=w: tail_copy(w).wait())

    def row_copy(r, dst_row):
        return pltpu.make_async_copy(u_ref.at[pl.ds(r, 1)], o_hbm.at[pl.ds(dst_row, 1)], sem)

    def start(r, _):
        a = (i * tm + r) * TOP_K
        for k in range(TOP_K):
            row_copy(r, pos_ref[a + k]).start(priority=k)
        return 0

    def wait(r, _):
        row_copy(0, 0).wait()
        return 0

    lax.fori_loop(0, tm, start, 0, unroll=8)
    lax.fori_loop(0, tm * TOP_K, wait, 0, unroll=8)


def _dispatch(pos_flat, item_block, item_rows, u):
    width = u.shape[1]
    grid_spec = pltpu.PrefetchScalarGridSpec(
        num_scalar_prefetch=3,
        grid=(N_TOK // DISPATCH_TM,),
        in_specs=[pl.BlockSpec((DISPATCH_TM, width), lambda i, pos, ib, ir: (i, 0))],
        out_specs=pl.BlockSpec(memory_space=pl.ANY),
        scratch_shapes=[pltpu.VMEM((MOE_ROW_TILE, width), u.dtype),
                        pltpu.SemaphoreType.DMA(()),
                        pltpu.SemaphoreType.DMA(())],
    )
    return pl.pallas_call(
        _dispatch_kernel,
        grid_spec=grid_spec,
        out_shape=jax.ShapeDtypeStruct((MOE_ROWS, width), u.dtype),
        compiler_params=_cparams(("arbitrary",)),
        name="moe_dispatch",
    )(pos_flat, item_block, item_rows, u)


def _postnorm_kernel(x_ref, y0_ref, y1_ref, rec_ref, gate_ref, g_ref, b_ref, op_ref, os_ref):
    i = pl.program_id(0)
    n_p = TOK_P // x_ref.shape[0]
    delta = (rec_ref[:, R_P0:R_P0 + 1] * y0_ref[...] + rec_ref[:, R_P1:R_P1 + 1] * y1_ref[...])
    y = DN_ALPHA * x_ref[...] + gate_ref[0] * delta
    out = _layer_norm_rows(y, g_ref[...], b_ref[...])

    @pl.when(i < n_p)
    def _():
        op_ref[...] = out

    @pl.when(i >= n_p)
    def _():
        os_ref[...] = out


def _postnorm(x, y0, y1, rec, mods, gate_chunk, g, b):
    tm = 256
    vec = pl.BlockSpec((1, D_MODEL), lambda i: (0, 0))
    rows = pl.BlockSpec((tm, D_MODEL), lambda i: (i, 0))
    return pl.pallas_call(
        _postnorm_kernel,
        grid=(N_TOK // tm,),
        in_specs=[rows, rows, rows, pl.BlockSpec((tm, LANE), lambda i: (i, 0)),
                  _mod_spec(tm, gate_chunk), vec, vec],
        out_specs=_token_row_specs(2, tm, D_MODEL),
        out_shape=[jax.ShapeDtypeStruct((TOK_P, D_MODEL), F32),
                   jax.ShapeDtypeStruct((TOK_S, D_MODEL), F32)],
        compiler_params=_cparams(("arbitrary",)),
        name="final_postnorm",
    )(x, y0, y1, rec, mods, g.reshape(1, D_MODEL), b.reshape(1, D_MODEL))


def kernel(x_prompt, x_sample, c, c_ctx, cache_l0_k, cache_l0_v, state_l0_lru, l0_ada_w, l0_ada_b, l0_w_in, l0_q_norm, l0_k_norm, l0_lru_conv_w, l0_lru_conv_b, l0_lru_lambda, l0_lru_w_r, l0_lru_b_r, l0_lru_w_i, l0_lru_b_i, l0_w_out, l0_ln1_g, l0_ln1_b, l0_ffn_w1, l0_ffn_w3, l0_ffn_w2, l0_ln2_g, l0_ln2_b, l1_ada_w, l1_ada_b, l1_w_in, l1_short_w, l1_short_b, l1_filt_w1, l1_filt_b1, l1_filt_f1, l1_filt_w2, l1_filt_b2, l1_filt_f2, l1_filt_w3, l1_filt_decay, l1_filt_bias, l1_w_out, l1_ln1_g, l1_ln1_b, l1_router, l1_exp_w1, l1_exp_w3, l1_exp_w2, l1_ln2_g, l1_ln2_b):
    x_in = [x_prompt.reshape(TOK_P, D_MODEL), x_sample.reshape(TOK_S, D_MODEL)]
    cond = jnp.concatenate([c_ctx[None, :], c, jnp.zeros((N_COND - 1 - DEC_BATCH, D_MODEL), F32)], axis=0)
    mods0 = _ada(cond, l0_ada_w, l0_ada_b)
    mods1 = _ada(cond, l1_ada_w, l1_ada_b)

    lru_p = dict(conv_w=l0_lru_conv_w, conv_b=l0_lru_conv_b, lam=l0_lru_lambda,
                 w_r=l0_lru_w_r, b_r=l0_lru_b_r, w_i=l0_lru_w_i, b_i=l0_lru_b_i)
    proj0 = _proj(x_in, mods0, l0_w_in, tn=512, name="l0_in_proj")
    attn, new_k, new_v = _attn_context(proj0, l0_q_norm, l0_k_norm)
    attn = _attn_latent(proj0, cache_l0_k, cache_l0_v, l0_q_norm, l0_k_norm, attn)
    lru, new_h = _lru(proj0, jnp.zeros((BATCH, 2, LRU_WIDTH), F32), lru_p,
                      seq=SEQ, batch=BATCH, row0=0, name="lru_context")
    lru, _ = _lru(proj0, state_l0_lru, lru_p,
                  seq=DEC_SEQ, batch=DEC_BATCH, row0=TOK_P, name="lru_latent", dst=lru)
    x = _outproj([attn, lru], l0_w_out, x_in, mods0, 2, l0_ln1_g, l0_ln1_b, name="l0_out_proj")[0]
    x = _ffn(x, mods0, dict(ffn_w1=l0_ffn_w1, ffn_w3=l0_ffn_w3, ffn_w2=l0_ffn_w2,
                            ln2_g=l0_ln2_g, ln2_b=l0_ln2_b))

    hy_p = dict(short_w=l1_short_w, short_b=l1_short_b, filt_w1=l1_filt_w1, filt_b1=l1_filt_b1,
                filt_f1=l1_filt_f1, filt_w2=l1_filt_w2, filt_b2=l1_filt_b2, filt_f2=l1_filt_f2,
                filt_w3=l1_filt_w3, filt_decay=l1_filt_decay, filt_bias=l1_filt_bias)
    proj1 = _proj([x], mods1, l1_w_in, tn=512, name="l1_in_proj")
    z = _hyena(proj1, hy_p, seq=SEQ, batch=BATCH, row0=0, nb=2, ct=1024, name="hyena_context")
    z = _hyena(proj1, hy_p, seq=DEC_SEQ, batch=DEC_BATCH, row0=TOK_P, nb=1, ct=512, name="hyena_latent",
               dst=z)
    router = jnp.pad(l1_router, ((0, 0), (0, LANE - N_EXPERTS)))
    x, u, logits = _outproj([z], l1_w_out, [x], mods1, 2, l1_ln1_g, l1_ln1_b, name="l1_out_proj",
                            router=router, router_chunks=(3, 4))
    rec, pos, item_expert, item_block, item_rows = _route(logits)
    xs = _dispatch(pos.reshape(-1), item_block, item_rows, u)
    ys = _moe_experts(xs, item_expert, item_block, item_rows,
                      dict(exp_w1=l1_exp_w1, exp_w3=l1_exp_w3, exp_w2=l1_exp_w2))
    y_prompt, y_sample = _postnorm(x, _rows(ys, pos[:, 0]), _rows(ys, pos[:, 1]), rec,
                                   mods1, 5, l1_ln2_g, l1_ln2_b)
    return (y_prompt.reshape(BATCH, SEQ, D_MODEL), y_sample.reshape(DEC_BATCH, DEC_SEQ, D_MODEL),
            new_k.reshape(BATCH, SEQ, N_KV_HEADS, HEAD_DIM),
            new_v.reshape(BATCH, SEQ, N_KV_HEADS, HEAD_DIM),
            new_h)
```

```python
import functools
import math

import jax
import jax.numpy as jnp
from jax import lax
from jax.experimental import pallas as pl
from jax.experimental.pallas import tpu as pltpu

F32 = jnp.float32
BF16 = jnp.bfloat16

D_MODEL = 2048
BATCH = 16
SEQ = 256
DEC_BATCH = 2
DEC_SEQ = 1024
PAST_LEN = 256
GRID_W = 64
HEAD_DIM = 128
N_Q_HEADS = 8
N_KV_HEADS = 2
Q_GROUP = N_Q_HEADS // N_KV_HEADS
Q_WIDTH = N_Q_HEADS * HEAD_DIM
KV_WIDTH = N_KV_HEADS * HEAD_DIM
ROPE_THETA = 10000.0
LRU_WIDTH = D_MODEL // 2
LRU_BLOCK = 128
LRU_C = 8.0
MIX0_IN = Q_WIDTH + 2 * KV_WIDTH + 2 * LRU_WIDTH
HYENA_ORDER = 2
HYENA_EMB = 33
HYENA_BANDS = (HYENA_EMB - 1) // 2
HYENA_FILTER_HIDDEN = 64
D_FF = 5632
N_EXPERTS = 8
TOP_K = 2
D_FF_EXPERT = 7168
N_MOD = 6
LN_EPS = 1e-5
QK_EPS = 1e-6
DEPTH = 2
DN_ALPHA = (2 * DEPTH) ** 0.25

TOK_P = BATCH * SEQ
TOK_S = DEC_BATCH * DEC_SEQ
N_TOK = TOK_P + TOK_S
N_COND = 8
LANE = 128
SUBLANE = 8
VMEM_LIMIT = 56 * 1024 * 1024

ROW_TILE = 1024
MOE_CHUNK = 2048
MOE_ROW_TILE = 256
MOE_FF_TILE = 256
MOE_ITEMS = N_EXPERTS + (N_TOK * TOP_K) // MOE_CHUNK
MOE_ROWS = MOE_ITEMS * MOE_CHUNK


def _cparams(sem):
    return pltpu.CompilerParams(dimension_semantics=sem, vmem_limit_bytes=VMEM_LIMIT)


def _cond_of_tile(i, tm):
    return jnp.maximum(i * tm // DEC_SEQ - (TOK_P // DEC_SEQ - 1), 0)


def _mod_spec(tm, chunk, width=D_MODEL, col_of=None):
    per = D_MODEL // width
    if col_of is None:
        return pl.BlockSpec((1, 1, width), lambda i, *_: (_cond_of_tile(i, tm), 0, chunk * per))
    return pl.BlockSpec((1, 1, width),
                        lambda i, j, *_: (_cond_of_tile(i, tm), 0, chunk * per + col_of(j)))


def _silu(x):
    return x * jax.nn.sigmoid(x)


def _split_bf16(x):
    hi = x.astype(BF16)
    lo = (x - hi.astype(F32)).astype(BF16)
    return hi, lo


def _pack_bf16_pair(lo, hi):
    lo_bits = lax.bitcast_convert_type(lo.astype(BF16).astype(F32), jnp.uint32) >> 16
    hi_bits = lax.bitcast_convert_type(hi.astype(BF16).astype(F32), jnp.uint32) & jnp.uint32(0xFFFF0000)
    return hi_bits | lo_bits


def _unpack_bf16_pair(words):
    lo = lax.bitcast_convert_type(words << 16, F32)
    hi = lax.bitcast_convert_type(words & jnp.uint32(0xFFFF0000), F32)
    return jnp.concatenate([lo, hi], axis=1).astype(BF16)


def _rows(a, idx):
    return a.at[idx].get(mode="promise_in_bounds")


def _dot3(a, b):
    ah, al = _split_bf16(a)
    bh, bl = _split_bf16(b)
    d = lambda x, y: jnp.dot(x, y, preferred_element_type=F32)
    return d(ah, bh) + (d(ah, bl) + d(al, bh))


def _layer_norm_rows(y, g, b):
    mu = jnp.mean(y, -1, keepdims=True)
    yc = y - mu
    var = jnp.mean(yc * yc, -1, keepdims=True)
    return yc * lax.rsqrt(var + LN_EPS) * g + b


def _ada_kernel(c_ref, w_ref, b_ref, o_ref):
    s = _silu(c_ref[...]).astype(BF16)
    o_ref[...] = jnp.dot(s, w_ref[...].astype(BF16), preferred_element_type=F32) + b_ref[...]


def _ada(cond, w, b):
    tn = 1024
    n = w.shape[1]
    out = pl.pallas_call(
        _ada_kernel,
        grid=(n // tn,),
        in_specs=[pl.BlockSpec((N_COND, D_MODEL), lambda j: (0, 0)),
                  pl.BlockSpec((D_MODEL, tn), lambda j: (0, j)),
                  pl.BlockSpec((1, tn), lambda j: (0, j))],
        out_specs=pl.BlockSpec((N_COND, tn), lambda j: (0, j)),
        out_shape=jax.ShapeDtypeStruct((N_COND, n), F32),
        compiler_params=_cparams(("arbitrary",)),
        name="ada_modulation",
    )(cond, w, b.reshape(1, n))
    return out.reshape(N_COND, 1, n)


def _cast_kernel(x_ref, o_ref):
    o_ref[...] = x_ref[...].astype(o_ref.dtype)


def _cast_bf16(w, rows, name):
    r, c = w.shape
    return pl.pallas_call(
        _cast_kernel,
        grid=(r // rows,),
        in_specs=[pl.BlockSpec((rows, c), lambda i: (i, 0))],
        out_specs=pl.BlockSpec((rows, c), lambda i: (i, 0)),
        out_shape=jax.ShapeDtypeStruct((r, c), BF16),
        compiler_params=_cparams(("arbitrary",)),
        name=name,
    )(w)


def _token_row_specs(n_parts, tm, width, col=None, buffered_once=False):
    kw = dict(pipeline_mode=pl.Buffered(1)) if buffered_once else {}
    col = col or (lambda i, *a: 0)
    if n_parts == 1:
        return [pl.BlockSpec((tm, width), lambda i, *a: (i, col(i, *a)), **kw)]
    n_p = TOK_P // tm
    return [pl.BlockSpec((tm, width), lambda i, *a: (jnp.minimum(i, n_p - 1), col(i, *a)), **kw),
            pl.BlockSpec((tm, width), lambda i, *a: (jnp.maximum(i - n_p, 0), col(i, *a)), **kw)]


def _proj_kernel(*refs, n_x):
    x_refs = refs[:n_x]
    sh_ref, sc_ref, w_ref, o_ref, u_ref, wb_ref = refs[n_x:]
    i = pl.program_id(0)
    j = pl.program_id(1)

    def modulate_from(x_ref):
        u_ref[...] = (x_ref[...] * (1.0 + sc_ref[0]) + sh_ref[0]).astype(BF16)

    @pl.when(j == 0)
    def _():
        if n_x == 1:
            modulate_from(x_refs[0])
        else:
            n_p = TOK_P // u_ref.shape[0]
            pl.when(i < n_p)(lambda: modulate_from(x_refs[0]))
            pl.when(i >= n_p)(lambda: modulate_from(x_refs[1]))

    @pl.when(i == 0)
    def _():
        wb_ref[j] = w_ref[...].astype(BF16)

    o_ref[...] = jnp.dot(u_ref[...], wb_ref[j], preferred_element_type=F32)


def _first_pass_block(i, j, nj):
    return jnp.where(i == 0, j, nj - 1)


def _proj(xs, mods, w, *, tn, name):
    tm = ROW_TILE
    n = w.shape[1]
    nj = n // tn
    return pl.pallas_call(
        functools.partial(_proj_kernel, n_x=len(xs)),
        grid=(N_TOK // tm, nj),
        in_specs=_token_row_specs(len(xs), tm, D_MODEL, buffered_once=True) + [
            _mod_spec(tm, 0), _mod_spec(tm, 1),
            pl.BlockSpec((D_MODEL, tn), lambda i, j: (0, _first_pass_block(i, j, nj)))],
        out_specs=pl.BlockSpec((tm, tn), lambda i, j: (i, j)),
        out_shape=jax.ShapeDtypeStruct((N_TOK, n), F32),
        scratch_shapes=[pltpu.VMEM((tm, D_MODEL), BF16),
                        pltpu.VMEM((nj, D_MODEL, tn), BF16)],
        compiler_params=_cparams(("arbitrary", "arbitrary")),
        name=name,
    )(*xs, mods, mods, w)


def _rms(x, g):
    return x * lax.rsqrt(jnp.mean(x * x, -1, keepdims=True) + QK_EPS) * g


def _dot_nt(a, b):
    return lax.dot_general(a, b, (((1,), (1,)), ((), ())), preferred_element_type=F32)


def _rope(x, cos, sin_signed):
    lane = lax.broadcasted_iota(jnp.int32, x.shape, 1)
    partner = jnp.where(lane % 2 == 0, pltpu.roll(x, HEAD_DIM - 1, 1), pltpu.roll(x, 1, 1))
    return x * cos + partner * sin_signed


def _attn_ctx_kernel(q_ref, k_ref, v_ref, qn_ref, kn_ref, o_ref, ko_ref, vo_ref):
    scale = HEAD_DIM ** -0.5
    kn = _rms(k_ref[...], kn_ref[...])
    v = v_ref[...]
    ko_ref[...] = kn
    vo_ref[...] = v
    kb = kn.astype(BF16)
    vb = v.astype(BF16)
    for g in range(Q_GROUP):
        cols = slice(g * HEAD_DIM, (g + 1) * HEAD_DIM)
        q = _rms(q_ref[:, cols], qn_ref[...]).astype(BF16)
        s = _dot_nt(q, kb) * scale
        p = jnp.exp(s - jnp.max(s, -1, keepdims=True))
        p = p / jnp.sum(p, -1, keepdims=True)
        o = jnp.dot(p.astype(BF16), vb, preferred_element_type=F32)
        o_ref[:, cols] = o.astype(o_ref.dtype)


def _attn_context(proj, q_norm, k_norm):
    qw = Q_GROUP * HEAD_DIM
    k_blk0 = Q_WIDTH // HEAD_DIM
    v_blk0 = (Q_WIDTH + KV_WIDTH) // HEAD_DIM
    vec = pl.BlockSpec((1, HEAD_DIM), lambda b, h: (0, 0))
    return pl.pallas_call(
        _attn_ctx_kernel,
        grid=(BATCH, N_KV_HEADS),
        in_specs=[pl.BlockSpec((SEQ, qw), lambda b, h: (b, h)),
                  pl.BlockSpec((SEQ, HEAD_DIM), lambda b, h: (b, k_blk0 + h)),
                  pl.BlockSpec((SEQ, HEAD_DIM), lambda b, h: (b, v_blk0 + h)),
                  vec, vec],
        out_specs=[pl.BlockSpec((SEQ, qw), lambda b, h: (b, h)),
                   pl.BlockSpec((SEQ, HEAD_DIM), lambda b, h: (b, h)),
                   pl.BlockSpec((SEQ, HEAD_DIM), lambda b, h: (b, h))],
        out_shape=[jax.ShapeDtypeStruct((N_TOK, Q_WIDTH), BF16),
                   jax.ShapeDtypeStruct((TOK_P, KV_WIDTH), F32),
                   jax.ShapeDtypeStruct((TOK_P, KV_WIDTH), F32)],
        compiler_params=_cparams(("arbitrary", "arbitrary")),
        name="attn_context",
    )(proj, proj, proj, q_norm.reshape(1, HEAD_DIM), k_norm.reshape(1, HEAD_DIM))


ATTN_Q_ROWS = 256


def _attn_lat_kernel(q_ref, k_ref, v_ref, ck_ref, cv_ref, qn_ref, kn_ref,
                     cq_ref, sq_ref, ck_tab_ref, sk_tab_ref, dst_ref, o_ref):
    del dst_ref
    scale = HEAD_DIM ** -0.5
    kb = _rope(_rms(k_ref[...], kn_ref[...]), ck_tab_ref[...], sk_tab_ref[...]).astype(BF16)
    vb = v_ref[...].astype(BF16)
    ckb = ck_ref[...].astype(BF16)
    cvb = cv_ref[...].astype(BF16)
    for g in range(Q_GROUP):
        cols = slice(g * HEAD_DIM, (g + 1) * HEAD_DIM)
        q = _rope(_rms(q_ref[:, cols], qn_ref[...]), cq_ref[...], sq_ref[...]).astype(BF16)
        s1 = _dot_nt(q, ckb) * scale
        s2 = _dot_nt(q, kb) * scale
        m = jnp.maximum(jnp.max(s1, -1, keepdims=True), jnp.max(s2, -1, keepdims=True))
        p1 = jnp.exp(s1 - m)
        p2 = jnp.exp(s2 - m)
        den = jnp.sum(p1, -1, keepdims=True) + jnp.sum(p2, -1, keepdims=True)
        o = (jnp.dot((p1 / den).astype(BF16), cvb, preferred_element_type=F32)
             + jnp.dot((p2 / den).astype(BF16), vb, preferred_element_type=F32))
        o_ref[:, cols] = o.astype(o_ref.dtype)


def _rope_tables():
    t = jnp.arange(DEC_SEQ)
    row = (t // GRID_W).astype(F32)
    col = (t % GRID_W).astype(F32)
    n_freq = HEAD_DIM // 4
    inv = 1.0 / (ROPE_THETA ** (jnp.arange(n_freq, dtype=F32) / n_freq))
    ang = jnp.concatenate([row[:, None] * inv, col[:, None] * inv], -1)
    cos = jnp.repeat(jnp.cos(ang), 2, axis=-1)
    sign = jnp.where(jnp.arange(HEAD_DIM) % 2 == 0, -1.0, 1.0).astype(F32)
    sin_signed = jnp.repeat(jnp.sin(ang), 2, axis=-1) * sign
    return cos, sin_signed


_IN_PLACE = pl.BlockSpec(memory_space=pl.ANY)


def _attn_latent(proj, cache_k, cache_v, q_norm, k_norm, dst):
    qw = Q_GROUP * HEAD_DIM
    nq = DEC_SEQ // ATTN_Q_ROWS
    q_row0 = TOK_P // ATTN_Q_ROWS
    kv_row0 = TOK_P // DEC_SEQ
    k_blk0 = Q_WIDTH // HEAD_DIM
    v_blk0 = (Q_WIDTH + KV_WIDTH) // HEAD_DIM
    cos, sin_signed = _rope_tables()
    vec = pl.BlockSpec((1, HEAD_DIM), lambda b, h, c: (0, 0))
    tab_q = pl.BlockSpec((ATTN_Q_ROWS, HEAD_DIM), lambda b, h, c: (c, 0))
    tab_k = pl.BlockSpec((DEC_SEQ, HEAD_DIM), lambda b, h, c: (0, 0))
    ctx = pl.BlockSpec((PAST_LEN, HEAD_DIM), lambda b, h, c: (b, h))
    return pl.pallas_call(
        _attn_lat_kernel,
        grid=(DEC_BATCH, N_KV_HEADS, nq),
        in_specs=[pl.BlockSpec((ATTN_Q_ROWS, qw), lambda b, h, c: (q_row0 + b * nq + c, h)),
                  pl.BlockSpec((DEC_SEQ, HEAD_DIM), lambda b, h, c: (kv_row0 + b, k_blk0 + h)),
                  pl.BlockSpec((DEC_SEQ, HEAD_DIM), lambda b, h, c: (kv_row0 + b, v_blk0 + h)),
                  ctx, ctx, vec, vec, tab_q, tab_q, tab_k, tab_k, _IN_PLACE],
        out_specs=pl.BlockSpec((ATTN_Q_ROWS, qw), lambda b, h, c: (q_row0 + b * nq + c, h)),
        out_shape=jax.ShapeDtypeStruct((N_TOK, Q_WIDTH), BF16),
        input_output_aliases={11: 0},
        compiler_params=_cparams(("arbitrary", "arbitrary", "arbitrary")),
        name="attn_latent",
    )(proj, proj, proj,
      cache_k.reshape(DEC_BATCH * PAST_LEN, KV_WIDTH), cache_v.reshape(DEC_BATCH * PAST_LEN, KV_WIDTH),
      q_norm.reshape(1, HEAD_DIM), k_norm.reshape(1, HEAD_DIM), cos, sin_signed, cos, sin_signed, dst)


LRU_CT = 512
assert (Q_WIDTH + 2 * KV_WIDTH) % LRU_CT == 0 and LRU_WIDTH % LRU_CT == 0


def _sigmoid_tanh(x):
    return 0.5 * (jnp.tanh(0.5 * x) + 1.0)
LRU_RC = 128


def _softplus(x):
    return jnp.maximum(x, 0.0) + jnp.log1p(jnp.exp(-jnp.abs(x)))


def _gelu_tanh(x):
    return 0.5 * x * (1.0 + jnp.tanh(math.sqrt(2.0 / math.pi) * (x + 0.044715 * (x * x * x))))


def _lru_kernel(*refs, seq, has_dst):
    xb_ref, gb_ref, cw_ref, cb_ref, lam_ref, wr_ref, br_ref, wi_ref, bi_ref, h0_ref = refs[:10]
    y_ref, hfin_ref, xpad_ref, a_ref, b_ref, hf_ref, hb_ref = refs[10 + int(has_dst):]
    ct = LRU_CT
    zeros = jnp.zeros((SUBLANE, ct), F32)
    xpad_ref[0:SUBLANE, :] = zeros
    xpad_ref[seq + SUBLANE:seq + 2 * SUBLANE, :] = zeros
    xpad_ref[SUBLANE:seq + SUBLANE, :] = xb_ref[...]
    cw = cw_ref[...]
    sp = _softplus(-lam_ref[...])
    for r0 in range(0, seq, LRU_RC):
        xc = cb_ref[...] + cw[0:1, :] * xpad_ref[r0 + 7:r0 + 7 + LRU_RC, :]
        for w in range(1, 4):
            xc = xc + cw[w:w + 1, :] * xpad_ref[r0 + 7 + w:r0 + 7 + w + LRU_RC, :]
        for kb in range(ct // LRU_BLOCK):
            cols = slice(kb * LRU_BLOCK, (kb + 1) * LRU_BLOCK)
            xk = xc[:, cols]
            xkb = xk.astype(BF16)
            for d in range(2):
                r = _sigmoid_tanh(jnp.dot(xkb, wr_ref[d, kb].astype(BF16), preferred_element_type=F32)
                                  + br_ref[d:d + 1, cols])
                i = _sigmoid_tanh(jnp.dot(xkb, wi_ref[d, kb].astype(BF16), preferred_element_type=F32)
                                  + bi_ref[d:d + 1, cols])
                log_a = -LRU_C * r * sp[d:d + 1, cols]
                a = jnp.exp(log_a)
                a_ref[d, r0:r0 + LRU_RC, cols] = a
                b_ref[d, r0:r0 + LRU_RC, cols] = jnp.sqrt(-jnp.tanh(log_a) * (a * a + 1.0)) * (i * xk)

    nblk = seq // SUBLANE
    row = lax.broadcasted_iota(jnp.int32, (SUBLANE, ct), 0)

    def body(i, carry):
        hf, hb = carry
        rf = pl.multiple_of(i * SUBLANE, SUBLANE)
        a = a_ref[0, pl.ds(rf, SUBLANE), :]
        b = b_ref[0, pl.ds(rf, SUBLANE), :]
        for s in (1, 2, 4):
            m = row >= s
            a_s = jnp.where(m, pltpu.roll(a, s, 0), 1.0)
            b_s = jnp.where(m, pltpu.roll(b, s, 0), 0.0)
            b = a * b_s + b
            a = a * a_s
        hblk = a * hf + b
        hf_ref[pl.ds(rf, SUBLANE), :] = hblk
        hf = hblk[SUBLANE - 1:SUBLANE, :]

        rb = pl.multiple_of((nblk - 1 - i) * SUBLANE, SUBLANE)
        a = a_ref[1, pl.ds(rb, SUBLANE), :]
        b = b_ref[1, pl.ds(rb, SUBLANE), :]
        for s in (1, 2, 4):
            m = row < SUBLANE - s
            a_s = jnp.where(m, pltpu.roll(a, SUBLANE - s, 0), 1.0)
            b_s = jnp.where(m, pltpu.roll(b, SUBLANE - s, 0), 0.0)
            b = a * b_s + b
            a = a * a_s
        hblk = a * hb + b
        hb_ref[pl.ds(rb, SUBLANE), :] = hblk
        hb = hblk[0:1, :]
        return hf, hb

    hf, hb = lax.fori_loop(0, nblk, body, (h0_ref[0:1, :], h0_ref[1:2, :]))
    hfin_ref[0:1, :] = hf
    hfin_ref[1:2, :] = hb
    for r0 in range(0, seq, LRU_RC):
        rows = slice(r0, r0 + LRU_RC)
        y = (hf_ref[rows, :] + hb_ref[rows, :]) * _gelu_tanh(gb_ref[rows, :])
        y_ref[rows, :] = y.astype(y_ref.dtype)


def _lru(proj, h0, p, *, seq, batch, row0, name, dst=None):
    ct = LRU_CT
    nkb = ct // LRU_BLOCK
    xb_blk0 = (Q_WIDTH + 2 * KV_WIDTH) // ct
    gb_blk0 = (Q_WIDTH + 2 * KV_WIDTH + LRU_WIDTH) // ct
    r0 = row0 // seq
    vec2 = pl.BlockSpec((2, ct), lambda b, c: (0, c))
    wblk = pl.BlockSpec((2, nkb, LRU_BLOCK, LRU_BLOCK), lambda b, c: (0, c, 0, 0))
    extra_specs, extra_args, aliases = [], [], {}
    if dst is not None:
        extra_specs, extra_args, aliases = [_IN_PLACE], [dst], {10: 0}
    return pl.pallas_call(
        functools.partial(_lru_kernel, seq=seq, has_dst=dst is not None),
        grid=(batch, LRU_WIDTH // ct),
        in_specs=[pl.BlockSpec((seq, ct), lambda b, c: (r0 + b, xb_blk0 + c)),
                  pl.BlockSpec((seq, ct), lambda b, c: (r0 + b, gb_blk0 + c)),
                  pl.BlockSpec((4, ct), lambda b, c: (0, c)),
                  pl.BlockSpec((1, ct), lambda b, c: (0, c)),
                  vec2, wblk, vec2, wblk, vec2,
                  pl.BlockSpec((None, 2, ct), lambda b, c: (b, 0, c))] + extra_specs,
        out_specs=[pl.BlockSpec((seq, ct), lambda b, c: (r0 + b, c)),
                   pl.BlockSpec((None, 2, ct), lambda b, c: (b, 0, c))],
        out_shape=[jax.ShapeDtypeStruct((N_TOK, LRU_WIDTH), BF16),
                   jax.ShapeDtypeStruct((batch, 2, LRU_WIDTH), F32)],
        input_output_aliases=aliases,
        scratch_shapes=[pltpu.VMEM((seq + 2 * SUBLANE, ct), F32),
                        pltpu.VMEM((2, seq, ct), F32),
                        pltpu.VMEM((2, seq, ct), F32),
                        pltpu.VMEM((seq, ct), F32),
                        pltpu.VMEM((seq, ct), F32)],
        compiler_params=_cparams(("arbitrary", "arbitrary")),
        name=name,
    )(proj, proj, p['conv_w'], p['conv_b'].reshape(1, LRU_WIDTH), p['lam'],
      p['w_r'], p['b_r'], p['w_i'], p['b_i'], h0, *extra_args)


OUT_TN = 1024
OUT_TM = 512


def _outproj_kernel(*refs, n_a, n_res, with_router):
    a_refs = refs[:n_a]
    w_refs = refs[n_a:2 * n_a]
    res_refs = refs[2 * n_a:2 * n_a + n_res]
    gate_ref, g_ref, b_ref = refs[2 * n_a + n_res:2 * n_a + n_res + 3]
    pos = 2 * n_a + n_res + 3
    if with_router:
        sh_ref, sc_ref, rt_ref = refs[pos:pos + 3]
        pos += 3
        o_ref, u_ref, lg_ref, acc_ref = refs[pos:pos + 4]
        pos += 4
    else:
        o_ref, acc_ref = refs[pos:pos + 2]
        pos += 2
    wb_refs = refs[pos:pos + n_a]
    i = pl.program_id(0)
    j = pl.program_id(1)
    nj = pl.num_programs(1)

    @pl.when(i == 0)
    def _():
        for k in range(n_a):
            wb_refs[k][j] = w_refs[k][...].astype(BF16)

    acc = jnp.dot(a_refs[0][...], wb_refs[0][j], preferred_element_type=F32)
    for k in range(1, n_a):
        acc = acc + jnp.dot(a_refs[k][...], wb_refs[k][j], preferred_element_type=F32)
    if n_res == 1:
        res = res_refs[0][...]
    else:
        n_p = TOK_P // acc_ref.shape[1]
        res = jnp.where(i < n_p, res_refs[0][...], res_refs[1][...])
    acc_ref[j] = DN_ALPHA * res + gate_ref[0] * acc

    @pl.when(j == nj - 1)
    def _():
        tm = acc_ref.shape[1]
        n_chunks = acc_ref.shape[0]
        rc = 128

        def chunk(c, _):
            r0 = pl.multiple_of(c * rc, rc)
            ys = [acc_ref[k, pl.ds(r0, rc), :] for k in range(n_chunks)]
            tot = ys[0].sum(-1, keepdims=True)
            for y in ys[1:]:
                tot = tot + y.sum(-1, keepdims=True)
            mu = tot / D_MODEL
            sq = None
            for y in ys:
                t = ((y - mu) * (y - mu)).sum(-1, keepdims=True)
                sq = t if sq is None else sq + t
            rstd = lax.rsqrt(sq / D_MODEL + LN_EPS)
            lg = None
            us = []
            for k, y in enumerate(ys):
                cols = slice(k * OUT_TN, (k + 1) * OUT_TN)
                xn = (y - mu) * rstd * g_ref[:, cols] + b_ref[:, cols]
                o_ref[pl.ds(r0, rc), cols] = xn
                if with_router:
                    u = xn * (1.0 + sc_ref[0][:, cols]) + sh_ref[0][:, cols]
                    us.append(u)
                    t = _dot3(u, rt_ref[cols, :])
                    lg = t if lg is None else lg + t
            if with_router:
                lg_ref[pl.ds(r0, rc), :] = lg
                half = n_chunks // 2
                for k in range(half):
                    cols = slice(k * OUT_TN, (k + 1) * OUT_TN)
                    u_ref[pl.ds(r0, rc), cols] = _pack_bf16_pair(us[k], us[k + half])
            return 0

        lax.fori_loop(0, tm // rc, chunk, 0)


def _outproj(a_list, w, res, mods, gate_chunk, g, b, *, name, router=None, router_chunks=None):
    tm = OUT_TM
    tn = OUT_TN
    nj = D_MODEL // tn
    n_a = len(a_list)
    in_specs = []
    k0 = 0
    w_specs = []
    wb_shapes = []
    for a in a_list:
        ka = a.shape[1]
        in_specs.append(pl.BlockSpec((tm, ka), lambda i, j: (i, 0)))
        w_specs.append(pl.BlockSpec((ka, tn),
                                    lambda i, j, blk=k0 // ka: (blk, _first_pass_block(i, j, nj))))
        wb_shapes.append(pltpu.VMEM((nj, ka, tn), BF16))
        k0 += ka
    in_specs += w_specs
    in_specs += _token_row_specs(len(res), tm, tn, col=lambda i, j: j)
    in_specs += [_mod_spec(tm, gate_chunk, tn, col_of=lambda j: j),
                 pl.BlockSpec((1, D_MODEL), lambda i, j: (0, 0)),
                 pl.BlockSpec((1, D_MODEL), lambda i, j: (0, 0))]
    args = list(a_list) + [w] * n_a + list(res) + [mods, g.reshape(1, D_MODEL), b.reshape(1, D_MODEL)]
    out_specs = [pl.BlockSpec((tm, D_MODEL), lambda i, j: (i, 0))]
    out_shape = [jax.ShapeDtypeStruct((N_TOK, D_MODEL), F32)]
    if router is not None:
        in_specs += [_mod_spec(tm, router_chunks[0]), _mod_spec(tm, router_chunks[1]),
                     pl.BlockSpec((D_MODEL, LANE), lambda i, j: (0, 0))]
        args += [mods, mods, router]
        out_specs += [pl.BlockSpec((tm, D_MODEL // 2), lambda i, j: (i, 0)),
                      pl.BlockSpec((tm, LANE), lambda i, j: (i, 0))]
        out_shape += [jax.ShapeDtypeStruct((N_TOK, D_MODEL // 2), jnp.uint32),
                      jax.ShapeDtypeStruct((N_TOK, LANE), F32)]
    return pl.pallas_call(
        functools.partial(_outproj_kernel, n_a=n_a, n_res=len(res), with_router=router is not None),
        grid=(N_TOK // tm, nj),
        in_specs=in_specs,
        out_specs=out_specs,
        out_shape=out_shape,
        scratch_shapes=[pltpu.VMEM((nj, tm, tn), F32)] + wb_shapes,
        compiler_params=_cparams(("arbitrary", "arbitrary")),
        name=name,
    )(*args)


FFN_TF = 512


def _ffn_kernel(x_ref, sh_ref, sc_ref, gate_ref, g_ref, b_ref, w1_ref, w3_ref, w2_ref, o_ref, u_ref):
    j = pl.program_id(1)
    nj = pl.num_programs(1)

    @pl.when(j == 0)
    def _():
        u_ref[...] = (x_ref[...] * (1.0 + sc_ref[0]) + sh_ref[0]).astype(BF16)
        o_ref[...] = jnp.zeros(o_ref.shape, F32)

    u = u_ref[...]
    h1 = jnp.dot(u, w1_ref[...], preferred_element_type=F32)
    h3 = jnp.dot(u, w3_ref[...], preferred_element_type=F32)
    h = (_silu(h1) * h3).astype(BF16)
    o_ref[...] += jnp.dot(h, w2_ref[...], preferred_element_type=F32)

    @pl.when(j == nj - 1)
    def _():
        rc = 128

        def chunk(c, _):
            r0 = pl.multiple_of(c * rc, rc)
            y = DN_ALPHA * x_ref[pl.ds(r0, rc), :] + gate_ref[0] * o_ref[pl.ds(r0, rc), :]
            o_ref[pl.ds(r0, rc), :] = _layer_norm_rows(y, g_ref[...], b_ref[...])
            return 0

        lax.fori_loop(0, o_ref.shape[0] // rc, chunk, 0)


def _ffn(x, mods, p):
    tm = ROW_TILE
    tf = FFN_TF
    vec = pl.BlockSpec((1, D_MODEL), lambda i, j: (0, 0))
    return pl.pallas_call(
        _ffn_kernel,
        grid=(N_TOK // tm, D_FF // tf),
        in_specs=[pl.BlockSpec((tm, D_MODEL), lambda i, j: (i, 0), pipeline_mode=pl.Buffered(1)),
                  _mod_spec(tm, 3), _mod_spec(tm, 4), _mod_spec(tm, 5), vec, vec,
                  pl.BlockSpec((D_MODEL, tf), lambda i, j: (0, j)),
                  pl.BlockSpec((D_MODEL, tf), lambda i, j: (0, j)),
                  pl.BlockSpec((tf, D_MODEL), lambda i, j: (j, 0))],
        out_specs=pl.BlockSpec((tm, D_MODEL), lambda i, j: (i, 0)),
        out_shape=jax.ShapeDtypeStruct((N_TOK, D_MODEL), F32),
        scratch_shapes=[pltpu.VMEM((tm, D_MODEL), BF16)],
        compiler_params=_cparams(("arbitrary", "arbitrary")),
        name="ffn_swiglu",
    )(x, mods, mods, mods, p['ln2_g'].reshape(1, D_MODEL), p['ln2_b'].reshape(1, D_MODEL),
      _cast_bf16(p['ffn_w1'], 256, "ffn_w1_bf16"), _cast_bf16(p['ffn_w3'], 256, "ffn_w3_bf16"),
      _cast_bf16(p['ffn_w2'], 512, "ffn_w2_bf16"))


def _filter_mlp_kernel(feat_ref, w1_ref, b1_ref, f1_ref, w2_ref, b2_ref, f2_ref, o_ref):
    h = jnp.sin(f1_ref[...] * (_dot3(feat_ref[...], w1_ref[...]) + b1_ref[...]))
    o_ref[...] = jnp.sin(f2_ref[...] * (_dot3(h, w2_ref[...]) + b2_ref[...]))


def _filter_features(seq):
    t = jnp.arange(seq, dtype=F32)
    t01 = t / (seq - 1)
    w = 2.0 * math.pi * t / seq
    f = jnp.linspace(1e-4, HYENA_BANDS - 1, HYENA_BANDS, dtype=F32)
    fw = w[:, None] * f[None, :]
    feat = jnp.concatenate([t01[:, None], jnp.cos(fw), -jnp.sin(fw)], -1)
    return jnp.pad(feat, ((0, 0), (0, LANE - HYENA_EMB))), t01[:, None]


def _filter_mlp(seq, p):
    feat, t01 = _filter_features(seq)
    hid = HYENA_FILTER_HIDDEN
    w1 = jnp.pad(p['filt_w1'], ((0, LANE - HYENA_EMB), (0, 0)))
    row = lambda v: v.reshape(1, hid)
    h2 = pl.pallas_call(
        _filter_mlp_kernel,
        out_shape=jax.ShapeDtypeStruct((seq, hid), F32),
        compiler_params=pltpu.CompilerParams(vmem_limit_bytes=VMEM_LIMIT),
        name=f"hyena_filter_mlp_{seq}",
    )(feat, w1, row(p['filt_b1']), row(p['filt_f1']), p['filt_w2'], row(p['filt_b2']), row(p['filt_f2']))
    return h2, t01


def _dft_matrices(seq):
    n = 2 * seq
    f = jnp.arange(seq, dtype=jnp.int32)

    def table(freqs):
        ang = ((freqs[:, None] * f[None, :]) % n).astype(F32) * (math.pi / seq)
        return jnp.cos(ang), jnp.sin(ang)

    step = 32
    ca, sa = table(jnp.arange(seq // step, dtype=jnp.int32) * step)
    cb, sb = table(jnp.arange(step, dtype=jnp.int32))
    c = (ca[:, None, :] * cb[None, :, :] - sa[:, None, :] * sb[None, :, :]).reshape(seq, seq)
    s = -(sa[:, None, :] * cb[None, :, :] + ca[:, None, :] * sb[None, :, :]).reshape(seq, seq)
    alt = jnp.where(f % 2 == 0, 1.0, -1.0).astype(F32)
    s_fwd = s.at[0, :].set(alt)
    s_inv = s.at[:, 0].set(alt)
    fwd = jnp.concatenate([c, s_fwd], axis=0).astype(BF16)
    inv = jnp.concatenate([c, s_inv], axis=1).astype(BF16)
    return fwd, inv, alt[:, None]


HY_RC = 128


def _hyena_kernel(*refs, seq, nb, ct, has_dst):
    (x1_ref, x2_ref, z_ref, sw1_ref, sw2_ref, swz_ref, sb1_ref, sb2_ref, sbz_ref,
     h2_ref, t01_ref, alt_ref, dec_ref, bias_ref, fwd_ref, inv_ref) = refs[:16]
    w3_ref = refs[16:16 + 2 * HYENA_ORDER]
    n_in = 16 + 2 * HYENA_ORDER + int(has_dst)
    (o_ref, kr_ref, ki_ref, zb_ref, zf_ref, yb_ref, pad_ref, g1_ref, g2_ref,
     zc_ref) = refs[n_in:]
    nrc = seq // HY_RC

    @pl.when(pl.program_id(1) == 0)
    def _():
        t01 = t01_ref[...]
        h2 = h2_ref[...]
        row0 = lax.broadcasted_iota(jnp.int32, (seq, ct), 0) == 0
        wf = jnp.where(row0, 1.0, 2.0) / (2.0 * seq)
        for n in range(HYENA_ORDER):
            kpos = _dot3(h2, w3_ref[n][...]) * jnp.exp(-t01 * jnp.abs(dec_ref[n:n + 1, :]))
            kneg = _dot3(h2, w3_ref[HYENA_ORDER + n][...]) * jnp.exp(
                -t01 * jnp.abs(dec_ref[HYENA_ORDER + n:HYENA_ORDER + n + 1, :]))
            kneg = jnp.where(row0, 0.0, kneg)
            ksum = kpos + kneg
            kdif = kpos - kneg
            kr = jnp.dot(fwd_ref[0:seq, :], ksum.astype(BF16), preferred_element_type=F32)
            ki = jnp.dot(fwd_ref[seq:2 * seq, :], kdif.astype(BF16), preferred_element_type=F32)
            nyq = jnp.sum(alt_ref[...] * ksum, axis=0, keepdims=True) / (2.0 * seq)
            kr_ref[n] = kr * wf
            ki_ref[n] = jnp.where(row0, nyq, ki * wf)

    zeros = jnp.zeros((SUBLANE, ct), F32)
    pad_ref[0:SUBLANE, :] = zeros
    pad_ref[seq + SUBLANE:seq + 2 * SUBLANE, :] = zeros

    def short_conv(src_ref, rows0, w_ref, b_ref, dst_ref):
        pad_ref[SUBLANE:seq + SUBLANE, :] = src_ref[rows0:rows0 + seq, :]
        w = w_ref[...]
        for c in range(nrc):
            r0 = c * HY_RC
            acc = b_ref[...] + w[0:1, :] * pad_ref[r0 + 7:r0 + 7 + HY_RC, :]
            acc = acc + w[1:2, :] * pad_ref[r0 + 8:r0 + 8 + HY_RC, :]
            acc = acc + w[2:3, :] * pad_ref[r0 + 9:r0 + 9 + HY_RC, :]
            dst_ref[r0:r0 + HY_RC, :] = acc

    for bi in range(nb):
        rows0 = bi * seq
        short_conv(x1_ref, rows0, sw1_ref, sb1_ref, g1_ref)
        short_conv(x2_ref, rows0, sw2_ref, sb2_ref, g2_ref)
        short_conv(z_ref, rows0, swz_ref, sbz_ref, zc_ref)
        for n, gate_ref in enumerate((g1_ref, g2_ref)):
            for c in range(nrc):
                rows = slice(c * HY_RC, (c + 1) * HY_RC)
                zb_ref[rows, :] = zc_ref[rows, :].astype(BF16)
            zf_ref[...] = jnp.dot(fwd_ref[...], zb_ref[...], preferred_element_type=F32)
            for c in range(nrc):
                rows = slice(c * HY_RC, (c + 1) * HY_RC)
                rows_i = slice(seq + c * HY_RC, seq + (c + 1) * HY_RC)
                zr = zf_ref[rows, :]
                zi = zf_ref[rows_i, :]
                kr = kr_ref[n, rows, :]
                ki = ki_ref[n, rows, :]
                yr = zr * kr - zi * ki
                yi = zr * ki + zi * kr
                if c == 0:
                    first = lax.broadcasted_iota(jnp.int32, (HY_RC, ct), 0) == 0
                    yr = jnp.where(first, zr * kr, yr)
                    yi = jnp.where(first, zi * ki, yi)
                yb_ref[rows, :] = yr.astype(BF16)
                yb_ref[rows_i, :] = yi.astype(BF16)
            zf_ref[0:seq, :] = jnp.dot(inv_ref[...], yb_ref[...], preferred_element_type=F32)
            for c in range(nrc):
                rows = slice(c * HY_RC, (c + 1) * HY_RC)
                zc = zc_ref[rows, :]
                znew = gate_ref[rows, :] * (zf_ref[rows, :] + zc * bias_ref[n:n + 1, :])
                if n == HYENA_ORDER - 1:
                    o_ref[rows0 + c * HY_RC:rows0 + (c + 1) * HY_RC, :] = znew.astype(o_ref.dtype)
                else:
                    zc_ref[rows, :] = znew


def _hyena(proj, p, *, seq, batch, row0, nb, ct, name, dst=None):
    nct = D_MODEL // ct
    extra_specs, extra_args, aliases = [], [], {}
    if dst is not None:
        extra_specs, extra_args, aliases = [_IN_PLACE], [dst], {16 + 2 * HYENA_ORDER: 0}
    h2, t01 = _filter_mlp(seq, p)
    fwd, inv, alt = _dft_matrices(seq)
    rb0 = row0 // (nb * seq)
    w3 = p['filt_w3']
    w3k = lambda k: pl.BlockSpec((HYENA_FILTER_HIDDEN, ct), lambda c, b: (0, k * nct + c))
    dec = p['filt_decay'].reshape(2 * HYENA_ORDER, D_MODEL)
    sw = p['short_w']
    sb = p['short_b'].reshape(1, 3 * D_MODEL)
    slab = lambda k: pl.BlockSpec((nb * seq, ct), lambda c, b: (rb0 + b, k * nct + c))
    swk = lambda k: pl.BlockSpec((3, ct), lambda c, b: (0, k * nct + c))
    sbk = lambda k: pl.BlockSpec((1, ct), lambda c, b: (0, k * nct + c))
    const = lambda shape: pl.BlockSpec(shape, lambda c, b: tuple(0 for _ in shape))
    return pl.pallas_call(
        functools.partial(_hyena_kernel, seq=seq, nb=nb, ct=ct, has_dst=dst is not None),
        grid=(nct, batch // nb),
        in_specs=[slab(0), slab(1), slab(2), swk(0), swk(1), swk(2), sbk(0), sbk(1), sbk(2),
                  const((seq, HYENA_FILTER_HIDDEN)), const((seq, 1)), const((seq, 1)),
                  pl.BlockSpec((2 * HYENA_ORDER, ct), lambda c, b: (0, c)),
                  pl.BlockSpec((HYENA_ORDER, ct), lambda c, b: (0, c)),
                  pl.BlockSpec((2 * seq, seq), lambda c, b: (0, 0), pipeline_mode=pl.Buffered(1)),
                  pl.BlockSpec((seq, 2 * seq), lambda c, b: (0, 0), pipeline_mode=pl.Buffered(1))]
        + [w3k(k) for k in range(2 * HYENA_ORDER)] + extra_specs,
        out_specs=pl.BlockSpec((nb * seq, ct), lambda c, b: (rb0 + b, c)),
        out_shape=jax.ShapeDtypeStruct((N_TOK, D_MODEL), BF16),
        input_output_aliases=aliases,
        scratch_shapes=[pltpu.VMEM((HYENA_ORDER, seq, ct), F32),
                        pltpu.VMEM((HYENA_ORDER, seq, ct), F32),
                        pltpu.VMEM((seq, ct), BF16),
                        pltpu.VMEM((2 * seq, ct), F32),
                        pltpu.VMEM((2 * seq, ct), BF16),
                        pltpu.VMEM((seq + 2 * SUBLANE, ct), F32),
                        pltpu.VMEM((seq, ct), F32),
                        pltpu.VMEM((seq, ct), F32),
                        pltpu.VMEM((seq, ct), F32)],
        compiler_params=_cparams(("arbitrary", "arbitrary")),
        name=name,
    )(proj, proj, proj, sw, sw, sw, sb, sb, sb, h2, t01, alt, dec, p['filt_bias'], fwd, inv,
      *([w3] * (2 * HYENA_ORDER)), *extra_args)


def _moe_kernel(ie_ref, ib_ref, ir_ref, x_ref, w1_ref, w3_ref, w2_ref, o_hbm,
                acc_ref, w1b_ref, w3b_ref, w2b_ref, sem):
    w = pl.program_id(0)
    j = pl.program_id(1)
    nj = pl.num_programs(1)
    rows = ir_ref[w]
    rt = MOE_ROW_TILE
    ht = rt // 2
    nfull = rows // rt
    tail = rows - nfull * rt
    ntiles = nfull + (tail > ht).astype(jnp.int32)
    has_half = jnp.logical_and(tail > 0, tail <= ht)

    def partial_out(off, size):
        xt = _unpack_bf16_pair(x_ref[pl.ds(off, size), :])
        h1 = jnp.dot(xt, w1b_ref[...], preferred_element_type=F32)
        h3 = jnp.dot(xt, w3b_ref[...], preferred_element_type=F32)
        h = (_silu(h1) * h3).astype(BF16)
        return jnp.dot(h, w2b_ref[...], preferred_element_type=F32)

    def cast_weights():
        w1b_ref[...] = w1_ref[...].astype(BF16)
        w3b_ref[...] = w3_ref[...].astype(BF16)
        w2b_ref[...] = w2_ref[...].astype(BF16)

    @pl.when(rows > 0)
    def _():
        def run_tiles(first):
            def one(r, size=rt):
                off = r * rt if isinstance(r, int) else pl.multiple_of(r * rt, rt)
                part = partial_out(off, size)
                if first:
                    acc_ref[pl.ds(off, size), :] = part
                else:
                    acc_ref[pl.ds(off, size), :] += part

            def half_tile():
                one(ntiles, ht)

            def pair(q, _):
                one(2 * q)
                one(2 * q + 1)
                return 0

            @pl.when(ntiles >= 2)
            def _():
                cast_weights()
                one(0)
                one(1)
                lax.fori_loop(1, ntiles // 2, pair, 0)
                pl.when(ntiles % 2 == 1)(lambda: one(ntiles - 1))
                pl.when(has_half)(half_tile)

            @pl.when(ntiles == 1)
            def _():
                cast_weights()
                one(0)
                pl.when(has_half)(half_tile)

            @pl.when(ntiles == 0)
            def _():
                cast_weights()
                half_tile()

        @pl.when(j == 0)
        def _():
            run_tiles(True)

        @pl.when(j > 0)
        def _():
            run_tiles(False)

        @pl.when(j == nj - 1)
        def _():
            base = ib_ref[w] * MOE_CHUNK

            def tile_copy(off, size=rt):
                return pltpu.make_async_copy(acc_ref.at[pl.ds(off, size)],
                                             o_hbm.at[pl.ds(base + off, size)], sem)

            def start(r, _):
                tile_copy(pl.multiple_of(r * rt, rt)).start()
                return 0

            def wait(r, _):
                tile_copy(pl.multiple_of(r * rt, rt)).wait()
                return 0

            half_off = pl.multiple_of(ntiles * rt, rt)
            lax.fori_loop(0, ntiles, start, 0)
            pl.when(has_half)(lambda: tile_copy(half_off, ht).start())
            lax.fori_loop(0, ntiles, wait, 0)
            pl.when(has_half)(lambda: tile_copy(half_off, ht).wait())


def _moe_experts(xs, item_expert, item_block, item_rows, p):
    tj = MOE_FF_TILE
    nj = D_FF_EXPERT // tj

    def jeff(w, j, ir):
        return jnp.where(ir[w] > 0, j, nj - 1)

    grid_spec = pltpu.PrefetchScalarGridSpec(
        num_scalar_prefetch=3,
        grid=(MOE_ITEMS, nj),
        in_specs=[pl.BlockSpec((MOE_CHUNK, D_MODEL // 2), lambda w, j, ie, ib, ir: (ib[w], 0),
                               pipeline_mode=pl.Buffered(1)),
                  pl.BlockSpec((None, D_MODEL, tj), lambda w, j, ie, ib, ir: (ie[w], 0, jeff(w, j, ir))),
                  pl.BlockSpec((None, D_MODEL, tj), lambda w, j, ie, ib, ir: (ie[w], 0, jeff(w, j, ir))),
                  pl.BlockSpec((None, tj, D_MODEL), lambda w, j, ie, ib, ir: (ie[w], jeff(w, j, ir), 0))],
        out_specs=pl.BlockSpec(memory_space=pl.ANY),
        scratch_shapes=[pltpu.VMEM((MOE_CHUNK, D_MODEL), F32),
                        pltpu.VMEM((D_MODEL, tj), BF16),
                        pltpu.VMEM((D_MODEL, tj), BF16),
                        pltpu.VMEM((tj, D_MODEL), BF16),
                        pltpu.SemaphoreType.DMA(())],
    )
    return pl.pallas_call(
        _moe_kernel,
        grid_spec=grid_spec,
        out_shape=jax.ShapeDtypeStruct((MOE_ROWS, D_MODEL), F32),
        compiler_params=_cparams(("arbitrary", "arbitrary")),
        name="moe_experts",
    )(item_expert, item_block, item_rows, xs, p['exp_w1'], p['exp_w3'], p['exp_w2'])


ROUTE_TM = 256
R_E0, R_E1, R_P0, R_P1, R_RANK0, R_RANK1 = range(6)


def _route_kernel(lg_ref, o_ref, cnt_ref, carry_ref):
    i = pl.program_id(0)
    tm = ROUTE_TM

    @pl.when(i == 0)
    def _():
        carry_ref[...] = jnp.zeros(carry_ref.shape, F32)

    lane = lax.broadcasted_iota(jnp.int32, (tm, LANE), 1)
    lg = jnp.where(lane < N_EXPERTS, lg_ref[...], -jnp.inf)
    m0 = jnp.max(lg, -1, keepdims=True)
    e0 = jnp.min(jnp.where(lg == m0, lane, LANE), -1, keepdims=True)
    lg1 = jnp.where(lane == e0, -jnp.inf, lg)
    m1 = jnp.max(lg1, -1, keepdims=True)
    e1 = jnp.min(jnp.where(lg1 == m1, lane, LANE), -1, keepdims=True)
    t = jnp.exp(m1 - m0)
    p0 = 1.0 / (1.0 + t)
    p1 = t / (1.0 + t)
    hit = ((lane == e0) | (lane == e1)).astype(BF16)
    r_i = lax.broadcasted_iota(jnp.int32, (tm, tm), 0)
    c_i = lax.broadcasted_iota(jnp.int32, (tm, tm), 1)
    before = (c_i < r_i).astype(BF16)
    pref = jnp.dot(before, hit, preferred_element_type=F32) + carry_ref[0:1, :]
    rank0 = jnp.sum(jnp.where(lane == e0, pref, 0.0), -1, keepdims=True)
    rank1 = jnp.sum(jnp.where(lane == e1, pref, 0.0), -1, keepdims=True)
    carry_ref[0:1, :] = carry_ref[0:1, :] + jnp.sum(hit.astype(F32), axis=0, keepdims=True)
    rec = jnp.zeros((tm, LANE), F32)
    for k, v in ((R_E0, e0.astype(F32)), (R_E1, e1.astype(F32)), (R_P0, p0), (R_P1, p1),
                 (R_RANK0, rank0), (R_RANK1, rank1)):
        rec = jnp.where(lane == k, v, rec)
    o_ref[...] = rec
    cnt_ref[...] = carry_ref[...]


def _route_records(logits):
    return pl.pallas_call(
        _route_kernel,
        grid=(N_TOK // ROUTE_TM,),
        in_specs=[pl.BlockSpec((ROUTE_TM, LANE), lambda i: (i, 0))],
        out_specs=[pl.BlockSpec((ROUTE_TM, LANE), lambda i: (i, 0)),
                   pl.BlockSpec((SUBLANE, LANE), lambda i: (0, 0))],
        out_shape=[jax.ShapeDtypeStruct((N_TOK, LANE), F32),
                   jax.ShapeDtypeStruct((SUBLANE, LANE), F32)],
        scratch_shapes=[pltpu.VMEM((SUBLANE, LANE), F32)],
        compiler_params=_cparams(("arbitrary",)),
        name="moe_route",
    )(logits)


def _route(logits):
    rec, cnt = _route_records(logits)
    e_flat = rec[:, R_E0:R_E1 + 1].astype(jnp.int32).reshape(-1)
    rank = rec[:, R_RANK0:R_RANK1 + 1].astype(jnp.int32).reshape(-1)
    counts = cnt[0, :N_EXPERTS].astype(jnp.int32)
    blocks = (counts + MOE_CHUNK - 1) // MOE_CHUNK
    bend = jnp.cumsum(blocks)
    bstart = bend - blocks
    total = bend[-1]
    pos = bstart[e_flat] * MOE_CHUNK + rank
    w = jnp.arange(MOE_ITEMS, dtype=jnp.int32)
    w_eff = jnp.minimum(w, total - 1)
    item_expert = jnp.minimum(jnp.sum((w_eff[:, None] >= bend[None, :]).astype(jnp.int32), axis=1),
                              N_EXPERTS - 1)
    item_rows = jnp.where(w < total,
                          jnp.clip(counts[item_expert] - (w - bstart[item_expert]) * MOE_CHUNK, 0, MOE_CHUNK),
                          0)
    return rec, pos.reshape(N_TOK, TOP_K), item_expert.astype(jnp.int32), \
        w_eff.astype(jnp.int32), item_rows.astype(jnp.int32)


DISPATCH_TM = 256


def _dispatch_kernel(pos_ref, ib_ref, ir_ref, u_ref, o_hbm, zero_ref, sem, zsem):
    i = pl.program_id(0)
    tm = DISPATCH_TM
    rt = MOE_ROW_TILE

    @pl.when(i == 0)
    def _():
        zero_ref[...] = jnp.zeros(zero_ref.shape, zero_ref.dtype)

        def tail_copy(w):
            last_tile = (ir_ref[w] - 1) // rt
            start = pl.multiple_of(ib_ref[w] * MOE_CHUNK + last_tile * rt, rt)
            return pltpu.make_async_copy(zero_ref, o_hbm.at[pl.ds(start, rt)], zsem)

        for w in range(MOE_ITEMS):
            pl.when(ir_ref[w] > 0)(lambda w=w: tail_copy(w).start())
        for w in range(MOE_ITEMS):
            pl.when(ir_ref[w] > 0)(lambda w=w: tail_copy(w).wait())

    def row_copy(r, dst_row):
        return pltpu.make_async_copy(u_ref.at[pl.ds(r, 1)], o_hbm.at[pl.ds(dst_row, 1)], sem)

    def start(r, _):
        a = (i * tm + r) * TOP_K
        for k in range(TOP_K):
            row_copy(r, pos_ref[a + k]).start(priority=k)
        return 0

    def wait(r, _):
        row_copy(0, 0).wait()
        return 0

    lax.fori_loop(0, tm, start, 0, unroll=8)
    lax.fori_loop(0, tm * TOP_K, wait, 0, unroll=8)


def _dispatch(pos_flat, item_block, item_rows, u):
    width = u.shape[1]
    grid_spec = pltpu.PrefetchScalarGridSpec(
        num_scalar_prefetch=3,
        grid=(N_TOK // DISPATCH_TM,),
        in_specs=[pl.BlockSpec((DISPATCH_TM, width), lambda i, pos, ib, ir: (i, 0))],
        out_specs=pl.BlockSpec(memory_space=pl.ANY),
        scratch_shapes=[pltpu.VMEM((MOE_ROW_TILE, width), u.dtype),
                        pltpu.SemaphoreType.DMA(()),
                        pltpu.SemaphoreType.DMA(())],
    )
    return pl.pallas_call(
        _dispatch_kernel,
        grid_spec=grid_spec,
        out_shape=jax.ShapeDtypeStruct((MOE_ROWS, width), u.dtype),
        compiler_params=_cparams(("arbitrary",)),
        name="moe_dispatch",
    )(pos_flat, item_block, item_rows, u)


def _postnorm_kernel(x_ref, y0_ref, y1_ref, rec_ref, gate_ref, g_ref, b_ref, op_ref, os_ref):
    i = pl.program_id(0)
    n_p = TOK_P // x_ref.shape[0]
    delta = (rec_ref[:, R_P0:R_P0 + 1] * y0_ref[...] + rec_ref[:, R_P1:R_P1 + 1] * y1_ref[...])
    y = DN_ALPHA * x_ref[...] + gate_ref[0] * delta
    out = _layer_norm_rows(y, g_ref[...], b_ref[...])

    @pl.when(i < n_p)
    def _():
        op_ref[...] = out

    @pl.when(i >= n_p)
    def _():
        os_ref[...] = out


def _postnorm(x, y0, y1, rec, mods, gate_chunk, g, b):
    tm = 256
    vec = pl.BlockSpec((1, D_MODEL), lambda i: (0, 0))
    rows = pl.BlockSpec((tm, D_MODEL), lambda i: (i, 0))
    return pl.pallas_call(
        _postnorm_kernel,
        grid=(N_TOK // tm,),
        in_specs=[rows, rows, rows, pl.BlockSpec((tm, LANE), lambda i: (i, 0)),
                  _mod_spec(tm, gate_chunk), vec, vec],
        out_specs=_token_row_specs(2, tm, D_MODEL),
        out_shape=[jax.ShapeDtypeStruct((TOK_P, D_MODEL), F32),
                   jax.ShapeDtypeStruct((TOK_S, D_MODEL), F32)],
        compiler_params=_cparams(("arbitrary",)),
        name="final_postnorm",
    )(x, y0, y1, rec, mods, g.reshape(1, D_MODEL), b.reshape(1, D_MODEL))


def kernel(x_prompt, x_sample, c, c_ctx, cache_l0_k, cache_l0_v, state_l0_lru, l0_ada_w, l0_ada_b, l0_w_in, l0_q_norm, l0_k_norm, l0_lru_conv_w, l0_lru_conv_b, l0_lru_lambda, l0_lru_w_r, l0_lru_b_r, l0_lru_w_i, l0_lru_b_i, l0_w_out, l0_ln1_g, l0_ln1_b, l0_ffn_w1, l0_ffn_w3, l0_ffn_w2, l0_ln2_g, l0_ln2_b, l1_ada_w, l1_ada_b, l1_w_in, l1_short_w, l1_short_b, l1_filt_w1, l1_filt_b1, l1_filt_f1, l1_filt_w2, l1_filt_b2, l1_filt_f2, l1_filt_w3, l1_filt_decay, l1_filt_bias, l1_w_out, l1_ln1_g, l1_ln1_b, l1_router, l1_exp_w1, l1_exp_w3, l1_exp_w2, l1_ln2_g, l1_ln2_b):
    x_in = [x_prompt.reshape(TOK_P, D_MODEL), x_sample.reshape(TOK_S, D_MODEL)]
    cond = jnp.concatenate([c_ctx[None, :], c, jnp.zeros((N_COND - 1 - DEC_BATCH, D_MODEL), F32)], axis=0)
    mods0 = _ada(cond, l0_ada_w, l0_ada_b)
    mods1 = _ada(cond, l1_ada_w, l1_ada_b)

    lru_p = dict(conv_w=l0_lru_conv_w, conv_b=l0_lru_conv_b, lam=l0_lru_lambda,
                 w_r=l0_lru_w_r, b_r=l0_lru_b_r, w_i=l0_lru_w_i, b_i=l0_lru_b_i)
    proj0 = _proj(x_in, mods0, l0_w_in, tn=512, name="l0_in_proj")
    attn, new_k, new_v = _attn_context(proj0, l0_q_norm, l0_k_norm)
    attn = _attn_latent(proj0, cache_l0_k, cache_l0_v, l0_q_norm, l0_k_norm, attn)
    lru, new_h = _lru(proj0, jnp.zeros((BATCH, 2, LRU_WIDTH), F32), lru_p,
                      seq=SEQ, batch=BATCH, row0=0, name="lru_context")
    lru, _ = _lru(proj0, state_l0_lru, lru_p,
                  seq=DEC_SEQ, batch=DEC_BATCH, row0=TOK_P, name="lru_latent", dst=lru)
    x = _outproj([attn, lru], l0_w_out, x_in, mods0, 2, l0_ln1_g, l0_ln1_b, name="l0_out_proj")[0]
    x = _ffn(x, mods0, dict(ffn_w1=l0_ffn_w1, ffn_w3=l0_ffn_w3, ffn_w2=l0_ffn_w2,
                            ln2_g=l0_ln2_g, ln2_b=l0_ln2_b))

    hy_p = dict(short_w=l1_short_w, short_b=l1_short_b, filt_w1=l1_filt_w1, filt_b1=l1_filt_b1,
                filt_f1=l1_filt_f1, filt_w2=l1_filt_w2, filt_b2=l1_filt_b2, filt_f2=l1_filt_f2,
                filt_w3=l1_filt_w3, filt_decay=l1_filt_decay, filt_bias=l1_filt_bias)
    proj1 = _proj([x], mods1, l1_w_in, tn=512, name="l1_in_proj")
    z = _hyena(proj1, hy_p, seq=SEQ, batch=BATCH, row0=0, nb=2, ct=1024, name="hyena_context")
    z = _hyena(proj1, hy_p, seq=DEC_SEQ, batch=DEC_BATCH, row0=TOK_P, nb=1, ct=512, name="hyena_latent",
               dst=z)
    router = jnp.pad(l1_router, ((0, 0), (0, LANE - N_EXPERTS)))
    x, u, logits = _outproj([z], l1_w_out, [x], mods1, 2, l1_ln1_g, l1_ln1_b, name="l1_out_proj",
                            router=router, router_chunks=(3, 4))
    rec, pos, item_expert, item_block, item_rows = _route(logits)
    xs = _dispatch(pos.reshape(-1), item_block, item_rows, u)
    ys = _moe_experts(xs, item_expert, item_block, item_rows,
                      dict(exp_w1=l1_exp_w1, exp_w3=l1_exp_w3, exp_w2=l1_exp_w2))
    y_prompt, y_sample = _postnorm(x, _rows(ys, pos[:, 0]), _rows(ys, pos[:, 1]), rec,
                                   mods1, 5, l1_ln2_g, l1_ln2_b)
    return (y_prompt.reshape(BATCH, SEQ, D_MODEL), y_sample.reshape(DEC_BATCH, DEC_SEQ, D_MODEL),
            new_k.reshape(BATCH, SEQ, N_KV_HEADS, HEAD_DIM),
            new_v.reshape(BATCH, SEQ, N_KV_HEADS, HEAD_DIM),
            new_h)
```

```python
import functools
import math

import jax
import jax.numpy as jnp
from jax import lax
from jax.experimental import pallas as pl
from jax.experimental.pallas import tpu as pltpu

F32 = jnp.float32
BF16 = jnp.bfloat16

D_MODEL = 2048
BATCH = 16
SEQ = 256
DEC_BATCH = 2
DEC_SEQ = 1024
PAST_LEN = 256
GRID_W = 64
HEAD_DIM = 128
N_Q_HEADS = 8
N_KV_HEADS = 2
Q_GROUP = N_Q_HEADS // N_KV_HEADS
Q_WIDTH = N_Q_HEADS * HEAD_DIM
KV_WIDTH = N_KV_HEADS * HEAD_DIM
ROPE_THETA = 10000.0
LRU_WIDTH = D_MODEL // 2
LRU_BLOCK = 128
LRU_C = 8.0
MIX0_IN = Q_WIDTH + 2 * KV_WIDTH + 2 * LRU_WIDTH
HYENA_ORDER = 2
HYENA_EMB = 33
HYENA_BANDS = (HYENA_EMB - 1) // 2
HYENA_FILTER_HIDDEN = 64
D_FF = 5632
N_EXPERTS = 8
TOP_K = 2
D_FF_EXPERT = 7168
N_MOD = 6
LN_EPS = 1e-5
QK_EPS = 1e-6
DEPTH = 2
DN_ALPHA = (2 * DEPTH) ** 0.25

TOK_P = BATCH * SEQ
TOK_S = DEC_BATCH * DEC_SEQ
N_TOK = TOK_P + TOK_S
N_COND = 8
LANE = 128
SUBLANE = 8
VMEM_LIMIT = 56 * 1024 * 1024

ROW_TILE = 1024
MOE_CHUNK = 2048
MOE_ROW_TILE = 256
MOE_FF_TILE = 256
MOE_ITEMS = N_EXPERTS + (N_TOK * TOP_K) // MOE_CHUNK
MOE_ROWS = MOE_ITEMS * MOE_CHUNK


def _cparams(sem):
    return pltpu.CompilerParams(dimension_semantics=sem, vmem_limit_bytes=VMEM_LIMIT)


def _cond_of_tile(i, tm):
    return jnp.maximum(i * tm // DEC_SEQ - (TOK_P // DEC_SEQ - 1), 0)


def _mod_spec(tm, chunk, width=D_MODEL, col_of=None):
    per = D_MODEL // width
    if col_of is None:
        return pl.BlockSpec((1, 1, width), lambda i, *_: (_cond_of_tile(i, tm), 0, chunk * per))
    return pl.BlockSpec((1, 1, width),
                        lambda i, j, *_: (_cond_of_tile(i, tm), 0, chunk * per + col_of(j)))


def _silu(x):
    return x * jax.nn.sigmoid(x)


def _split_bf16(x):
    hi = x.astype(BF16)
    lo = (x - hi.astype(F32)).astype(BF16)
    return hi, lo


def _pack_bf16_pair(lo, hi):
    lo_bits = lax.bitcast_convert_type(lo.astype(BF16).astype(F32), jnp.uint32) >> 16
    hi_bits = lax.bitcast_convert_type(hi.astype(BF16).astype(F32), jnp.uint32) & jnp.uint32(0xFFFF0000)
    return hi_bits | lo_bits


def _unpack_bf16_pair(words):
    lo = lax.bitcast_convert_type(words << 16, F32)
    hi = lax.bitcast_convert_type(words & jnp.uint32(0xFFFF0000), F32)
    return jnp.concatenate([lo, hi], axis=1).astype(BF16)


def _rows(a, idx):
    return a.at[idx].get(mode="promise_in_bounds")


def _dot3(a, b):
    ah, al = _split_bf16(a)
    bh, bl = _split_bf16(b)
    d = lambda x, y: jnp.dot(x, y, preferred_element_type=F32)
    return d(ah, bh) + (d(ah, bl) + d(al, bh))


def _layer_norm_rows(y, g, b):
    mu = jnp.mean(y, -1, keepdims=True)
    yc = y - mu
    var = jnp.mean(yc * yc, -1, keepdims=True)
    return yc * lax.rsqrt(var + LN_EPS) * g + b


def _ada_kernel(c_ref, w_ref, b_ref, o_ref):
    s = _silu(c_ref[...]).astype(BF16)
    o_ref[...] = jnp.dot(s, w_ref[...].astype(BF16), preferred_element_type=F32) + b_ref[...]


def _ada(cond, w, b):
    tn = 1024
    n = w.shape[1]
    out = pl.pallas_call(
        _ada_kernel,
        grid=(n // tn,),
        in_specs=[pl.BlockSpec((N_COND, D_MODEL), lambda j: (0, 0)),
                  pl.BlockSpec((D_MODEL, tn), lambda j: (0, j)),
                  pl.BlockSpec((1, tn), lambda j: (0, j))],
        out_specs=pl.BlockSpec((N_COND, tn), lambda j: (0, j)),
        out_shape=jax.ShapeDtypeStruct((N_COND, n), F32),
        compiler_params=_cparams(("arbitrary",)),
        name="ada_modulation",
    )(cond, w, b.reshape(1, n))
    return out.reshape(N_COND, 1, n)


def _cast_kernel(x_ref, o_ref):
    o_ref[...] = x_ref[...].astype(o_ref.dtype)


def _cast_bf16(w, rows, name):
    r, c = w.shape
    return pl.pallas_call(
        _cast_kernel,
        grid=(r // rows,),
        in_specs=[pl.BlockSpec((rows, c), lambda i: (i, 0))],
        out_specs=pl.BlockSpec((rows, c), lambda i: (i, 0)),
        out_shape=jax.ShapeDtypeStruct((r, c), BF16),
        compiler_params=_cparams(("arbitrary",)),
        name=name,
    )(w)


def _token_row_specs(n_parts, tm, width, col=None, buffered_once=False):
    kw = dict(pipeline_mode=pl.Buffered(1)) if buffered_once else {}
    col = col or (lambda i, *a: 0)
    if n_parts == 1:
        return [pl.BlockSpec((tm, width), lambda i, *a: (i, col(i, *a)), **kw)]
    n_p = TOK_P // tm
    return [pl.BlockSpec((tm, width), lambda i, *a: (jnp.minimum(i, n_p - 1), col(i, *a)), **kw),
            pl.BlockSpec((tm, width), lambda i, *a: (jnp.maximum(i - n_p, 0), col(i, *a)), **kw)]


def _proj_kernel(*refs, n_x):
    x_refs = refs[:n_x]
    sh_ref, sc_ref, w_ref, o_ref, u_ref, wb_ref = refs[n_x:]
    i = pl.program_id(0)
    j = pl.program_id(1)

    def modulate_from(x_ref):
        u_ref[...] = (x_ref[...] * (1.0 + sc_ref[0]) + sh_ref[0]).astype(BF16)

    @pl.when(j == 0)
    def _():
        if n_x == 1:
            modulate_from(x_refs[0])
        else:
            n_p = TOK_P // u_ref.shape[0]
            pl.when(i < n_p)(lambda: modulate_from(x_refs[0]))
            pl.when(i >= n_p)(lambda: modulate_from(x_refs[1]))

    @pl.when(i == 0)
    def _():
        wb_ref[j] = w_ref[...].astype(BF16)

    o_ref[...] = jnp.dot(u_ref[...], wb_ref[j], preferred_element_type=F32)


def _first_pass_block(i, j, nj):
    return jnp.where(i == 0, j, nj - 1)


def _proj(xs, mods, w, *, tn, name):
    tm = ROW_TILE
    n = w.shape[1]
    nj = n // tn
    return pl.pallas_call(
        functools.partial(_proj_kernel, n_x=len(xs)),
        grid=(N_TOK // tm, nj),
        in_specs=_token_row_specs(len(xs), tm, D_MODEL, buffered_once=True) + [
            _mod_spec(tm, 0), _mod_spec(tm, 1),
            pl.BlockSpec((D_MODEL, tn), lambda i, j: (0, _first_pass_block(i, j, nj)))],
        out_specs=pl.BlockSpec((tm, tn), lambda i, j: (i, j)),
        out_shape=jax.ShapeDtypeStruct((N_TOK, n), F32),
        scratch_shapes=[pltpu.VMEM((tm, D_MODEL), BF16),
                        pltpu.VMEM((nj, D_MODEL, tn), BF16)],
        compiler_params=_cparams(("arbitrary", "arbitrary")),
        name=name,
    )(*xs, mods, mods, w)


def _rms(x, g):
    return x * lax.rsqrt(jnp.mean(x * x, -1, keepdims=True) + QK_EPS) * g


def _dot_nt(a, b):
    return lax.dot_general(a, b, (((1,), (1,)), ((), ())), preferred_element_type=F32)


def _rope(x, cos, sin_signed):
    lane = lax.broadcasted_iota(jnp.int32, x.shape, 1)
    partner = jnp.where(lane % 2 == 0, pltpu.roll(x, HEAD_DIM - 1, 1), pltpu.roll(x, 1, 1))
    return x * cos + partner * sin_signed


def _attn_ctx_kernel(q_ref, k_ref, v_ref, qn_ref, kn_ref, o_ref, ko_ref, vo_ref):
    scale = HEAD_DIM ** -0.5
    kn = _rms(k_ref[...], kn_ref[...])
    v = v_ref[...]
    ko_ref[...] = kn
    vo_ref[...] = v
    kb = kn.astype(BF16)
    vb = v.astype(BF16)
    for g in range(Q_GROUP):
        cols = slice(g * HEAD_DIM, (g + 1) * HEAD_DIM)
        q = _rms(q_ref[:, cols], qn_ref[...]).astype(BF16)
        s = _dot_nt(q, kb) * scale
        p = jnp.exp(s - jnp.max(s, -1, keepdims=True))
        p = p / jnp.sum(p, -1, keepdims=True)
        o = jnp.dot(p.astype(BF16), vb, preferred_element_type=F32)
        o_ref[:, cols] = o.astype(o_ref.dtype)


def _attn_context(proj, q_norm, k_norm):
    qw = Q_GROUP * HEAD_DIM
    k_blk0 = Q_WIDTH // HEAD_DIM
    v_blk0 = (Q_WIDTH + KV_WIDTH) // HEAD_DIM
    vec = pl.BlockSpec((1, HEAD_DIM), lambda b, h: (0, 0))
    return pl.pallas_call(
        _attn_ctx_kernel,
        grid=(BATCH, N_KV_HEADS),
        in_specs=[pl.BlockSpec((SEQ, qw), lambda b, h: (b, h)),
                  pl.BlockSpec((SEQ, HEAD_DIM), lambda b, h: (b, k_blk0 + h)),
                  pl.BlockSpec((SEQ, HEAD_DIM), lambda b, h: (b, v_blk0 + h)),
                  vec, vec],
        out_specs=[pl.BlockSpec((SEQ, qw), lambda b, h: (b, h)),
                   pl.BlockSpec((SEQ, HEAD_DIM), lambda b, h: (b, h)),
                   pl.BlockSpec((SEQ, HEAD_DIM), lambda b, h: (b, h))],
        out_shape=[jax.ShapeDtypeStruct((N_TOK, Q_WIDTH), BF16),
                   jax.ShapeDtypeStruct((TOK_P, KV_WIDTH), F32),
                   jax.ShapeDtypeStruct((TOK_P, KV_WIDTH), F32)],
        compiler_params=_cparams(("arbitrary", "arbitrary")),
        name="attn_context",
    )(proj, proj, proj, q_norm.reshape(1, HEAD_DIM), k_norm.reshape(1, HEAD_DIM))


ATTN_Q_ROWS = 256


def _attn_lat_kernel(q_ref, k_ref, v_ref, ck_ref, cv_ref, qn_ref, kn_ref,
                     cq_ref, sq_ref, ck_tab_ref, sk_tab_ref, dst_ref, o_ref):
    del dst_ref
    scale = HEAD_DIM ** -0.5
    kb = _rope(_rms(k_ref[...], kn_ref[...]), ck_tab_ref[...], sk_tab_ref[...]).astype(BF16)
    vb = v_ref[...].astype(BF16)
    ckb = ck_ref[...].astype(BF16)
    cvb = cv_ref[...].astype(BF16)
    for g in range(Q_GROUP):
        cols = slice(g * HEAD_DIM, (g + 1) * HEAD_DIM)
        q = _rope(_rms(q_ref[:, cols], qn_ref[...]), cq_ref[...], sq_ref[...]).astype(BF16)
        s1 = _dot_nt(q, ckb) * scale
        s2 = _dot_nt(q, kb) * scale
        m = jnp.maximum(jnp.max(s1, -1, keepdims=True), jnp.max(s2, -1, keepdims=True))
        p1 = jnp.exp(s1 - m)
        p2 = jnp.exp(s2 - m)
        den = jnp.sum(p1, -1, keepdims=True) + jnp.sum(p2, -1, keepdims=True)
        o = (jnp.dot((p1 / den).astype(BF16), cvb, preferred_element_type=F32)
             + jnp.dot((p2 / den).astype(BF16), vb, preferred_element_type=F32))
        o_ref[:, cols] = o.astype(o_ref.dtype)


def _rope_tables():
    t = jnp.arange(DEC_SEQ)
    row = (t // GRID_W).astype(F32)
    col = (t % GRID_W).astype(F32)
    n_freq = HEAD_DIM // 4
    inv = 1.0 / (ROPE_THETA ** (jnp.arange(n_freq, dtype=F32) / n_freq))
    ang = jnp.concatenate([row[:, None] * inv, col[:, None] * inv], -1)
    cos = jnp.repeat(jnp.cos(ang), 2, axis=-1)
    sign = jnp.where(jnp.arange(HEAD_DIM) % 2 == 0, -1.0, 1.0).astype(F32)
    sin_signed = jnp.repeat(jnp.sin(ang), 2, axis=-1) * sign
    return cos, sin_signed


_IN_PLACE = pl.BlockSpec(memory_space=pl.ANY)


def _attn_latent(proj, cache_k, cache_v, q_norm, k_norm, dst):
    qw = Q_GROUP * HEAD_DIM
    nq = DEC_SEQ // ATTN_Q_ROWS
    q_row0 = TOK_P // ATTN_Q_ROWS
    kv_row0 = TOK_P // DEC_SEQ
    k_blk0 = Q_WIDTH // HEAD_DIM
    v_blk0 = (Q_WIDTH + KV_WIDTH) // HEAD_DIM
    cos, sin_signed = _rope_tables()
    vec = pl.BlockSpec((1, HEAD_DIM), lambda b, h, c: (0, 0))
    tab_q = pl.BlockSpec((ATTN_Q_ROWS, HEAD_DIM), lambda b, h, c: (c, 0))
    tab_k = pl.BlockSpec((DEC_SEQ, HEAD_DIM), lambda b, h, c: (0, 0))
    ctx = pl.BlockSpec((PAST_LEN, HEAD_DIM), lambda b, h, c: (b, h))
    return pl.pallas_call(
        _attn_lat_kernel,
        grid=(DEC_BATCH, N_KV_HEADS, nq),
        in_specs=[pl.BlockSpec((ATTN_Q_ROWS, qw), lambda b, h, c: (q_row0 + b * nq + c, h)),
                  pl.BlockSpec((DEC_SEQ, HEAD_DIM), lambda b, h, c: (kv_row0 + b, k_blk0 + h)),
                  pl.BlockSpec((DEC_SEQ, HEAD_DIM), lambda b, h, c: (kv_row0 + b, v_blk0 + h)),
                  ctx, ctx, vec, vec, tab_q, tab_q, tab_k, tab_k, _IN_PLACE],
        out_specs=pl.BlockSpec((ATTN_Q_ROWS, qw), lambda b, h, c: (q_row0 + b * nq + c, h)),
        out_shape=jax.ShapeDtypeStruct((N_TOK, Q_WIDTH), BF16),
        input_output_aliases={11: 0},
        compiler_params=_cparams(("arbitrary", "arbitrary", "arbitrary")),
        name="attn_latent",
    )(proj, proj, proj,
      cache_k.reshape(DEC_BATCH * PAST_LEN, KV_WIDTH), cache_v.reshape(DEC_BATCH * PAST_LEN, KV_WIDTH),
      q_norm.reshape(1, HEAD_DIM), k_norm.reshape(1, HEAD_DIM), cos, sin_signed, cos, sin_signed, dst)


LRU_CT = 512
assert (Q_WIDTH + 2 * KV_WIDTH) % LRU_CT == 0 and LRU_WIDTH % LRU_CT == 0


def _sigmoid_tanh(x):
    return 0.5 * (jnp.tanh(0.5 * x) + 1.0)
LRU_RC = 128


def _softplus(x):
    return jnp.maximum(x, 0.0) + jnp.log1p(jnp.exp(-jnp.abs(x)))


def _gelu_tanh(x):
    return 0.5 * x * (1.0 + jnp.tanh(math.sqrt(2.0 / math.pi) * (x + 0.044715 * (x * x * x))))


def _lru_kernel(*refs, seq, has_dst):
    xb_ref, gb_ref, cw_ref, cb_ref, lam_ref, wr_ref, br_ref, wi_ref, bi_ref, h0_ref = refs[:10]
    y_ref, hfin_ref, xpad_ref, a_ref, b_ref, hf_ref, hb_ref = refs[10 + int(has_dst):]
    ct = LRU_CT
    zeros = jnp.zeros((SUBLANE, ct), F32)
    xpad_ref[0:SUBLANE, :] = zeros
    xpad_ref[seq + SUBLANE:seq + 2 * SUBLANE, :] = zeros
    xpad_ref[SUBLANE:seq + SUBLANE, :] = xb_ref[...]
    cw = cw_ref[...]
    sp = _softplus(-lam_ref[...])
    for r0 in range(0, seq, LRU_RC):
        xc = cb_ref[...] + cw[0:1, :] * xpad_ref[r0 + 7:r0 + 7 + LRU_RC, :]
        for w in range(1, 4):
            xc = xc + cw[w:w + 1, :] * xpad_ref[r0 + 7 + w:r0 + 7 + w + LRU_RC, :]
        for kb in range(ct // LRU_BLOCK):
            cols = slice(kb * LRU_BLOCK, (kb + 1) * LRU_BLOCK)
            xk = xc[:, cols]
            xkb = xk.astype(BF16)
            for d in range(2):
                r = _sigmoid_tanh(jnp.dot(xkb, wr_ref[d, kb].astype(BF16), preferred_element_type=F32)
                                  + br_ref[d:d + 1, cols])
                i = _sigmoid_tanh(jnp.dot(xkb, wi_ref[d, kb].astype(BF16), preferred_element_type=F32)
                                  + bi_ref[d:d + 1, cols])
                log_a = -LRU_C * r * sp[d:d + 1, cols]
                a = jnp.exp(log_a)
                a_ref[d, r0:r0 + LRU_RC, cols] = a
                b_ref[d, r0:r0 + LRU_RC, cols] = jnp.sqrt(-jnp.tanh(log_a) * (a * a + 1.0)) * (i * xk)

    nblk = seq // SUBLANE
    row = lax.broadcasted_iota(jnp.int32, (SUBLANE, ct), 0)

    def body(i, carry):
        hf, hb = carry
        rf = pl.multiple_of(i * SUBLANE, SUBLANE)
        a = a_ref[0, pl.ds(rf, SUBLANE), :]
        b = b_ref[0, pl.ds(rf, SUBLANE), :]
        for s in (1, 2, 4):
            m = row >= s
            a_s = jnp.where(m, pltpu.roll(a, s, 0), 1.0)
            b_s = jnp.where(m, pltpu.roll(b, s, 0), 0.0)
            b = a * b_s + b
            a = a * a_s
        hblk = a * hf + b
        hf_ref[pl.ds(rf, SUBLANE), :] = hblk
        hf = hblk[SUBLANE - 1:SUBLANE, :]

        rb = pl.multiple_of((nblk - 1 - i) * SUBLANE, SUBLANE)
        a = a_ref[1, pl.ds(rb, SUBLANE), :]
        b = b_ref[1, pl.ds(rb, SUBLANE), :]
        for s in (1, 2, 4):
            m = row < SUBLANE - s
            a_s = jnp.where(m, pltpu.roll(a, SUBLANE - s, 0), 1.0)
            b_s = jnp.where(m, pltpu.roll(b, SUBLANE - s, 0), 0.0)
            b = a * b_s + b
            a = a * a_s
        hblk = a * hb + b
        hb_ref[pl.ds(rb, SUBLANE), :] = hblk
        hb = hblk[0:1, :]
        return hf, hb

    hf, hb = lax.fori_loop(0, nblk, body, (h0_ref[0:1, :], h0_ref[1:2, :]))
    hfin_ref[0:1, :] = hf
    hfin_ref[1:2, :] = hb
    for r0 in range(0, seq, LRU_RC):
        rows = slice(r0, r0 + LRU_RC)
        y = (hf_ref[rows, :] + hb_ref[rows, :]) * _gelu_tanh(gb_ref[rows, :])
        y_ref[rows, :] = y.astype(y_ref.dtype)


def _lru(proj, h0, p, *, seq, batch, row0, name, dst=None):
    ct = LRU_CT
    nkb = ct // LRU_BLOCK
    xb_blk0 = (Q_WIDTH + 2 * KV_WIDTH) // ct
    gb_blk0 = (Q_WIDTH + 2 * KV_WIDTH + LRU_WIDTH) // ct
    r0 = row0 // seq
    vec2 = pl.BlockSpec((2, ct), lambda b, c: (0, c))
    wblk = pl.BlockSpec((2, nkb, LRU_BLOCK, LRU_BLOCK), lambda b, c: (0, c, 0, 0))
    extra_specs, extra_args, aliases = [], [], {}
    if dst is not None:
        extra_specs, extra_args, aliases = [_IN_PLACE], [dst], {10: 0}
    return pl.pallas_call(
        functools.partial(_lru_kernel, seq=seq, has_dst=dst is not None),
        grid=(batch, LRU_WIDTH // ct),
        in_specs=[pl.BlockSpec((seq, ct), lambda b, c: (r0 + b, xb_blk0 + c)),
                  pl.BlockSpec((seq, ct), lambda b, c: (r0 + b, gb_blk0 + c)),
                  pl.BlockSpec((4, ct), lambda b, c: (0, c)),
                  pl.BlockSpec((1, ct), lambda b, c: (0, c)),
                  vec2, wblk, vec2, wblk, vec2,
                  pl.BlockSpec((None, 2, ct), lambda b, c: (b, 0, c))] + extra_specs,
        out_specs=[pl.BlockSpec((seq, ct), lambda b, c: (r0 + b, c)),
                   pl.BlockSpec((None, 2, ct), lambda b, c: (b, 0, c))],
        out_shape=[jax.ShapeDtypeStruct((N_TOK, LRU_WIDTH), BF16),
                   jax.ShapeDtypeStruct((batch, 2, LRU_WIDTH), F32)],
        input_output_aliases=aliases,
        scratch_shapes=[pltpu.VMEM((seq + 2 * SUBLANE, ct), F32),
                        pltpu.VMEM((2, seq, ct), F32),
                        pltpu.VMEM((2, seq, ct), F32),
                        pltpu.VMEM((seq, ct), F32),
                        pltpu.VMEM((seq, ct), F32)],
        compiler_params=_cparams(("arbitrary", "arbitrary")),
        name=name,
    )(proj, proj, p['conv_w'], p['conv_b'].reshape(1, LRU_WIDTH), p['lam'],
      p['w_r'], p['b_r'], p['w_i'], p['b_i'], h0, *extra_args)


OUT_TN = 1024
OUT_TM = 512


def _outproj_kernel(*refs, n_a, n_res, with_router):
    a_refs = refs[:n_a]
    w_refs = refs[n_a:2 * n_a]
    res_refs = refs[2 * n_a:2 * n_a + n_res]
    gate_ref, g_ref, b_ref = refs[2 * n_a + n_res:2 * n_a + n_res + 3]
    pos = 2 * n_a + n_res + 3
    if with_router:
        sh_ref, sc_ref, rt_ref = refs[pos:pos + 3]
        pos += 3
        o_ref, u_ref, lg_ref, acc_ref = refs[pos:pos + 4]
        pos += 4
    else:
        o_ref, acc_ref = refs[pos:pos + 2]
        pos += 2
    wb_refs = refs[pos:pos + n_a]
    i = pl.program_id(0)
    j = pl.program_id(1)
    nj = pl.num_programs(1)

    @pl.when(i == 0)
    def _():
        for k in range(n_a):
            wb_refs[k][j] = w_refs[k][...].astype(BF16)

    acc = jnp.dot(a_refs[0][...], wb_refs[0][j], preferred_element_type=F32)
    for k in range(1, n_a):
        acc = acc + jnp.dot(a_refs[k][...], wb_refs[k][j], preferred_element_type=F32)
    if n_res == 1:
        res = res_refs[0][...]
    else:
        n_p = TOK_P // acc_ref.shape[1]
        res = jnp.where(i < n_p, res_refs[0][...], res_refs[1][...])
    acc_ref[j] = DN_ALPHA * res + gate_ref[0] * acc

    @pl.when(j == nj - 1)
    def _():
        tm = acc_ref.shape[1]
        n_chunks = acc_ref.shape[0]
        rc = 128

        def chunk(c, _):
            r0 = pl.multiple_of(c * rc, rc)
            ys = [acc_ref[k, pl.ds(r0, rc), :] for k in range(n_chunks)]
            tot = ys[0].sum(-1, keepdims=True)
            for y in ys[1:]:
                tot = tot + y.sum(-1, keepdims=True)
            mu = tot / D_MODEL
            sq = None
            for y in ys:
                t = ((y - mu) * (y - mu)).sum(-1, keepdims=True)
                sq = t if sq is None else sq + t
            rstd = lax.rsqrt(sq / D_MODEL + LN_EPS)
            lg = None
            us = []
            for k, y in enumerate(ys):
                cols = slice(k * OUT_TN, (k + 1) * OUT_TN)
                xn = (y - mu) * rstd * g_ref[:, cols] + b_ref[:, cols]
                o_ref[pl.ds(r0, rc), cols] = xn
                if with_router:
                    u = xn * (1.0 + sc_ref[0][:, cols]) + sh_ref[0][:, cols]
                    us.append(u)
                    t = _dot3(u, rt_ref[cols, :])
                    lg = t if lg is None else lg + t
            if with_router:
                lg_ref[pl.ds(r0, rc), :] = lg
                half = n_chunks // 2
                for k in range(half):
                    cols = slice(k * OUT_TN, (k + 1) * OUT_TN)
                    u_ref[pl.ds(r0, rc), cols] = _pack_bf16_pair(us[k], us[k + half])
            return 0

        lax.fori_loop(0, tm // rc, chunk, 0)


def _outproj(a_list, w, res, mods, gate_chunk, g, b, *, name, router=None, router_chunks=None):
    tm = OUT_TM
    tn = OUT_TN
    nj = D_MODEL // tn
    n_a = len(a_list)
    in_specs = []
    k0 = 0
    w_specs = []
    wb_shapes = []
    for a in a_list:
        ka = a.shape[1]
        in_specs.append(pl.BlockSpec((tm, ka), lambda i, j: (i, 0)))
        w_specs.append(pl.BlockSpec((ka, tn),
                                    lambda i, j, blk=k0 // ka: (blk, _first_pass_block(i, j, nj))))
        wb_shapes.append(pltpu.VMEM((nj, ka, tn), BF16))
        k0 += ka
    in_specs += w_specs
    in_specs += _token_row_specs(len(res), tm, tn, col=lambda i, j: j)
    in_specs += [_mod_spec(tm, gate_chunk, tn, col_of=lambda j: j),
                 pl.BlockSpec((1, D_MODEL), lambda i, j: (0, 0)),
                 pl.BlockSpec((1, D_MODEL), lambda i, j: (0, 0))]
    args = list(a_list) + [w] * n_a + list(res) + [mods, g.reshape(1, D_MODEL), b.reshape(1, D_MODEL)]
    out_specs = [pl.BlockSpec((tm, D_MODEL), lambda i, j: (i, 0))]
    out_shape = [jax.ShapeDtypeStruct((N_TOK, D_MODEL), F32)]
    if router is not None:
        in_specs += [_mod_spec(tm, router_chunks[0]), _mod_spec(tm, router_chunks[1]),
                     pl.BlockSpec((D_MODEL, LANE), lambda i, j: (0, 0))]
        args += [mods, mods, router]
        out_specs += [pl.BlockSpec((tm, D_MODEL // 2), lambda i, j: (i, 0)),
                      pl.BlockSpec((tm, LANE), lambda i, j: (i, 0))]
        out_shape += [jax.ShapeDtypeStruct((N_TOK, D_MODEL // 2), jnp.uint32),
                      jax.ShapeDtypeStruct((N_TOK, LANE), F32)]
    return pl.pallas_call(
        functools.partial(_outproj_kernel, n_a=n_a, n_res=len(res), with_router=router is not None),
        grid=(N_TOK // tm, nj),
        in_specs=in_specs,
        out_specs=out_specs,
        out_shape=out_shape,
        scratch_shapes=[pltpu.VMEM((nj, tm, tn), F32)] + wb_shapes,
        compiler_params=_cparams(("arbitrary", "arbitrary")),
        name=name,
    )(*args)


FFN_TF = 512


def _ffn_kernel(x_ref, sh_ref, sc_ref, gate_ref, g_ref, b_ref, w1_ref, w3_ref, w2_ref, o_ref, u_ref):
    j = pl.program_id(1)
    nj = pl.num_programs(1)

    @pl.when(j == 0)
    def _():
        u_ref[...] = (x_ref[...] * (1.0 + sc_ref[0]) + sh_ref[0]).astype(BF16)
        o_ref[...] = jnp.zeros(o_ref.shape, F32)

    u = u_ref[...]
    h1 = jnp.dot(u, w1_ref[...], preferred_element_type=F32)
    h3 = jnp.dot(u, w3_ref[...], preferred_element_type=F32)
    h = (_silu(h1) * h3).astype(BF16)
    o_ref[...] += jnp.dot(h, w2_ref[...], preferred_element_type=F32)

    @pl.when(j == nj - 1)
    def _():
        rc = 128

        def chunk(c, _):
            r0 = pl.multiple_of(c * rc, rc)
            y = DN_ALPHA * x_ref[pl.ds(r0, rc), :] + gate_ref[0] * o_ref[pl.ds(r0, rc), :]
            o_ref[pl.ds(r0, rc), :] = _layer_norm_rows(y, g_ref[...], b_ref[...])
            return 0

        lax.fori_loop(0, o_ref.shape[0] // rc, chunk, 0)


def _ffn(x, mods, p):
    tm = ROW_TILE
    tf = FFN_TF
    vec = pl.BlockSpec((1, D_MODEL), lambda i, j: (0, 0))
    return pl.pallas_call(
        _ffn_kernel,
        grid=(N_TOK // tm, D_FF // tf),
        in_specs=[pl.BlockSpec((tm, D_MODEL), lambda i, j: (i, 0), pipeline_mode=pl.Buffered(1)),
                  _mod_spec(tm, 3), _mod_spec(tm, 4), _mod_spec(tm, 5), vec, vec,
                  pl.BlockSpec((D_MODEL, tf), lambda i, j: (0, j)),
                  pl.BlockSpec((D_MODEL, tf), lambda i, j: (0, j)),
                  pl.BlockSpec((tf, D_MODEL), lambda i, j: (j, 0))],
        out_specs=pl.BlockSpec((tm, D_MODEL), lambda i, j: (i, 0)),
        out_shape=jax.ShapeDtypeStruct((N_TOK, D_MODEL), F32),
        scratch_shapes=[pltpu.VMEM((tm, D_MODEL), BF16)],
        compiler_params=_cparams(("arbitrary", "arbitrary")),
        name="ffn_swiglu",
    )(x, mods, mods, mods, p['ln2_g'].reshape(1, D_MODEL), p['ln2_b'].reshape(1, D_MODEL),
      _cast_bf16(p['ffn_w1'], 256, "ffn_w1_bf16"), _cast_bf16(p['ffn_w3'], 256, "ffn_w3_bf16"),
      _cast_bf16(p['ffn_w2'], 512, "ffn_w2_bf16"))


def _filter_mlp_kernel(feat_ref, w1_ref, b1_ref, f1_ref, w2_ref, b2_ref, f2_ref, o_ref):
    h = jnp.sin(f1_ref[...] * (_dot3(feat_ref[...], w1_ref[...]) + b1_ref[...]))
    o_ref[...] = jnp.sin(f2_ref[...] * (_dot3(h, w2_ref[...]) + b2_ref[...]))


def _filter_features(seq):
    t = jnp.arange(seq, dtype=F32)
    t01 = t / (seq - 1)
    w = 2.0 * math.pi * t / seq
    f = jnp.linspace(1e-4, HYENA_BANDS - 1, HYENA_BANDS, dtype=F32)
    fw = w[:, None] * f[None, :]
    feat = jnp.concatenate([t01[:, None], jnp.cos(fw), -jnp.sin(fw)], -1)
    return jnp.pad(feat, ((0, 0), (0, LANE - HYENA_EMB))), t01[:, None]


def _filter_mlp(seq, p):
    feat, t01 = _filter_features(seq)
    hid = HYENA_FILTER_HIDDEN
    w1 = jnp.pad(p['filt_w1'], ((0, LANE - HYENA_EMB), (0, 0)))
    row = lambda v: v.reshape(1, hid)
    h2 = pl.pallas_call(
        _filter_mlp_kernel,
        out_shape=jax.ShapeDtypeStruct((seq, hid), F32),
        compiler_params=pltpu.CompilerParams(vmem_limit_bytes=VMEM_LIMIT),
        name=f"hyena_filter_mlp_{seq}",
    )(feat, w1, row(p['filt_b1']), row(p['filt_f1']), p['filt_w2'], row(p['filt_b2']), row(p['filt_f2']))
    return h2, t01


def _dft_matrices(seq):
    n = 2 * seq
    f = jnp.arange(seq, dtype=jnp.int32)

    def table(freqs):
        ang = ((freqs[:, None] * f[None, :]) % n).astype(F32) * (math.pi / seq)
        return jnp.cos(ang), jnp.sin(ang)

    step = 32
    ca, sa = table(jnp.arange(seq // step, dtype=jnp.int32) * step)
    cb, sb = table(jnp.arange(step, dtype=jnp.int32))
    c = (ca[:, None, :] * cb[None, :, :] - sa[:, None, :] * sb[None, :, :]).reshape(seq, seq)
    s = -(sa[:, None, :] * cb[None, :, :] + ca[:, None, :] * sb[None, :, :]).reshape(seq, seq)
    alt = jnp.where(f % 2 == 0, 1.0, -1.0).astype(F32)
    s_fwd = s.at[0, :].set(alt)
    s_inv = s.at[:, 0].set(alt)
    fwd = jnp.concatenate([c, s_fwd], axis=0).astype(BF16)
    inv = jnp.concatenate([c, s_inv], axis=1).astype(BF16)
    return fwd, inv, alt[:, None]


HY_RC = 128


def _hyena_kernel(*refs, seq, nb, ct, has_dst):
    (x1_ref, x2_ref, z_ref, sw1_ref, sw2_ref, swz_ref, sb1_ref, sb2_ref, sbz_ref,
     h2_ref, t01_ref, alt_ref, dec_ref, bias_ref, fwd_ref, inv_ref) = refs[:16]
    w3_ref = refs[16:16 + 2 * HYENA_ORDER]
    n_in = 16 + 2 * HYENA_ORDER + int(has_dst)
    (o_ref, kr_ref, ki_ref, zb_ref, zf_ref, yb_ref, pad_ref, g1_ref, g2_ref,
     zc_ref) = refs[n_in:]
    nrc = seq // HY_RC

    @pl.when(pl.program_id(1) == 0)
    def _():
        t01 = t01_ref[...]
        h2 = h2_ref[...]
        row0 = lax.broadcasted_iota(jnp.int32, (seq, ct), 0) == 0
        wf = jnp.where(row0, 1.0, 2.0) / (2.0 * seq)
        for n in range(HYENA_ORDER):
            kpos = _dot3(h2, w3_ref[n][...]) * jnp.exp(-t01 * jnp.abs(dec_ref[n:n + 1, :]))
            kneg = _dot3(h2, w3_ref[HYENA_ORDER + n][...]) * jnp.exp(
                -t01 * jnp.abs(dec_ref[HYENA_ORDER + n:HYENA_ORDER + n + 1, :]))
            kneg = jnp.where(row0, 0.0, kneg)
            ksum = kpos + kneg
            kdif = kpos - kneg
            kr = jnp.dot(fwd_ref[0:seq, :], ksum.astype(BF16), preferred_element_type=F32)
            ki = jnp.dot(fwd_ref[seq:2 * seq, :], kdif.astype(BF16), preferred_element_type=F32)
            nyq = jnp.sum(alt_ref[...] * ksum, axis=0, keepdims=True) / (2.0 * seq)
            kr_ref[n] = kr * wf
            ki_ref[n] = jnp.where(row0, nyq, ki * wf)

    zeros = jnp.zeros((SUBLANE, ct), F32)
    pad_ref[0:SUBLANE, :] = zeros
    pad_ref[seq + SUBLANE:seq + 2 * SUBLANE, :] = zeros

    def short_conv(src_ref, rows0, w_ref, b_ref, dst_ref):
        pad_ref[SUBLANE:seq + SUBLANE, :] = src_ref[rows0:rows0 + seq, :]
        w = w_ref[...]
        for c in range(nrc):
            r0 = c * HY_RC
            acc = b_ref[...] + w[0:1, :] * pad_ref[r0 + 7:r0 + 7 + HY_RC, :]
            acc = acc + w[1:2, :] * pad_ref[r0 + 8:r0 + 8 + HY_RC, :]
            acc = acc + w[2:3, :] * pad_ref[r0 + 9:r0 + 9 + HY_RC, :]
            dst_ref[r0:r0 + HY_RC, :] = acc

    for bi in range(nb):
        rows0 = bi * seq
        short_conv(x1_ref, rows0, sw1_ref, sb1_ref, g1_ref)
        short_conv(x2_ref, rows0, sw2_ref, sb2_ref, g2_ref)
        short_conv(z_ref, rows0, swz_ref, sbz_ref, zc_ref)
        for n, gate_ref in enumerate((g1_ref, g2_ref)):
            for c in range(nrc):
                rows = slice(c * HY_RC, (c + 1) * HY_RC)
                zb_ref[rows, :] = zc_ref[rows, :].astype(BF16)
            zf_ref[...] = jnp.dot(fwd_ref[...], zb_ref[...], preferred_element_type=F32)
            for c in range(nrc):
                rows = slice(c * HY_RC, (c + 1) * HY_RC)
                rows_i = slice(seq + c * HY_RC, seq + (c + 1) * HY_RC)
                zr = zf_ref[rows, :]
                zi = zf_ref[rows_i, :]
                kr = kr_ref[n, rows, :]
                ki = ki_ref[n, rows, :]
                yr = zr * kr - zi * ki
                yi = zr * ki + zi * kr
                if c == 0:
                    first = lax.broadcasted_iota(jnp.int32, (HY_RC, ct), 0) == 0
                    yr = jnp.where(first, zr * kr, yr)
                    yi = jnp.where(first, zi * ki, yi)
                yb_ref[rows, :] = yr.astype(BF16)
                yb_ref[rows_i, :] = yi.astype(BF16)
            zf_ref[0:seq, :] = jnp.dot(inv_ref[...], yb_ref[...], preferred_element_type=F32)
            for c in range(nrc):
                rows = slice(c * HY_RC, (c + 1) * HY_RC)
                zc = zc_ref[rows, :]
                znew = gate_ref[rows, :] * (zf_ref[rows, :] + zc * bias_ref[n:n + 1, :])
                if n == HYENA_ORDER - 1:
                    o_ref[rows0 + c * HY_RC:rows0 + (c + 1) * HY_RC, :] = znew.astype(o_ref.dtype)
                else:
                    zc_ref[rows, :] = znew


def _hyena(proj, p, *, seq, batch, row0, nb, ct, name, dst=None):
    nct = D_MODEL // ct
    extra_specs, extra_args, aliases = [], [], {}
    if dst is not None:
        extra_specs, extra_args, aliases = [_IN_PLACE], [dst], {16 + 2 * HYENA_ORDER: 0}
    h2, t01 = _filter_mlp(seq, p)
    fwd, inv, alt = _dft_matrices(seq)
    rb0 = row0 // (nb * seq)
    w3 = p['filt_w3']
    w3k = lambda k: pl.BlockSpec((HYENA_FILTER_HIDDEN, ct), lambda c, b: (0, k * nct + c))
    dec = p['filt_decay'].reshape(2 * HYENA_ORDER, D_MODEL)
    sw = p['short_w']
    sb = p['short_b'].reshape(1, 3 * D_MODEL)
    slab = lambda k: pl.BlockSpec((nb * seq, ct), lambda c, b: (rb0 + b, k * nct + c))
    swk = lambda k: pl.BlockSpec((3, ct), lambda c, b: (0, k * nct + c))
    sbk = lambda k: pl.BlockSpec((1, ct), lambda c, b: (0, k * nct + c))
    const = lambda shape: pl.BlockSpec(shape, lambda c, b: tuple(0 for _ in shape))
    return pl.pallas_call(
        functools.partial(_hyena_kernel, seq=seq, nb=nb, ct=ct, has_dst=dst is not None),
        grid=(nct, batch // nb),
        in_specs=[slab(0), slab(1), slab(2), swk(0), swk(1), swk(2), sbk(0), sbk(1), sbk(2),
                  const((seq, HYENA_FILTER_HIDDEN)), const((seq, 1)), const((seq, 1)),
                  pl.BlockSpec((2 * HYENA_ORDER, ct), lambda c, b: (0, c)),
                  pl.BlockSpec((HYENA_ORDER, ct), lambda c, b: (0, c)),
                  pl.BlockSpec((2 * seq, seq), lambda c, b: (0, 0), pipeline_mode=pl.Buffered(1)),
                  pl.BlockSpec((seq, 2 * seq), lambda c, b: (0, 0), pipeline_mode=pl.Buffered(1))]
        + [w3k(k) for k in range(2 * HYENA_ORDER)] + extra_specs,
        out_specs=pl.BlockSpec((nb * seq, ct), lambda c, b: (rb0 + b, c)),
        out_shape=jax.ShapeDtypeStruct((N_TOK, D_MODEL), BF16),
        input_output_aliases=aliases,
        scratch_shapes=[pltpu.VMEM((HYENA_ORDER, seq, ct), F32),
                        pltpu.VMEM((HYENA_ORDER, seq, ct), F32),
                        pltpu.VMEM((seq, ct), BF16),
                        pltpu.VMEM((2 * seq, ct), F32),
                        pltpu.VMEM((2 * seq, ct), BF16),
                        pltpu.VMEM((seq + 2 * SUBLANE, ct), F32),
                        pltpu.VMEM((seq, ct), F32),
                        pltpu.VMEM((seq, ct), F32),
                        pltpu.VMEM((seq, ct), F32)],
        compiler_params=_cparams(("arbitrary", "arbitrary")),
        name=name,
    )(proj, proj, proj, sw, sw, sw, sb, sb, sb, h2, t01, alt, dec, p['filt_bias'], fwd, inv,
      *([w3] * (2 * HYENA_ORDER)), *extra_args)


def _moe_kernel(ie_ref, ib_ref, ir_ref, x_ref, w1_ref, w3_ref, w2_ref, o_hbm,
                acc_ref, w1b_ref, w3b_ref, w2b_ref, sem):
    w = pl.program_id(0)
    j = pl.program_id(1)
    nj = pl.num_programs(1)
    rows = ir_ref[w]
    rt = MOE_ROW_TILE
    ht = rt // 2
    nfull = rows // rt
    tail = rows - nfull * rt
    ntiles = nfull + (tail > ht).astype(jnp.int32)
    has_half = jnp.logical_and(tail > 0, tail <= ht)

    def partial_out(off, size):
        xt = _unpack_bf16_pair(x_ref[pl.ds(off, size), :])
        h1 = jnp.dot(xt, w1b_ref[...], preferred_element_type=F32)
        h3 = jnp.dot(xt, w3b_ref[...], preferred_element_type=F32)
        h = (_silu(h1) * h3).astype(BF16)
        return jnp.dot(h, w2b_ref[...], preferred_element_type=F32)

    def cast_weights():
        w1b_ref[...] = w1_ref[...].astype(BF16)
        w3b_ref[...] = w3_ref[...].astype(BF16)
        w2b_ref[...] = w2_ref[...].astype(BF16)

    @pl.when(rows > 0)
    def _():
        def run_tiles(first):
            def one(r, size=rt):
                off = r * rt if isinstance(r, int) else pl.multiple_of(r * rt, rt)
                part = partial_out(off, size)
                if first:
                    acc_ref[pl.ds(off, size), :] = part
                else:
                    acc_ref[pl.ds(off, size), :] += part

            def half_tile():
                one(ntiles, ht)

            def quad(q, _):
                for k in range(4):
                    one(2 + 4 * q + k)
                return 0

            @pl.when(ntiles >= 2)
            def _():
                cast_weights()
                one(0)
                one(1)
                rest = ntiles - 2
                lax.fori_loop(0, rest // 4, quad, 0)
                done = 2 + (rest // 4) * 4

                @pl.when(rest % 4 >= 2)
                def _():
                    one(done)
                    one(done + 1)

                pl.when(ntiles % 2 == 1)(lambda: one(ntiles - 1))
                pl.when(has_half)(half_tile)

            @pl.when(ntiles == 1)
            def _():
                cast_weights()
                one(0)
                pl.when(has_half)(half_tile)

            @pl.when(ntiles == 0)
            def _():
                cast_weights()
                half_tile()

        @pl.when(j == 0)
        def _():
            run_tiles(True)

        @pl.when(j > 0)
        def _():
            run_tiles(False)

        @pl.when(j == nj - 1)
        def _():
            base = ib_ref[w] * MOE_CHUNK

            def tile_copy(off, size=rt):
                return pltpu.make_async_copy(acc_ref.at[pl.ds(off, size)],
                                             o_hbm.at[pl.ds(base + off, size)], sem)

            def start(r, _):
                tile_copy(pl.multiple_of(r * rt, rt)).start()
                return 0

            def wait(r, _):
                tile_copy(pl.multiple_of(r * rt, rt)).wait()
                return 0

            half_off = pl.multiple_of(ntiles * rt, rt)
            lax.fori_loop(0, ntiles, start, 0)
            pl.when(has_half)(lambda: tile_copy(half_off, ht).start())
            lax.fori_loop(0, ntiles, wait, 0)
            pl.when(has_half)(lambda: tile_copy(half_off, ht).wait())


def _moe_experts(xs, item_expert, item_block, item_rows, p):
    tj = MOE_FF_TILE
    nj = D_FF_EXPERT // tj

    def jeff(w, j, ir):
        return jnp.where(ir[w] > 0, j, nj - 1)

    grid_spec = pltpu.PrefetchScalarGridSpec(
        num_scalar_prefetch=3,
        grid=(MOE_ITEMS, nj),
        in_specs=[pl.BlockSpec((MOE_CHUNK, D_MODEL // 2), lambda w, j, ie, ib, ir: (ib[w], 0),
                               pipeline_mode=pl.Buffered(1)),
                  pl.BlockSpec((None, D_MODEL, tj), lambda w, j, ie, ib, ir: (ie[w], 0, jeff(w, j, ir))),
                  pl.BlockSpec((None, D_MODEL, tj), lambda w, j, ie, ib, ir: (ie[w], 0, jeff(w, j, ir))),
                  pl.BlockSpec((None, tj, D_MODEL), lambda w, j, ie, ib, ir: (ie[w], jeff(w, j, ir), 0))],
        out_specs=pl.BlockSpec(memory_space=pl.ANY),
        scratch_shapes=[pltpu.VMEM((MOE_CHUNK, D_MODEL), F32),
                        pltpu.VMEM((D_MODEL, tj), BF16),
                        pltpu.VMEM((D_MODEL, tj), BF16),
                        pltpu.VMEM((tj, D_MODEL), BF16),
                        pltpu.SemaphoreType.DMA(())],
    )
    return pl.pallas_call(
        _moe_kernel,
        grid_spec=grid_spec,
        out_shape=jax.ShapeDtypeStruct((MOE_ROWS, D_MODEL), F32),
        compiler_params=_cparams(("arbitrary", "arbitrary")),
        name="moe_experts",
    )(item_expert, item_block, item_rows, xs, p['exp_w1'], p['exp_w3'], p['exp_w2'])


ROUTE_TM = 256
R_E0, R_E1, R_P0, R_P1, R_RANK0, R_RANK1 = range(6)


def _route_kernel(lg_ref, o_ref, cnt_ref, carry_ref):
    i = pl.program_id(0)
    tm = ROUTE_TM

    @pl.when(i == 0)
    def _():
        carry_ref[...] = jnp.zeros(carry_ref.shape, F32)

    lane = lax.broadcasted_iota(jnp.int32, (tm, LANE), 1)
    lg = jnp.where(lane < N_EXPERTS, lg_ref[...], -jnp.inf)
    m0 = jnp.max(lg, -1, keepdims=True)
    e0 = jnp.min(jnp.where(lg == m0, lane, LANE), -1, keepdims=True)
    lg1 = jnp.where(lane == e0, -jnp.inf, lg)
    m1 = jnp.max(lg1, -1, keepdims=True)
    e1 = jnp.min(jnp.where(lg1 == m1, lane, LANE), -1, keepdims=True)
    t = jnp.exp(m1 - m0)
    p0 = 1.0 / (1.0 + t)
    p1 = t / (1.0 + t)
    hit = ((lane == e0) | (lane == e1)).astype(BF16)
    r_i = lax.broadcasted_iota(jnp.int32, (tm, tm), 0)
    c_i = lax.broadcasted_iota(jnp.int32, (tm, tm), 1)
    before = (c_i < r_i).astype(BF16)
    pref = jnp.dot(before, hit, preferred_element_type=F32) + carry_ref[0:1, :]
    rank0 = jnp.sum(jnp.where(lane == e0, pref, 0.0), -1, keepdims=True)
    rank1 = jnp.sum(jnp.where(lane == e1, pref, 0.0), -1, keepdims=True)
    carry_ref[0:1, :] = carry_ref[0:1, :] + jnp.sum(hit.astype(F32), axis=0, keepdims=True)
    rec = jnp.zeros((tm, LANE), F32)
    for k, v in ((R_E0, e0.astype(F32)), (R_E1, e1.astype(F32)), (R_P0, p0), (R_P1, p1),
                 (R_RANK0, rank0), (R_RANK1, rank1)):
        rec = jnp.where(lane == k, v, rec)
    o_ref[...] = rec
    cnt_ref[...] = carry_ref[...]


def _route_records(logits):
    return pl.pallas_call(
        _route_kernel,
        grid=(N_TOK // ROUTE_TM,),
        in_specs=[pl.BlockSpec((ROUTE_TM, LANE), lambda i: (i, 0))],
        out_specs=[pl.BlockSpec((ROUTE_TM, LANE), lambda i: (i, 0)),
                   pl.BlockSpec((SUBLANE, LANE), lambda i: (0, 0))],
        out_shape=[jax.ShapeDtypeStruct((N_TOK, LANE), F32),
                   jax.ShapeDtypeStruct((SUBLANE, LANE), F32)],
        scratch_shapes=[pltpu.VMEM((SUBLANE, LANE), F32)],
        compiler_params=_cparams(("arbitrary",)),
        name="moe_route",
    )(logits)


def _route(logits):
    rec, cnt = _route_records(logits)
    e_flat = rec[:, R_E0:R_E1 + 1].astype(jnp.int32).reshape(-1)
    rank = rec[:, R_RANK0:R_RANK1 + 1].astype(jnp.int32).reshape(-1)
    counts = cnt[0, :N_EXPERTS].astype(jnp.int32)
    blocks = (counts + MOE_CHUNK - 1) // MOE_CHUNK
    bend = jnp.cumsum(blocks)
    bstart = bend - blocks
    total = bend[-1]
    pos = bstart[e_flat] * MOE_CHUNK + rank
    w = jnp.arange(MOE_ITEMS, dtype=jnp.int32)
    w_eff = jnp.minimum(w, total - 1)
    item_expert = jnp.minimum(jnp.sum((w_eff[:, None] >= bend[None, :]).astype(jnp.int32), axis=1),
                              N_EXPERTS - 1)
    item_rows = jnp.where(w < total,
                          jnp.clip(counts[item_expert] - (w - bstart[item_expert]) * MOE_CHUNK, 0, MOE_CHUNK),
                          0)
    return rec, pos.reshape(N_TOK, TOP_K), item_expert.astype(jnp.int32), \
        w_eff.astype(jnp.int32), item_rows.astype(jnp.int32)


DISPATCH_TM = 256


def _dispatch_kernel(pos_ref, ib_ref, ir_ref, u_ref, o_hbm, zero_ref, sem, zsem):
    i = pl.program_id(0)
    tm = DISPATCH_TM
    rt = MOE_ROW_TILE

    @pl.when(i == 0)
    def _():
        zero_ref[...] = jnp.zeros(zero_ref.shape, zero_ref.dtype)

        def tail_copy(w):
            last_tile = (ir_ref[w] - 1) // rt
            start = pl.multiple_of(ib_ref[w] * MOE_CHUNK + last_tile * rt, rt)
            return pltpu.make_async_copy(zero_ref, o_hbm.at[pl.ds(start, rt)], zsem)

        for w in range(MOE_ITEMS):
            pl.when(ir_ref[w] > 0)(lambda w=w: tail_copy(w).start())
        for w in range(MOE_ITEMS):
            pl.when(ir_ref[w] > 0)(lambda w=w: tail_copy(w).wait())

    def row_copy(r, dst_row):
        return pltpu.make_async_copy(u_ref.at[pl.ds(r, 1)], o_hbm.at[pl.ds(dst_row, 1)], sem)

    def start(r, _):
        a = (i * tm + r) * TOP_K
        for k in range(TOP_K):
            row_copy(r, pos_ref[a + k]).start(priority=k)
        return 0

    def wait(r, _):
        row_copy(0, 0).wait()
        return 0

    lax.fori_loop(0, tm, start, 0, unroll=8)
    lax.fori_loop(0, tm * TOP_K, wait, 0, unroll=8)


def _dispatch(pos_flat, item_block, item_rows, u):
    width = u.shape[1]
    grid_spec = pltpu.PrefetchScalarGridSpec(
        num_scalar_prefetch=3,
        grid=(N_TOK // DISPATCH_TM,),
        in_specs=[pl.BlockSpec((DISPATCH_TM, width), lambda i, pos, ib, ir: (i, 0))],
        out_specs=pl.BlockSpec(memory_space=pl.ANY),
        scratch_shapes=[pltpu.VMEM((MOE_ROW_TILE, width), u.dtype),
                        pltpu.SemaphoreType.DMA(()),
                        pltpu.SemaphoreType.DMA(())],
    )
    return pl.pallas_call(
        _dispatch_kernel,
        grid_spec=grid_spec,
        out_shape=jax.ShapeDtypeStruct((MOE_ROWS, width), u.dtype),
        compiler_params=_cparams(("arbitrary",)),
        name="moe_dispatch",
    )(pos_flat, item_block, item_rows, u)


def _postnorm_kernel(x_ref, y0_ref, y1_ref, rec_ref, gate_ref, g_ref, b_ref, op_ref, os_ref):
    i = pl.program_id(0)
    n_p = TOK_P // x_ref.shape[0]
    delta = (rec_ref[:, R_P0:R_P0 + 1] * y0_ref[...] + rec_ref[:, R_P1:R_P1 + 1] * y1_ref[...])
    y = DN_ALPHA * x_ref[...] + gate_ref[0] * delta
    out = _layer_norm_rows(y, g_ref[...], b_ref[...])

    @pl.when(i < n_p)
    def _():
        op_ref[...] = out

    @pl.when(i >= n_p)
    def _():
        os_ref[...] = out


def _postnorm(x, y0, y1, rec, mods, gate_chunk, g, b):
    tm = 256
    vec = pl.BlockSpec((1, D_MODEL), lambda i: (0, 0))
    rows = pl.BlockSpec((tm, D_MODEL), lambda i: (i, 0))
    return pl.pallas_call(
        _postnorm_kernel,
        grid=(N_TOK // tm,),
        in_specs=[rows, rows, rows, pl.BlockSpec((tm, LANE), lambda i: (i, 0)),
                  _mod_spec(tm, gate_chunk), vec, vec],
        out_specs=_token_row_specs(2, tm, D_MODEL),
        out_shape=[jax.ShapeDtypeStruct((TOK_P, D_MODEL), F32),
                   jax.ShapeDtypeStruct((TOK_S, D_MODEL), F32)],
        compiler_params=_cparams(("arbitrary",)),
        name="final_postnorm",
    )(x, y0, y1, rec, mods, g.reshape(1, D_MODEL), b.reshape(1, D_MODEL))


def kernel(x_prompt, x_sample, c, c_ctx, cache_l0_k, cache_l0_v, state_l0_lru, l0_ada_w, l0_ada_b, l0_w_in, l0_q_norm, l0_k_norm, l0_lru_conv_w, l0_lru_conv_b, l0_lru_lambda, l0_lru_w_r, l0_lru_b_r, l0_lru_w_i, l0_lru_b_i, l0_w_out, l0_ln1_g, l0_ln1_b, l0_ffn_w1, l0_ffn_w3, l0_ffn_w2, l0_ln2_g, l0_ln2_b, l1_ada_w, l1_ada_b, l1_w_in, l1_short_w, l1_short_b, l1_filt_w1, l1_filt_b1, l1_filt_f1, l1_filt_w2, l1_filt_b2, l1_filt_f2, l1_filt_w3, l1_filt_decay, l1_filt_bias, l1_w_out, l1_ln1_g, l1_ln1_b, l1_router, l1_exp_w1, l1_exp_w3, l1_exp_w2, l1_ln2_g, l1_ln2_b):
    x_in = [x_prompt.reshape(TOK_P, D_MODEL), x_sample.reshape(TOK_S, D_MODEL)]
    cond = jnp.concatenate([c_ctx[None, :], c, jnp.zeros((N_COND - 1 - DEC_BATCH, D_MODEL), F32)], axis=0)
    mods0 = _ada(cond, l0_ada_w, l0_ada_b)
    mods1 = _ada(cond, l1_ada_w, l1_ada_b)

    lru_p = dict(conv_w=l0_lru_conv_w, conv_b=l0_lru_conv_b, lam=l0_lru_lambda,
                 w_r=l0_lru_w_r, b_r=l0_lru_b_r, w_i=l0_lru_w_i, b_i=l0_lru_b_i)
    proj0 = _proj(x_in, mods0, l0_w_in, tn=512, name="l0_in_proj")
    attn, new_k, new_v = _attn_context(proj0, l0_q_norm, l0_k_norm)
    attn = _attn_latent(proj0, cache_l0_k, cache_l0_v, l0_q_norm, l0_k_norm, attn)
    lru, new_h = _lru(proj0, jnp.zeros((BATCH, 2, LRU_WIDTH), F32), lru_p,
                      seq=SEQ, batch=BATCH, row0=0, name="lru_context")
    lru, _ = _lru(proj0, state_l0_lru, lru_p,
                  seq=DEC_SEQ, batch=DEC_BATCH, row0=TOK_P, name="lru_latent", dst=lru)
    x = _outproj([attn, lru], l0_w_out, x_in, mods0, 2, l0_ln1_g, l0_ln1_b, name="l0_out_proj")[0]
    x = _ffn(x, mods0, dict(ffn_w1=l0_ffn_w1, ffn_w3=l0_ffn_w3, ffn_w2=l0_ffn_w2,
                            ln2_g=l0_ln2_g, ln2_b=l0_ln2_b))

    hy_p = dict(short_w=l1_short_w, short_b=l1_short_b, filt_w1=l1_filt_w1, filt_b1=l1_filt_b1,
                filt_f1=l1_filt_f1, filt_w2=l1_filt_w2, filt_b2=l1_filt_b2, filt_f2=l1_filt_f2,
                filt_w3=l1_filt_w3, filt_decay=l1_filt_decay, filt_bias=l1_filt_bias)
    proj1 = _proj([x], mods1, l1_w_in, tn=512, name="l1_in_proj")
    z = _hyena(proj1, hy_p, seq=SEQ, batch=BATCH, row0=0, nb=2, ct=1024, name="hyena_context")
    z = _hyena(proj1, hy_p, seq=DEC_SEQ, batch=DEC_BATCH, row0=TOK_P, nb=1, ct=512, name="hyena_latent",
               dst=z)
    router = jnp.pad(l1_router, ((0, 0), (0, LANE - N_EXPERTS)))
    x, u, logits = _outproj([z], l1_w_out, [x], mods1, 2, l1_ln1_g, l1_ln1_b, name="l1_out_proj",
                            router=router, router_chunks=(3, 4))
    rec, pos, item_expert, item_block, item_rows = _route(logits)
    xs = _dispatch(pos.reshape(-1), item_block, item_rows, u)
    ys = _moe_experts(xs, item_expert, item_block, item_rows,
                      dict(exp_w1=l1_exp_w1, exp_w3=l1_exp_w3, exp_w2=l1_exp_w2))
    y_prompt, y_sample = _postnorm(x, _rows(ys, pos[:, 0]), _rows(ys, pos[:, 1]), rec,
                                   mods1, 5, l1_ln2_g, l1_ln2_b)
    return (y_prompt.reshape(BATCH, SEQ, D_MODEL), y_sample.reshape(DEC_BATCH, DEC_SEQ, D_MODEL),
            new_k.reshape(BATCH, SEQ, N_KV_HEADS, HEAD_DIM),
            new_v.reshape(BATCH, SEQ, N_KV_HEADS, HEAD_DIM),
            new_h)
```

```python
import functools
import math

import jax
import jax.numpy as jnp
from jax import lax
from jax.experimental import pallas as pl
from jax.experimental.pallas import tpu as pltpu

F32 = jnp.float32
BF16 = jnp.bfloat16

D_MODEL = 2048
BATCH = 16
SEQ = 256
DEC_BATCH = 2
DEC_SEQ = 1024
PAST_LEN = 256
GRID_W = 64
HEAD_DIM = 128
N_Q_HEADS = 8
N_KV_HEADS = 2
Q_GROUP = N_Q_HEADS // N_KV_HEADS
Q_WIDTH = N_Q_HEADS * HEAD_DIM
KV_WIDTH = N_KV_HEADS * HEAD_DIM
ROPE_THETA = 10000.0
LRU_WIDTH = D_MODEL // 2
LRU_BLOCK = 128
LRU_C = 8.0
MIX0_IN = Q_WIDTH + 2 * KV_WIDTH + 2 * LRU_WIDTH
HYENA_ORDER = 2
HYENA_EMB = 33
HYENA_BANDS = (HYENA_EMB - 1) // 2
HYENA_FILTER_HIDDEN = 64
D_FF = 5632
N_EXPERTS = 8
TOP_K = 2
D_FF_EXPERT = 7168
N_MOD = 6
LN_EPS = 1e-5
QK_EPS = 1e-6
DEPTH = 2
DN_ALPHA = (2 * DEPTH) ** 0.25

TOK_P = BATCH * SEQ
TOK_S = DEC_BATCH * DEC_SEQ
N_TOK = TOK_P + TOK_S
N_COND = 8
LANE = 128
SUBLANE = 8
VMEM_LIMIT = 56 * 1024 * 1024

ROW_TILE = 1024
MOE_CHUNK = 2048
MOE_ROW_TILE = 256
MOE_FF_TILE = 256
MOE_ITEMS = N_EXPERTS + (N_TOK * TOP_K) // MOE_CHUNK
MOE_ROWS = MOE_ITEMS * MOE_CHUNK


def _cparams(sem):
    return pltpu.CompilerParams(dimension_semantics=sem, vmem_limit_bytes=VMEM_LIMIT)


def _cond_of_tile(i, tm):
    return jnp.maximum(i * tm // DEC_SEQ - (TOK_P // DEC_SEQ - 1), 0)


def _mod_spec(tm, chunk, width=D_MODEL, col_of=None):
    per = D_MODEL // width
    if col_of is None:
        return pl.BlockSpec((1, 1, width), lambda i, *_: (_cond_of_tile(i, tm), 0, chunk * per))
    return pl.BlockSpec((1, 1, width),
                        lambda i, j, *_: (_cond_of_tile(i, tm), 0, chunk * per + col_of(j)))


def _silu(x):
    return x * jax.nn.sigmoid(x)


def _split_bf16(x):
    hi = x.astype(BF16)
    lo = (x - hi.astype(F32)).astype(BF16)
    return hi, lo


def _pack_bf16_pair(lo, hi):
    lo_bits = lax.bitcast_convert_type(lo.astype(BF16).astype(F32), jnp.uint32) >> 16
    hi_bits = lax.bitcast_convert_type(hi.astype(BF16).astype(F32), jnp.uint32) & jnp.uint32(0xFFFF0000)
    return hi_bits | lo_bits


def _unpack_bf16_pair(words):
    lo = lax.bitcast_convert_type(words << 16, F32)
    hi = lax.bitcast_convert_type(words & jnp.uint32(0xFFFF0000), F32)
    return jnp.concatenate([lo, hi], axis=1).astype(BF16)


def _rows(a, idx):
    return a.at[idx].get(mode="promise_in_bounds")


def _dot3(a, b):
    ah, al = _split_bf16(a)
    bh, bl = _split_bf16(b)
    d = lambda x, y: jnp.dot(x, y, preferred_element_type=F32)
    return d(ah, bh) + (d(ah, bl) + d(al, bh))


def _layer_norm_rows(y, g, b):
    mu = jnp.mean(y, -1, keepdims=True)
    yc = y - mu
    var = jnp.mean(yc * yc, -1, keepdims=True)
    return yc * lax.rsqrt(var + LN_EPS) * g + b


def _ada_kernel(c_ref, w_ref, b_ref, o_ref):
    s = _silu(c_ref[...]).astype(BF16)
    o_ref[...] = jnp.dot(s, w_ref[...].astype(BF16), preferred_element_type=F32) + b_ref[...]


def _ada(cond, w, b):
    tn = 1024
    n = w.shape[1]
    out = pl.pallas_call(
        _ada_kernel,
        grid=(n // tn,),
        in_specs=[pl.BlockSpec((N_COND, D_MODEL), lambda j: (0, 0)),
                  pl.BlockSpec((D_MODEL, tn), lambda j: (0, j)),
                  pl.BlockSpec((1, tn), lambda j: (0, j))],
        out_specs=pl.BlockSpec((N_COND, tn), lambda j: (0, j)),
        out_shape=jax.ShapeDtypeStruct((N_COND, n), F32),
        compiler_params=_cparams(("arbitrary",)),
        name="ada_modulation",
    )(cond, w, b.reshape(1, n))
    return out.reshape(N_COND, 1, n)


def _cast_kernel(x_ref, o_ref):
    o_ref[...] = x_ref[...].astype(o_ref.dtype)


def _cast_bf16(w, rows, name):
    r, c = w.shape
    return pl.pallas_call(
        _cast_kernel,
        grid=(r // rows,),
        in_specs=[pl.BlockSpec((rows, c), lambda i: (i, 0))],
        out_specs=pl.BlockSpec((rows, c), lambda i: (i, 0)),
        out_shape=jax.ShapeDtypeStruct((r, c), BF16),
        compiler_params=_cparams(("arbitrary",)),
        name=name,
    )(w)


def _token_row_specs(n_parts, tm, width, col=None, buffered_once=False):
    kw = dict(pipeline_mode=pl.Buffered(1)) if buffered_once else {}
    col = col or (lambda i, *a: 0)
    if n_parts == 1:
        return [pl.BlockSpec((tm, width), lambda i, *a: (i, col(i, *a)), **kw)]
    n_p = TOK_P // tm
    return [pl.BlockSpec((tm, width), lambda i, *a: (jnp.minimum(i, n_p - 1), col(i, *a)), **kw),
            pl.BlockSpec((tm, width), lambda i, *a: (jnp.maximum(i - n_p, 0), col(i, *a)), **kw)]


def _proj_kernel(*refs, n_x):
    x_refs = refs[:n_x]
    sh_ref, sc_ref, w_ref, o_ref, u_ref, wb_ref = refs[n_x:]
    i = pl.program_id(0)
    j = pl.program_id(1)

    def modulate_from(x_ref):
        u_ref[...] = (x_ref[...] * (1.0 + sc_ref[0]) + sh_ref[0]).astype(BF16)

    @pl.when(j == 0)
    def _():
        if n_x == 1:
            modulate_from(x_refs[0])
        else:
            n_p = TOK_P // u_ref.shape[0]
            pl.when(i < n_p)(lambda: modulate_from(x_refs[0]))
            pl.when(i >= n_p)(lambda: modulate_from(x_refs[1]))

    @pl.when(i == 0)
    def _():
        wb_ref[j] = w_ref[...].astype(BF16)

    o_ref[...] = jnp.dot(u_ref[...], wb_ref[j], preferred_element_type=F32)


def _first_pass_block(i, j, nj):
    return jnp.where(i == 0, j, nj - 1)


def _proj(xs, mods, w, *, tn, name):
    tm = ROW_TILE
    n = w.shape[1]
    nj = n // tn
    return pl.pallas_call(
        functools.partial(_proj_kernel, n_x=len(xs)),
        grid=(N_TOK // tm, nj),
        in_specs=_token_row_specs(len(xs), tm, D_MODEL, buffered_once=True) + [
            _mod_spec(tm, 0), _mod_spec(tm, 1),
            pl.BlockSpec((D_MODEL, tn), lambda i, j: (0, _first_pass_block(i, j, nj)))],
        out_specs=pl.BlockSpec((tm, tn), lambda i, j: (i, j)),
        out_shape=jax.ShapeDtypeStruct((N_TOK, n), F32),
        scratch_shapes=[pltpu.VMEM((tm, D_MODEL), BF16),
                        pltpu.VMEM((nj, D_MODEL, tn), BF16)],
        compiler_params=_cparams(("arbitrary", "arbitrary")),
        name=name,
    )(*xs, mods, mods, w)


def _rms(x, g):
    return x * lax.rsqrt(jnp.mean(x * x, -1, keepdims=True) + QK_EPS) * g


def _dot_nt(a, b):
    return lax.dot_general(a, b, (((1,), (1,)), ((), ())), preferred_element_type=F32)


def _rope(x, cos, sin_signed):
    lane = lax.broadcasted_iota(jnp.int32, x.shape, 1)
    partner = jnp.where(lane % 2 == 0, pltpu.roll(x, HEAD_DIM - 1, 1), pltpu.roll(x, 1, 1))
    return x * cos + partner * sin_signed


def _attn_ctx_kernel(q_ref, k_ref, v_ref, qn_ref, kn_ref, o_ref, ko_ref, vo_ref):
    scale = HEAD_DIM ** -0.5
    kn = _rms(k_ref[...], kn_ref[...])
    v = v_ref[...]
    ko_ref[...] = kn
    vo_ref[...] = v
    kb = kn.astype(BF16)
    vb = v.astype(BF16)
    for g in range(Q_GROUP):
        cols = slice(g * HEAD_DIM, (g + 1) * HEAD_DIM)
        q = _rms(q_ref[:, cols], qn_ref[...]).astype(BF16)
        s = _dot_nt(q, kb) * scale
        p = jnp.exp(s - jnp.max(s, -1, keepdims=True))
        p = p / jnp.sum(p, -1, keepdims=True)
        o = jnp.dot(p.astype(BF16), vb, preferred_element_type=F32)
        o_ref[:, cols] = o.astype(o_ref.dtype)


def _attn_context(proj, q_norm, k_norm):
    qw = Q_GROUP * HEAD_DIM
    k_blk0 = Q_WIDTH // HEAD_DIM
    v_blk0 = (Q_WIDTH + KV_WIDTH) // HEAD_DIM
    vec = pl.BlockSpec((1, HEAD_DIM), lambda b, h: (0, 0))
    return pl.pallas_call(
        _attn_ctx_kernel,
        grid=(BATCH, N_KV_HEADS),
        in_specs=[pl.BlockSpec((SEQ, qw), lambda b, h: (b, h)),
                  pl.BlockSpec((SEQ, HEAD_DIM), lambda b, h: (b, k_blk0 + h)),
                  pl.BlockSpec((SEQ, HEAD_DIM), lambda b, h: (b, v_blk0 + h)),
                  vec, vec],
        out_specs=[pl.BlockSpec((SEQ, qw), lambda b, h: (b, h)),
                   pl.BlockSpec((SEQ, HEAD_DIM), lambda b, h: (b, h)),
                   pl.BlockSpec((SEQ, HEAD_DIM), lambda b, h: (b, h))],
        out_shape=[jax.ShapeDtypeStruct((N_TOK, Q_WIDTH), BF16),
                   jax.ShapeDtypeStruct((TOK_P, KV_WIDTH), F32),
                   jax.ShapeDtypeStruct((TOK_P, KV_WIDTH), F32)],
        compiler_params=_cparams(("arbitrary", "arbitrary")),
        name="attn_context",
    )(proj, proj, proj, q_norm.reshape(1, HEAD_DIM), k_norm.reshape(1, HEAD_DIM))


ATTN_Q_ROWS = 256


def _attn_lat_kernel(q_ref, k_ref, v_ref, ck_ref, cv_ref, qn_ref, kn_ref,
                     cq_ref, sq_ref, ck_tab_ref, sk_tab_ref, dst_ref, o_ref):
    del dst_ref
    scale = HEAD_DIM ** -0.5
    kb = _rope(_rms(k_ref[...], kn_ref[...]), ck_tab_ref[...], sk_tab_ref[...]).astype(BF16)
    vb = v_ref[...].astype(BF16)
    ckb = ck_ref[...].astype(BF16)
    cvb = cv_ref[...].astype(BF16)
    for g in range(Q_GROUP):
        cols = slice(g * HEAD_DIM, (g + 1) * HEAD_DIM)
        q = _rope(_rms(q_ref[:, cols], qn_ref[...]), cq_ref[...], sq_ref[...]).astype(BF16)
        s1 = _dot_nt(q, ckb) * scale
        s2 = _dot_nt(q, kb) * scale
        m = jnp.maximum(jnp.max(s1, -1, keepdims=True), jnp.max(s2, -1, keepdims=True))
        p1 = jnp.exp(s1 - m)
        p2 = jnp.exp(s2 - m)
        den = jnp.sum(p1, -1, keepdims=True) + jnp.sum(p2, -1, keepdims=True)
        o = (jnp.dot((p1 / den).astype(BF16), cvb, preferred_element_type=F32)
             + jnp.dot((p2 / den).astype(BF16), vb, preferred_element_type=F32))
        o_ref[:, cols] = o.astype(o_ref.dtype)


def _rope_tables():
    t = jnp.arange(DEC_SEQ)
    row = (t // GRID_W).astype(F32)
    col = (t % GRID_W).astype(F32)
    n_freq = HEAD_DIM // 4
    inv = 1.0 / (ROPE_THETA ** (jnp.arange(n_freq, dtype=F32) / n_freq))
    ang = jnp.concatenate([row[:, None] * inv, col[:, None] * inv], -1)
    cos = jnp.repeat(jnp.cos(ang), 2, axis=-1)
    sign = jnp.where(jnp.arange(HEAD_DIM) % 2 == 0, -1.0, 1.0).astype(F32)
    sin_signed = jnp.repeat(jnp.sin(ang), 2, axis=-1) * sign
    return cos, sin_signed


_IN_PLACE = pl.BlockSpec(memory_space=pl.ANY)


def _attn_latent(proj, cache_k, cache_v, q_norm, k_norm, dst):
    qw = Q_GROUP * HEAD_DIM
    nq = DEC_SEQ // ATTN_Q_ROWS
    q_row0 = TOK_P // ATTN_Q_ROWS
    kv_row0 = TOK_P // DEC_SEQ
    k_blk0 = Q_WIDTH // HEAD_DIM
    v_blk0 = (Q_WIDTH + KV_WIDTH) // HEAD_DIM
    cos, sin_signed = _rope_tables()
    vec = pl.BlockSpec((1, HEAD_DIM), lambda b, h, c: (0, 0))
    tab_q = pl.BlockSpec((ATTN_Q_ROWS, HEAD_DIM), lambda b, h, c: (c, 0))
    tab_k = pl.BlockSpec((DEC_SEQ, HEAD_DIM), lambda b, h, c: (0, 0))
    ctx = pl.BlockSpec((PAST_LEN, HEAD_DIM), lambda b, h, c: (b, h))
    return pl.pallas_call(
        _attn_lat_kernel,
        grid=(DEC_BATCH, N_KV_HEADS, nq),
        in_specs=[pl.BlockSpec((ATTN_Q_ROWS, qw), lambda b, h, c: (q_row0 + b * nq + c, h)),
                  pl.BlockSpec((DEC_SEQ, HEAD_DIM), lambda b, h, c: (kv_row0 + b, k_blk0 + h)),
                  pl.BlockSpec((DEC_SEQ, HEAD_DIM), lambda b, h, c: (kv_row0 + b, v_blk0 + h)),
                  ctx, ctx, vec, vec, tab_q, tab_q, tab_k, tab_k, _IN_PLACE],
        out_specs=pl.BlockSpec((ATTN_Q_ROWS, qw), lambda b, h, c: (q_row0 + b * nq + c, h)),
        out_shape=jax.ShapeDtypeStruct((N_TOK, Q_WIDTH), BF16),
        input_output_aliases={11: 0},
        compiler_params=_cparams(("arbitrary", "arbitrary", "arbitrary")),
        name="attn_latent",
    )(proj, proj, proj,
      cache_k.reshape(DEC_BATCH * PAST_LEN, KV_WIDTH), cache_v.reshape(DEC_BATCH * PAST_LEN, KV_WIDTH),
      q_norm.reshape(1, HEAD_DIM), k_norm.reshape(1, HEAD_DIM), cos, sin_signed, cos, sin_signed, dst)


LRU_CT = 512
assert (Q_WIDTH + 2 * KV_WIDTH) % LRU_CT == 0 and LRU_WIDTH % LRU_CT == 0


def _sigmoid_tanh(x):
    return 0.5 * (jnp.tanh(0.5 * x) + 1.0)
LRU_RC = 128


def _softplus(x):
    return jnp.maximum(x, 0.0) + jnp.log1p(jnp.exp(-jnp.abs(x)))


def _gelu_tanh(x):
    return 0.5 * x * (1.0 + jnp.tanh(math.sqrt(2.0 / math.pi) * (x + 0.044715 * (x * x * x))))


def _lru_kernel(*refs, seq, has_dst):
    xb_ref, gb_ref, cw_ref, cb_ref, lam_ref, wr_ref, br_ref, wi_ref, bi_ref, h0_ref = refs[:10]
    y_ref, hfin_ref, xpad_ref, a_ref, b_ref, hf_ref, hb_ref = refs[10 + int(has_dst):]
    ct = LRU_CT
    zeros = jnp.zeros((SUBLANE, ct), F32)
    xpad_ref[0:SUBLANE, :] = zeros
    xpad_ref[seq + SUBLANE:seq + 2 * SUBLANE, :] = zeros
    xpad_ref[SUBLANE:seq + SUBLANE, :] = xb_ref[...]
    cw = cw_ref[...]
    sp = _softplus(-lam_ref[...])
    for r0 in range(0, seq, LRU_RC):
        xc = cb_ref[...] + cw[0:1, :] * xpad_ref[r0 + 7:r0 + 7 + LRU_RC, :]
        for w in range(1, 4):
            xc = xc + cw[w:w + 1, :] * xpad_ref[r0 + 7 + w:r0 + 7 + w + LRU_RC, :]
        for kb in range(ct // LRU_BLOCK):
            cols = slice(kb * LRU_BLOCK, (kb + 1) * LRU_BLOCK)
            xk = xc[:, cols]
            xkb = xk.astype(BF16)
            for d in range(2):
                r = _sigmoid_tanh(jnp.dot(xkb, wr_ref[d, kb].astype(BF16), preferred_element_type=F32)
                                  + br_ref[d:d + 1, cols])
                i = _sigmoid_tanh(jnp.dot(xkb, wi_ref[d, kb].astype(BF16), preferred_element_type=F32)
                                  + bi_ref[d:d + 1, cols])
                log_a = -LRU_C * r * sp[d:d + 1, cols]
                a = jnp.exp(log_a)
                a_ref[d, r0:r0 + LRU_RC, cols] = a
                b_ref[d, r0:r0 + LRU_RC, cols] = jnp.sqrt(-jnp.tanh(log_a) * (a * a + 1.0)) * (i * xk)

    nblk = seq // SUBLANE
    row = lax.broadcasted_iota(jnp.int32, (SUBLANE, ct), 0)

    def body(i, carry):
        hf, hb = carry
        rf = pl.multiple_of(i * SUBLANE, SUBLANE)
        a = a_ref[0, pl.ds(rf, SUBLANE), :]
        b = b_ref[0, pl.ds(rf, SUBLANE), :]
        for s in (1, 2, 4):
            m = row >= s
            a_s = jnp.where(m, pltpu.roll(a, s, 0), 1.0)
            b_s = jnp.where(m, pltpu.roll(b, s, 0), 0.0)
            b = a * b_s + b
            a = a * a_s
        hblk = a * hf + b
        hf_ref[pl.ds(rf, SUBLANE), :] = hblk
        hf = hblk[SUBLANE - 1:SUBLANE, :]

        rb = pl.multiple_of((nblk - 1 - i) * SUBLANE, SUBLANE)
        a = a_ref[1, pl.ds(rb, SUBLANE), :]
        b = b_ref[1, pl.ds(rb, SUBLANE), :]
        for s in (1, 2, 4):
            m = row < SUBLANE - s
            a_s = jnp.where(m, pltpu.roll(a, SUBLANE - s, 0), 1.0)
            b_s = jnp.where(m, pltpu.roll(b, SUBLANE - s, 0), 0.0)
            b = a * b_s + b
            a = a * a_s
        hblk = a * hb + b
        hb_ref[pl.ds(rb, SUBLANE), :] = hblk
        hb = hblk[0:1, :]
        return hf, hb

    hf, hb = lax.fori_loop(0, nblk, body, (h0_ref[0:1, :], h0_ref[1:2, :]))
    hfin_ref[0:1, :] = hf
    hfin_ref[1:2, :] = hb
    for r0 in range(0, seq, LRU_RC):
        rows = slice(r0, r0 + LRU_RC)
        y = (hf_ref[rows, :] + hb_ref[rows, :]) * _gelu_tanh(gb_ref[rows, :])
        y_ref[rows, :] = y.astype(y_ref.dtype)


def _lru(proj, h0, p, *, seq, batch, row0, name, dst=None):
    ct = LRU_CT
    nkb = ct // LRU_BLOCK
    xb_blk0 = (Q_WIDTH + 2 * KV_WIDTH) // ct
    gb_blk0 = (Q_WIDTH + 2 * KV_WIDTH + LRU_WIDTH) // ct
    r0 = row0 // seq
    vec2 = pl.BlockSpec((2, ct), lambda b, c: (0, c))
    wblk = pl.BlockSpec((2, nkb, LRU_BLOCK, LRU_BLOCK), lambda b, c: (0, c, 0, 0))
    extra_specs, extra_args, aliases = [], [], {}
    if dst is not None:
        extra_specs, extra_args, aliases = [_IN_PLACE], [dst], {10: 0}
    return pl.pallas_call(
        functools.partial(_lru_kernel, seq=seq, has_dst=dst is not None),
        grid=(batch, LRU_WIDTH // ct),
        in_specs=[pl.BlockSpec((seq, ct), lambda b, c: (r0 + b, xb_blk0 + c)),
                  pl.BlockSpec((seq, ct), lambda b, c: (r0 + b, gb_blk0 + c)),
                  pl.BlockSpec((4, ct), lambda b, c: (0, c)),
                  pl.BlockSpec((1, ct), lambda b, c: (0, c)),
                  vec2, wblk, vec2, wblk, vec2,
                  pl.BlockSpec((None, 2, ct), lambda b, c: (b, 0, c))] + extra_specs,
        out_specs=[pl.BlockSpec((seq, ct), lambda b, c: (r0 + b, c)),
                   pl.BlockSpec((None, 2, ct), lambda b, c: (b, 0, c))],
        out_shape=[jax.ShapeDtypeStruct((N_TOK, LRU_WIDTH), BF16),
                   jax.ShapeDtypeStruct((batch, 2, LRU_WIDTH), F32)],
        input_output_aliases=aliases,
        scratch_shapes=[pltpu.VMEM((seq + 2 * SUBLANE, ct), F32),
                        pltpu.VMEM((2, seq, ct), F32),
                        pltpu.VMEM((2, seq, ct), F32),
                        pltpu.VMEM((seq, ct), F32),
                        pltpu.VMEM((seq, ct), F32)],
        compiler_params=_cparams(("arbitrary", "arbitrary")),
        name=name,
    )(proj, proj, p['conv_w'], p['conv_b'].reshape(1, LRU_WIDTH), p['lam'],
      p['w_r'], p['b_r'], p['w_i'], p['b_i'], h0, *extra_args)


OUT_TN = 1024
OUT_TM = 512


def _outproj_kernel(*refs, n_a, n_res, with_router):
    a_refs = refs[:n_a]
    w_refs = refs[n_a:2 * n_a]
    res_refs = refs[2 * n_a:2 * n_a + n_res]
    gate_ref, g_ref, b_ref = refs[2 * n_a + n_res:2 * n_a + n_res + 3]
    pos = 2 * n_a + n_res + 3
    if with_router:
        sh_ref, sc_ref, rt_ref = refs[pos:pos + 3]
        pos += 3
        o_ref, u_ref, lg_ref, acc_ref = refs[pos:pos + 4]
        pos += 4
    else:
        o_ref, acc_ref = refs[pos:pos + 2]
        pos += 2
    wb_refs = refs[pos:pos + n_a]
    i = pl.program_id(0)
    j = pl.program_id(1)
    nj = pl.num_programs(1)

    @pl.when(i == 0)
    def _():
        for k in range(n_a):
            wb_refs[k][j] = w_refs[k][...].astype(BF16)

    acc = jnp.dot(a_refs[0][...], wb_refs[0][j], preferred_element_type=F32)
    for k in range(1, n_a):
        acc = acc + jnp.dot(a_refs[k][...], wb_refs[k][j], preferred_element_type=F32)
    if n_res == 1:
        res = res_refs[0][...]
    else:
        n_p = TOK_P // acc_ref.shape[1]
        res = jnp.where(i < n_p, res_refs[0][...], res_refs[1][...])
    acc_ref[j] = DN_ALPHA * res + gate_ref[0] * acc

    @pl.when(j == nj - 1)
    def _():
        tm = acc_ref.shape[1]
        n_chunks = acc_ref.shape[0]
        rc = 128

        def chunk(c, _):
            r0 = pl.multiple_of(c * rc, rc)
            ys = [acc_ref[k, pl.ds(r0, rc), :] for k in range(n_chunks)]
            tot = ys[0].sum(-1, keepdims=True)
            for y in ys[1:]:
                tot = tot + y.sum(-1, keepdims=True)
            mu = tot / D_MODEL
            sq = None
            for y in ys:
                t = ((y - mu) * (y - mu)).sum(-1, keepdims=True)
                sq = t if sq is None else sq + t
            rstd = lax.rsqrt(sq / D_MODEL + LN_EPS)
            lg = None
            us = []
            for k, y in enumerate(ys):
                cols = slice(k * OUT_TN, (k + 1) * OUT_TN)
                xn = (y - mu) * rstd * g_ref[:, cols] + b_ref[:, cols]
                o_ref[pl.ds(r0, rc), cols] = xn
                if with_router:
                    u = xn * (1.0 + sc_ref[0][:, cols]) + sh_ref[0][:, cols]
                    us.append(u)
                    t = _dot3(u, rt_ref[cols, :])
                    lg = t if lg is None else lg + t
            if with_router:
                lg_ref[pl.ds(r0, rc), :] = lg
                half = n_chunks // 2
                for k in range(half):
                    cols = slice(k * OUT_TN, (k + 1) * OUT_TN)
                    u_ref[pl.ds(r0, rc), cols] = _pack_bf16_pair(us[k], us[k + half])
            return 0

        lax.fori_loop(0, tm // rc, chunk, 0)


def _outproj(a_list, w, res, mods, gate_chunk, g, b, *, name, router=None, router_chunks=None):
    tm = OUT_TM
    tn = OUT_TN
    nj = D_MODEL // tn
    n_a = len(a_list)
    in_specs = []
    k0 = 0
    w_specs = []
    wb_shapes = []
    for a in a_list:
        ka = a.shape[1]
        in_specs.append(pl.BlockSpec((tm, ka), lambda i, j: (i, 0)))
        w_specs.append(pl.BlockSpec((ka, tn),
                                    lambda i, j, blk=k0 // ka: (blk, _first_pass_block(i, j, nj))))
        wb_shapes.append(pltpu.VMEM((nj, ka, tn), BF16))
        k0 += ka
    in_specs += w_specs
    in_specs += _token_row_specs(len(res), tm, tn, col=lambda i, j: j)
    in_specs += [_mod_spec(tm, gate_chunk, tn, col_of=lambda j: j),
                 pl.BlockSpec((1, D_MODEL), lambda i, j: (0, 0)),
                 pl.BlockSpec((1, D_MODEL), lambda i, j: (0, 0))]
    args = list(a_list) + [w] * n_a + list(res) + [mods, g.reshape(1, D_MODEL), b.reshape(1, D_MODEL)]
    out_specs = [pl.BlockSpec((tm, D_MODEL), lambda i, j: (i, 0))]
    out_shape = [jax.ShapeDtypeStruct((N_TOK, D_MODEL), F32)]
    if router is not None:
        in_specs += [_mod_spec(tm, router_chunks[0]), _mod_spec(tm, router_chunks[1]),
                     pl.BlockSpec((D_MODEL, LANE), lambda i, j: (0, 0))]
        args += [mods, mods, router]
        out_specs += [pl.BlockSpec((tm, D_MODEL // 2), lambda i, j: (i, 0)),
                      pl.BlockSpec((tm, LANE), lambda i, j: (i, 0))]
        out_shape += [jax.ShapeDtypeStruct((N_TOK, D_MODEL // 2), jnp.uint32),
                      jax.ShapeDtypeStruct((N_TOK, LANE), F32)]
    return pl.pallas_call(
        functools.partial(_outproj_kernel, n_a=n_a, n_res=len(res), with_router=router is not None),
        grid=(N_TOK // tm, nj),
        in_specs=in_specs,
        out_specs=out_specs,
        out_shape=out_shape,
        scratch_shapes=[pltpu.VMEM((nj, tm, tn), F32)] + wb_shapes,
        compiler_params=_cparams(("arbitrary", "arbitrary")),
        name=name,
    )(*args)


FFN_TF = 512


def _ffn_kernel(x_ref, sh_ref, sc_ref, gate_ref, g_ref, b_ref, w1_ref, w3_ref, w2_ref, o_ref, u_ref):
    j = pl.program_id(1)
    nj = pl.num_programs(1)

    @pl.when(j == 0)
    def _():
        u_ref[...] = (x_ref[...] * (1.0 + sc_ref[0]) + sh_ref[0]).astype(BF16)
        o_ref[...] = jnp.zeros(o_ref.shape, F32)

    u = u_ref[...]
    h1 = jnp.dot(u, w1_ref[...], preferred_element_type=F32)
    h3 = jnp.dot(u, w3_ref[...], preferred_element_type=F32)
    h = (_silu(h1) * h3).astype(BF16)
    o_ref[...] += jnp.dot(h, w2_ref[...], preferred_element_type=F32)

    @pl.when(j == nj - 1)
    def _():
        rc = 128

        def chunk(c, _):
            r0 = pl.multiple_of(c * rc, rc)
            y = DN_ALPHA * x_ref[pl.ds(r0, rc), :] + gate_ref[0] * o_ref[pl.ds(r0, rc), :]
            o_ref[pl.ds(r0, rc), :] = _layer_norm_rows(y, g_ref[...], b_ref[...])
            return 0

        lax.fori_loop(0, o_ref.shape[0] // rc, chunk, 0)


def _ffn(x, mods, p):
    tm = ROW_TILE
    tf = FFN_TF
    vec = pl.BlockSpec((1, D_MODEL), lambda i, j: (0, 0))
    return pl.pallas_call(
        _ffn_kernel,
        grid=(N_TOK // tm, D_FF // tf),
        in_specs=[pl.BlockSpec((tm, D_MODEL), lambda i, j: (i, 0), pipeline_mode=pl.Buffered(1)),
                  _mod_spec(tm, 3), _mod_spec(tm, 4), _mod_spec(tm, 5), vec, vec,
                  pl.BlockSpec((D_MODEL, tf), lambda i, j: (0, j)),
                  pl.BlockSpec((D_MODEL, tf), lambda i, j: (0, j)),
                  pl.BlockSpec((tf, D_MODEL), lambda i, j: (j, 0))],
        out_specs=pl.BlockSpec((tm, D_MODEL), lambda i, j: (i, 0)),
        out_shape=jax.ShapeDtypeStruct((N_TOK, D_MODEL), F32),
        scratch_shapes=[pltpu.VMEM((tm, D_MODEL), BF16)],
        compiler_params=_cparams(("arbitrary", "arbitrary")),
        name="ffn_swiglu",
    )(x, mods, mods, mods, p['ln2_g'].reshape(1, D_MODEL), p['ln2_b'].reshape(1, D_MODEL),
      _cast_bf16(p['ffn_w1'], 256, "ffn_w1_bf16"), _cast_bf16(p['ffn_w3'], 256, "ffn_w3_bf16"),
      _cast_bf16(p['ffn_w2'], 512, "ffn_w2_bf16"))


def _filter_mlp_kernel(feat_ref, w1_ref, b1_ref, f1_ref, w2_ref, b2_ref, f2_ref, o_ref):
    h = jnp.sin(f1_ref[...] * (_dot3(feat_ref[...], w1_ref[...]) + b1_ref[...]))
    o_ref[...] = jnp.sin(f2_ref[...] * (_dot3(h, w2_ref[...]) + b2_ref[...]))


def _filter_features(seq):
    t = jnp.arange(seq, dtype=F32)
    t01 = t / (seq - 1)
    w = 2.0 * math.pi * t / seq
    f = jnp.linspace(1e-4, HYENA_BANDS - 1, HYENA_BANDS, dtype=F32)
    fw = w[:, None] * f[None, :]
    feat = jnp.concatenate([t01[:, None], jnp.cos(fw), -jnp.sin(fw)], -1)
    return jnp.pad(feat, ((0, 0), (0, LANE - HYENA_EMB))), t01[:, None]


def _filter_mlp(seq, p):
    feat, t01 = _filter_features(seq)
    hid = HYENA_FILTER_HIDDEN
    w1 = jnp.pad(p['filt_w1'], ((0, LANE - HYENA_EMB), (0, 0)))
    row = lambda v: v.reshape(1, hid)
    h2 = pl.pallas_call(
        _filter_mlp_kernel,
        out_shape=jax.ShapeDtypeStruct((seq, hid), F32),
        compiler_params=pltpu.CompilerParams(vmem_limit_bytes=VMEM_LIMIT),
        name=f"hyena_filter_mlp_{seq}",
    )(feat, w1, row(p['filt_b1']), row(p['filt_f1']), p['filt_w2'], row(p['filt_b2']), row(p['filt_f2']))
    return h2, t01


def _dft_matrices(seq):
    n = 2 * seq
    f = jnp.arange(seq, dtype=jnp.int32)

    def table(freqs):
        ang = ((freqs[:, None] * f[None, :]) % n).astype(F32) * (math.pi / seq)
        return jnp.cos(ang), jnp.sin(ang)

    step = 32
    ca, sa = table(jnp.arange(seq // step, dtype=jnp.int32) * step)
    cb, sb = table(jnp.arange(step, dtype=jnp.int32))
    c = (ca[:, None, :] * cb[None, :, :] - sa[:, None, :] * sb[None, :, :]).reshape(seq, seq)
    s = -(sa[:, None, :] * cb[None, :, :] + ca[:, None, :] * sb[None, :, :]).reshape(seq, seq)
    alt = jnp.where(f % 2 == 0, 1.0, -1.0).astype(F32)
    s_fwd = s.at[0, :].set(alt)
    s_inv = s.at[:, 0].set(alt)
    fwd = jnp.concatenate([c, s_fwd], axis=0).astype(BF16)
    inv = jnp.concatenate([c, s_inv], axis=1).astype(BF16)
    return fwd, inv, alt[:, None]


HY_RC = 128


def _hyena_kernel(*refs, seq, nb, ct, has_dst):
    (x1_ref, x2_ref, z_ref, sw1_ref, sw2_ref, swz_ref, sb1_ref, sb2_ref, sbz_ref,
     h2_ref, t01_ref, alt_ref, dec_ref, bias_ref, fwd_ref, inv_ref) = refs[:16]
    w3_ref = refs[16:16 + 2 * HYENA_ORDER]
    n_in = 16 + 2 * HYENA_ORDER + int(has_dst)
    (o_ref, kr_ref, ki_ref, zb_ref, zf_ref, yb_ref, pad_ref, g1_ref, g2_ref,
     zc_ref) = refs[n_in:]
    nrc = seq // HY_RC

    @pl.when(pl.program_id(1) == 0)
    def _():
        t01 = t01_ref[...]
        h2 = h2_ref[...]
        row0 = lax.broadcasted_iota(jnp.int32, (seq, ct), 0) == 0
        wf = jnp.where(row0, 1.0, 2.0) / (2.0 * seq)
        for n in range(HYENA_ORDER):
            kpos = _dot3(h2, w3_ref[n][...]) * jnp.exp(-t01 * jnp.abs(dec_ref[n:n + 1, :]))
            kneg = _dot3(h2, w3_ref[HYENA_ORDER + n][...]) * jnp.exp(
                -t01 * jnp.abs(dec_ref[HYENA_ORDER + n:HYENA_ORDER + n + 1, :]))
            kneg = jnp.where(row0, 0.0, kneg)
            ksum = kpos + kneg
            kdif = kpos - kneg
            kr = jnp.dot(fwd_ref[0:seq, :], ksum.astype(BF16), preferred_element_type=F32)
            ki = jnp.dot(fwd_ref[seq:2 * seq, :], kdif.astype(BF16), preferred_element_type=F32)
            nyq = jnp.sum(alt_ref[...] * ksum, axis=0, keepdims=True) / (2.0 * seq)
            kr_ref[n] = kr * wf
            ki_ref[n] = jnp.where(row0, nyq, ki * wf)

    zeros = jnp.zeros((SUBLANE, ct), F32)
    pad_ref[0:SUBLANE, :] = zeros
    pad_ref[seq + SUBLANE:seq + 2 * SUBLANE, :] = zeros

    def short_conv(src_ref, rows0, w_ref, b_ref, dst_ref):
        pad_ref[SUBLANE:seq + SUBLANE, :] = src_ref[rows0:rows0 + seq, :]
        w = w_ref[...]
        for c in range(nrc):
            r0 = c * HY_RC
            acc = b_ref[...] + w[0:1, :] * pad_ref[r0 + 7:r0 + 7 + HY_RC, :]
            acc = acc + w[1:2, :] * pad_ref[r0 + 8:r0 + 8 + HY_RC, :]
            acc = acc + w[2:3, :] * pad_ref[r0 + 9:r0 + 9 + HY_RC, :]
            dst_ref[r0:r0 + HY_RC, :] = acc

    for bi in range(nb):
        rows0 = bi * seq
        short_conv(x1_ref, rows0, sw1_ref, sb1_ref, g1_ref)
        short_conv(x2_ref, rows0, sw2_ref, sb2_ref, g2_ref)
        short_conv(z_ref, rows0, swz_ref, sbz_ref, zc_ref)
        for n, gate_ref in enumerate((g1_ref, g2_ref)):
            for c in range(nrc):
                rows = slice(c * HY_RC, (c + 1) * HY_RC)
                zb_ref[rows, :] = zc_ref[rows, :].astype(BF16)
            zf_ref[...] = jnp.dot(fwd_ref[...], zb_ref[...], preferred_element_type=F32)
            for c in range(nrc):
                rows = slice(c * HY_RC, (c + 1) * HY_RC)
                rows_i = slice(seq + c * HY_RC, seq + (c + 1) * HY_RC)
                zr = zf_ref[rows, :]
                zi = zf_ref[rows_i, :]
                kr = kr_ref[n, rows, :]
                ki = ki_ref[n, rows, :]
                yr = zr * kr - zi * ki
                yi = zr * ki + zi * kr
                if c == 0:
                    first = lax.broadcasted_iota(jnp.int32, (HY_RC, ct), 0) == 0
                    yr = jnp.where(first, zr * kr, yr)
                    yi = jnp.where(first, zi * ki, yi)
                yb_ref[rows, :] = yr.astype(BF16)
                yb_ref[rows_i, :] = yi.astype(BF16)
            zf_ref[0:seq, :] = jnp.dot(inv_ref[...], yb_ref[...], preferred_element_type=F32)
            for c in range(nrc):
                rows = slice(c * HY_RC, (c + 1) * HY_RC)
                zc = zc_ref[rows, :]
                znew = gate_ref[rows, :] * (zf_ref[rows, :] + zc * bias_ref[n:n + 1, :])
                if n == HYENA_ORDER - 1:
                    o_ref[rows0 + c * HY_RC:rows0 + (c + 1) * HY_RC, :] = znew.astype(o_ref.dtype)
                else:
                    zc_ref[rows, :] = znew


def _hyena(proj, p, *, seq, batch, row0, nb, ct, name, dst=None):
    nct = D_MODEL // ct
    extra_specs, extra_args, aliases = [], [], {}
    if dst is not None:
        extra_specs, extra_args, aliases = [_IN_PLACE], [dst], {16 + 2 * HYENA_ORDER: 0}
    h2, t01 = _filter_mlp(seq, p)
    fwd, inv, alt = _dft_matrices(seq)
    rb0 = row0 // (nb * seq)
    w3 = p['filt_w3']
    w3k = lambda k: pl.BlockSpec((HYENA_FILTER_HIDDEN, ct), lambda c, b: (0, k * nct + c))
    dec = p['filt_decay'].reshape(2 * HYENA_ORDER, D_MODEL)
    sw = p['short_w']
    sb = p['short_b'].reshape(1, 3 * D_MODEL)
    slab = lambda k: pl.BlockSpec((nb * seq, ct), lambda c, b: (rb0 + b, k * nct + c))
    swk = lambda k: pl.BlockSpec((3, ct), lambda c, b: (0, k * nct + c))
    sbk = lambda k: pl.BlockSpec((1, ct), lambda c, b: (0, k * nct + c))
    const = lambda shape: pl.BlockSpec(shape, lambda c, b: tuple(0 for _ in shape))
    return pl.pallas_call(
        functools.partial(_hyena_kernel, seq=seq, nb=nb, ct=ct, has_dst=dst is not None),
        grid=(nct, batch // nb),
        in_specs=[slab(0), slab(1), slab(2), swk(0), swk(1), swk(2), sbk(0), sbk(1), sbk(2),
                  const((seq, HYENA_FILTER_HIDDEN)), const((seq, 1)), const((seq, 1)),
                  pl.BlockSpec((2 * HYENA_ORDER, ct), lambda c, b: (0, c)),
                  pl.BlockSpec((HYENA_ORDER, ct), lambda c, b: (0, c)),
                  pl.BlockSpec((2 * seq, seq), lambda c, b: (0, 0), pipeline_mode=pl.Buffered(1)),
                  pl.BlockSpec((seq, 2 * seq), lambda c, b: (0, 0), pipeline_mode=pl.Buffered(1))]
        + [w3k(k) for k in range(2 * HYENA_ORDER)] + extra_specs,
        out_specs=pl.BlockSpec((nb * seq, ct), lambda c, b: (rb0 + b, c)),
        out_shape=jax.ShapeDtypeStruct((N_TOK, D_MODEL), BF16),
        input_output_aliases=aliases,
        scratch_shapes=[pltpu.VMEM((HYENA_ORDER, seq, ct), F32),
                        pltpu.VMEM((HYENA_ORDER, seq, ct), F32),
                        pltpu.VMEM((seq, ct), BF16),
                        pltpu.VMEM((2 * seq, ct), F32),
                        pltpu.VMEM((2 * seq, ct), BF16),
                        pltpu.VMEM((seq + 2 * SUBLANE, ct), F32),
                        pltpu.VMEM((seq, ct), F32),
                        pltpu.VMEM((seq, ct), F32),
                        pltpu.VMEM((seq, ct), F32)],
        compiler_params=_cparams(("arbitrary", "arbitrary")),
        name=name,
    )(proj, proj, proj, sw, sw, sw, sb, sb, sb, h2, t01, alt, dec, p['filt_bias'], fwd, inv,
      *([w3] * (2 * HYENA_ORDER)), *extra_args)


def _moe_kernel(ie_ref, ib_ref, ir_ref, x_ref, w1_ref, w3_ref, w2_ref, o_hbm,
                acc_ref, w1b_ref, w3b_ref, w2b_ref, sem):
    w = pl.program_id(0)
    j = pl.program_id(1)
    nj = pl.num_programs(1)
    rows = ir_ref[w]
    rt = MOE_ROW_TILE
    ht = rt // 2
    nfull = rows // rt
    tail = rows - nfull * rt
    ntiles = nfull + (tail > ht).astype(jnp.int32)
    has_half = jnp.logical_and(tail > 0, tail <= ht)

    def partial_out(off, size):
        xt = _unpack_bf16_pair(x_ref[pl.ds(off, size), :])
        h1 = jnp.dot(xt, w1b_ref[...], preferred_element_type=F32)
        h3 = jnp.dot(xt, w3b_ref[...], preferred_element_type=F32)
        h = (_silu(h1) * h3).astype(BF16)
        return jnp.dot(h, w2b_ref[...], preferred_element_type=F32)

    def cast_weights():
        w1b_ref[...] = w1_ref[...].astype(BF16)
        w3b_ref[...] = w3_ref[...].astype(BF16)
        w2b_ref[...] = w2_ref[...].astype(BF16)

    @pl.when(rows > 0)
    def _():
        def run_tiles(first):
            def one(r, size=rt):
                off = r * rt if isinstance(r, int) else pl.multiple_of(r * rt, rt)
                part = partial_out(off, size)
                if first:
                    acc_ref[pl.ds(off, size), :] = part
                else:
                    acc_ref[pl.ds(off, size), :] += part

            def half_tile():
                one(ntiles, ht)

            def quad(q, _):
                for k in range(4):
                    one(2 + 4 * q + k)
                return 0

            @pl.when(ntiles >= 2)
            def _():
                cast_weights()
                one(0)
                one(1)
                rest = ntiles - 2
                lax.fori_loop(0, rest // 4, quad, 0)
                done = 2 + (rest // 4) * 4

                @pl.when(rest % 4 >= 2)
                def _():
                    one(done)
                    one(done + 1)

                pl.when(ntiles % 2 == 1)(lambda: one(ntiles - 1))
                pl.when(has_half)(half_tile)

            @pl.when(ntiles == 1)
            def _():
                cast_weights()
                one(0)
                pl.when(has_half)(half_tile)

            @pl.when(ntiles == 0)
            def _():
                cast_weights()
                half_tile()

        @pl.when(j == 0)
        def _():
            run_tiles(True)

        @pl.when(j > 0)
        def _():
            run_tiles(False)

        @pl.when(j == nj - 1)
        def _():
            base = ib_ref[w] * MOE_CHUNK

            def tile_copy(off, size=rt):
                return pltpu.make_async_copy(acc_ref.at[pl.ds(off, size)],
                                             o_hbm.at[pl.ds(base + off, size)], sem)

            def start(r, _):
                tile_copy(pl.multiple_of(r * rt, rt)).start()
                return 0

            def wait(r, _):
                tile_copy(pl.multiple_of(r * rt, rt)).wait()
                return 0

            half_off = pl.multiple_of(ntiles * rt, rt)
            lax.fori_loop(0, ntiles, start, 0)
            pl.when(has_half)(lambda: tile_copy(half_off, ht).start())
            lax.fori_loop(0, ntiles, wait, 0)
            pl.when(has_half)(lambda: tile_copy(half_off, ht).wait())


def _moe_experts(xs, item_expert, item_block, item_rows, p):
    tj = MOE_FF_TILE
    nj = D_FF_EXPERT // tj

    def jeff(w, j, ir):
        return jnp.where(ir[w] > 0, j, nj - 1)

    grid_spec = pltpu.PrefetchScalarGridSpec(
        num_scalar_prefetch=3,
        grid=(MOE_ITEMS, nj),
        in_specs=[pl.BlockSpec((MOE_CHUNK, D_MODEL // 2), lambda w, j, ie, ib, ir: (ib[w], 0),
                               pipeline_mode=pl.Buffered(1)),
                  pl.BlockSpec((None, D_MODEL, tj), lambda w, j, ie, ib, ir: (ie[w], 0, jeff(w, j, ir))),
                  pl.BlockSpec((None, D_MODEL, tj), lambda w, j, ie, ib, ir: (ie[w], 0, jeff(w, j, ir))),
                  pl.BlockSpec((None, tj, D_MODEL), lambda w, j, ie, ib, ir: (ie[w], jeff(w, j, ir), 0))],
        out_specs=pl.BlockSpec(memory_space=pl.ANY),
        scratch_shapes=[pltpu.VMEM((MOE_CHUNK, D_MODEL), F32),
                        pltpu.VMEM((D_MODEL, tj), BF16),
                        pltpu.VMEM((D_MODEL, tj), BF16),
                        pltpu.VMEM((tj, D_MODEL), BF16),
                        pltpu.SemaphoreType.DMA(())],
    )
    return pl.pallas_call(
        _moe_kernel,
        grid_spec=grid_spec,
        out_shape=jax.ShapeDtypeStruct((MOE_ROWS, D_MODEL), F32),
        compiler_params=_cparams(("arbitrary", "arbitrary")),
        name="moe_experts",
    )(item_expert, item_block, item_rows, xs, p['exp_w1'], p['exp_w3'], p['exp_w2'])


ROUTE_TM = 256
R_E0, R_E1, R_P0, R_P1, R_RANK0, R_RANK1 = range(6)


def _route_kernel(lg_ref, o_ref, cnt_ref, carry_ref):
    i = pl.program_id(0)
    tm = ROUTE_TM

    @pl.when(i == 0)
    def _():
        carry_ref[...] = jnp.zeros(carry_ref.shape, F32)

    lane = lax.broadcasted_iota(jnp.int32, (tm, LANE), 1)
    lg = jnp.where(lane < N_EXPERTS, lg_ref[...], -jnp.inf)
    m0 = jnp.max(lg, -1, keepdims=True)
    e0 = jnp.min(jnp.where(lg == m0, lane, LANE), -1, keepdims=True)
    lg1 = jnp.where(lane == e0, -jnp.inf, lg)
    m1 = jnp.max(lg1, -1, keepdims=True)
    e1 = jnp.min(jnp.where(lg1 == m1, lane, LANE), -1, keepdims=True)
    t = jnp.exp(m1 - m0)
    p0 = 1.0 / (1.0 + t)
    p1 = t / (1.0 + t)
    hit = ((lane == e0) | (lane == e1)).astype(BF16)
    r_i = lax.broadcasted_iota(jnp.int32, (tm, tm), 0)
    c_i = lax.broadcasted_iota(jnp.int32, (tm, tm), 1)
    before = (c_i < r_i).astype(BF16)
    pref = jnp.dot(before, hit, preferred_element_type=F32) + carry_ref[0:1, :]
    rank0 = jnp.sum(jnp.where(lane == e0, pref, 0.0), -1, keepdims=True)
    rank1 = jnp.sum(jnp.where(lane == e1, pref, 0.0), -1, keepdims=True)
    carry_ref[0:1, :] = carry_ref[0:1, :] + jnp.sum(hit.astype(F32), axis=0, keepdims=True)
    rec = jnp.zeros((tm, LANE), F32)
    for k, v in ((R_E0, e0.astype(F32)), (R_E1, e1.astype(F32)), (R_P0, p0), (R_P1, p1),
                 (R_RANK0, rank0), (R_RANK1, rank1)):
        rec = jnp.where(lane == k, v, rec)
    o_ref[...] = rec
    cnt_ref[...] = carry_ref[...]


def _route_records(logits):
    return pl.pallas_call(
        _route_kernel,
        grid=(N_TOK // ROUTE_TM,),
        in_specs=[pl.BlockSpec((ROUTE_TM, LANE), lambda i: (i, 0))],
        out_specs=[pl.BlockSpec((ROUTE_TM, LANE), lambda i: (i, 0)),
                   pl.BlockSpec((SUBLANE, LANE), lambda i: (0, 0))],
        out_shape=[jax.ShapeDtypeStruct((N_TOK, LANE), F32),
                   jax.ShapeDtypeStruct((SUBLANE, LANE), F32)],
        scratch_shapes=[pltpu.VMEM((SUBLANE, LANE), F32)],
        compiler_params=_cparams(("arbitrary",)),
        name="moe_route",
    )(logits)


def _route(logits):
    rec, cnt = _route_records(logits)
    e_flat = rec[:, R_E0:R_E1 + 1].astype(jnp.int32).reshape(-1)
    rank = rec[:, R_RANK0:R_RANK1 + 1].astype(jnp.int32).reshape(-1)
    counts = cnt[0, :N_EXPERTS].astype(jnp.int32)
    blocks = (counts + MOE_CHUNK - 1) // MOE_CHUNK
    even = (counts + jnp.maximum(blocks, 1) - 1) // jnp.maximum(blocks, 1)
    per = (even + MOE_ROW_TILE - 1) // MOE_ROW_TILE * MOE_ROW_TILE
    per = jnp.where(blocks > 0, per, MOE_CHUNK)
    bend = jnp.cumsum(blocks)
    bstart = bend - blocks
    total = bend[-1]
    per_a = per[e_flat]
    pos = (bstart[e_flat] + rank // per_a) * MOE_CHUNK + rank % per_a
    w = jnp.arange(MOE_ITEMS, dtype=jnp.int32)
    w_eff = jnp.minimum(w, total - 1)
    item_expert = jnp.minimum(jnp.sum((w_eff[:, None] >= bend[None, :]).astype(jnp.int32), axis=1),
                              N_EXPERTS - 1)
    per_w = per[item_expert]
    item_rows = jnp.where(w < total,
                          jnp.clip(counts[item_expert] - (w - bstart[item_expert]) * per_w, 0, per_w),
                          0)
    return rec, pos.reshape(N_TOK, TOP_K), item_expert.astype(jnp.int32), \
        w_eff.astype(jnp.int32), item_rows.astype(jnp.int32)


DISPATCH_TM = 256


def _dispatch_kernel(pos_ref, ib_ref, ir_ref, u_ref, o_hbm, zero_ref, sem, zsem):
    i = pl.program_id(0)
    tm = DISPATCH_TM
    rt = MOE_ROW_TILE

    @pl.when(i == 0)
    def _():
        zero_ref[...] = jnp.zeros(zero_ref.shape, zero_ref.dtype)

        def tail_copy(w):
            last_tile = (ir_ref[w] - 1) // rt
            start = pl.multiple_of(ib_ref[w] * MOE_CHUNK + last_tile * rt, rt)
            return pltpu.make_async_copy(zero_ref, o_hbm.at[pl.ds(start, rt)], zsem)

        for w in range(MOE_ITEMS):
            pl.when(ir_ref[w] > 0)(lambda w=w: tail_copy(w).start())
        for w in range(MOE_ITEMS):
            pl.when(ir_ref[w] > 0)(lambda w=w: tail_copy(w).wait())

    def row_copy(r, dst_row):
        return pltpu.make_async_copy(u_ref.at[pl.ds(r, 1)], o_hbm.at[pl.ds(dst_row, 1)], sem)

    def start(r, _):
        a = (i * tm + r) * TOP_K
        for k in range(TOP_K):
            row_copy(r, pos_ref[a + k]).start(priority=k)
        return 0

    def wait(r, _):
        row_copy(0, 0).wait()
        return 0

    lax.fori_loop(0, tm, start, 0, unroll=8)
    lax.fori_loop(0, tm * TOP_K, wait, 0, unroll=8)


def _dispatch(pos_flat, item_block, item_rows, u):
    width = u.shape[1]
    grid_spec = pltpu.PrefetchScalarGridSpec(
        num_scalar_prefetch=3,
        grid=(N_TOK // DISPATCH_TM,),
        in_specs=[pl.BlockSpec((DISPATCH_TM, width), lambda i, pos, ib, ir: (i, 0))],
        out_specs=pl.BlockSpec(memory_space=pl.ANY),
        scratch_shapes=[pltpu.VMEM((MOE_ROW_TILE, width), u.dtype),
                        pltpu.SemaphoreType.DMA(()),
                        pltpu.SemaphoreType.DMA(())],
    )
    return pl.pallas_call(
        _dispatch_kernel,
        grid_spec=grid_spec,
        out_shape=jax.ShapeDtypeStruct((MOE_ROWS, width), u.dtype),
        compiler_params=_cparams(("arbitrary",)),
        name="moe_dispatch",
    )(pos_flat, item_block, item_rows, u)


def _postnorm_kernel(x_ref, y0_ref, y1_ref, rec_ref, gate_ref, g_ref, b_ref, op_ref, os_ref):
    i = pl.program_id(0)
    n_p = TOK_P // x_ref.shape[0]
    delta = (rec_ref[:, R_P0:R_P0 + 1] * y0_ref[...] + rec_ref[:, R_P1:R_P1 + 1] * y1_ref[...])
    y = DN_ALPHA * x_ref[...] + gate_ref[0] * delta
    out = _layer_norm_rows(y, g_ref[...], b_ref[...])

    @pl.when(i < n_p)
    def _():
        op_ref[...] = out

    @pl.when(i >= n_p)
    def _():
        os_ref[...] = out


def _postnorm(x, y0, y1, rec, mods, gate_chunk, g, b):
    tm = 256
    vec = pl.BlockSpec((1, D_MODEL), lambda i: (0, 0))
    rows = pl.BlockSpec((tm, D_MODEL), lambda i: (i, 0))
    return pl.pallas_call(
        _postnorm_kernel,
        grid=(N_TOK // tm,),
        in_specs=[rows, rows, rows, pl.BlockSpec((tm, LANE), lambda i: (i, 0)),
                  _mod_spec(tm, gate_chunk), vec, vec],
        out_specs=_token_row_specs(2, tm, D_MODEL),
        out_shape=[jax.ShapeDtypeStruct((TOK_P, D_MODEL), F32),
                   jax.ShapeDtypeStruct((TOK_S, D_MODEL), F32)],
        compiler_params=_cparams(("arbitrary",)),
        name="final_postnorm",
    )(x, y0, y1, rec, mods, g.reshape(1, D_MODEL), b.reshape(1, D_MODEL))


def kernel(x_prompt, x_sample, c, c_ctx, cache_l0_k, cache_l0_v, state_l0_lru, l0_ada_w, l0_ada_b, l0_w_in, l0_q_norm, l0_k_norm, l0_lru_conv_w, l0_lru_conv_b, l0_lru_lambda, l0_lru_w_r, l0_lru_b_r, l0_lru_w_i, l0_lru_b_i, l0_w_out, l0_ln1_g, l0_ln1_b, l0_ffn_w1, l0_ffn_w3, l0_ffn_w2, l0_ln2_g, l0_ln2_b, l1_ada_w, l1_ada_b, l1_w_in, l1_short_w, l1_short_b, l1_filt_w1, l1_filt_b1, l1_filt_f1, l1_filt_w2, l1_filt_b2, l1_filt_f2, l1_filt_w3, l1_filt_decay, l1_filt_bias, l1_w_out, l1_ln1_g, l1_ln1_b, l1_router, l1_exp_w1, l1_exp_w3, l1_exp_w2, l1_ln2_g, l1_ln2_b):
    x_in = [x_prompt.reshape(TOK_P, D_MODEL), x_sample.reshape(TOK_S, D_MODEL)]
    cond = jnp.concatenate([c_ctx[None, :], c, jnp.zeros((N_COND - 1 - DEC_BATCH, D_MODEL), F32)], axis=0)
    mods0 = _ada(cond, l0_ada_w, l0_ada_b)
    mods1 = _ada(cond, l1_ada_w, l1_ada_b)

    lru_p = dict(conv_w=l0_lru_conv_w, conv_b=l0_lru_conv_b, lam=l0_lru_lambda,
                 w_r=l0_lru_w_r, b_r=l0_lru_b_r, w_i=l0_lru_w_i, b_i=l0_lru_b_i)
    proj0 = _proj(x_in, mods0, l0_w_in, tn=512, name="l0_in_proj")
    attn, new_k, new_v = _attn_context(proj0, l0_q_norm, l0_k_norm)
    attn = _attn_latent(proj0, cache_l0_k, cache_l0_v, l0_q_norm, l0_k_norm, attn)
    lru, new_h = _lru(proj0, jnp.zeros((BATCH, 2, LRU_WIDTH), F32), lru_p,
                      seq=SEQ, batch=BATCH, row0=0, name="lru_context")
    lru, _ = _lru(proj0, state_l0_lru, lru_p,
                  seq=DEC_SEQ, batch=DEC_BATCH, row0=TOK_P, name="lru_latent", dst=lru)
    x = _outproj([attn, lru], l0_w_out, x_in, mods0, 2, l0_ln1_g, l0_ln1_b, name="l0_out_proj")[0]
    x = _ffn(x, mods0, dict(ffn_w1=l0_ffn_w1, ffn_w3=l0_ffn_w3, ffn_w2=l0_ffn_w2,
                            ln2_g=l0_ln2_g, ln2_b=l0_ln2_b))

    hy_p = dict(short_w=l1_short_w, short_b=l1_short_b, filt_w1=l1_filt_w1, filt_b1=l1_filt_b1,
                filt_f1=l1_filt_f1, filt_w2=l1_filt_w2, filt_b2=l1_filt_b2, filt_f2=l1_filt_f2,
                filt_w3=l1_filt_w3, filt_decay=l1_filt_decay, filt_bias=l1_filt_bias)
    proj1 = _proj([x], mods1, l1_w_in, tn=512, name="l1_in_proj")
    z = _hyena(proj1, hy_p, seq=SEQ, batch=BATCH, row0=0, nb=2, ct=1024, name="hyena_context")
    z = _hyena(proj1, hy_p, seq=DEC_SEQ, batch=DEC_BATCH, row0=TOK_P, nb=1, ct=512, name="hyena_latent",
               dst=z)
    router = jnp.pad(l1_router, ((0, 0), (0, LANE - N_EXPERTS)))
    x, u, logits = _outproj([z], l1_w_out, [x], mods1, 2, l1_ln1_g, l1_ln1_b, name="l1_out_proj",
                            router=router, router_chunks=(3, 4))
    rec, pos, item_expert, item_block, item_rows = _route(logits)
    xs = _dispatch(pos.reshape(-1), item_block, item_rows, u)
    ys = _moe_experts(xs, item_expert, item_block, item_rows,
                      dict(exp_w1=l1_exp_w1, exp_w3=l1_exp_w3, exp_w2=l1_exp_w2))
    y_prompt, y_sample = _postnorm(x, _rows(ys, pos[:, 0]), _rows(ys, pos[:, 1]), rec,
                                   mods1, 5, l1_ln2_g, l1_ln2_b)
    return (y_prompt.reshape(BATCH, SEQ, D_MODEL), y_sample.reshape(DEC_BATCH, DEC_SEQ, D_MODEL),
            new_k.reshape(BATCH, SEQ, N_KV_HEADS, HEAD_DIM),
            new_v.reshape(BATCH, SEQ, N_KV_HEADS, HEAD_DIM),
            new_h)
```
